```python
import jax, jax.numpy as jnp
from jax import lax
import numpy as np

D_MODEL = 2048
BATCH = 4
SEQ = 2048
DEPTH = 4

GRID_W = 64
CTX_LEN = 256
EPS = 1e-6

A_HEADS = 8
A_KDIM = 128
A_VDIM = 128
A_WIDTH = A_HEADS * A_KDIM
A_CHUNK = 64

POOL_WINDOWS = (2, 4, 8, 16)
B_WIDTH = 1024
B_GROUP = B_WIDTH // 4

C_HEADS = 8
C_HEAD_DIM = 128
C_WIDTH = C_HEADS * C_HEAD_DIM
NA_ROWS_MAX = 8
NA_COLS = 16
ROPE_THETA = 10000.0

N_BRANCH = 3
BRANCH_WIDTH = 1024
IN_SPLITS = (A_WIDTH, A_WIDTH, A_WIDTH, A_WIDTH, A_WIDTH, B_WIDTH, C_WIDTH, C_WIDTH, C_WIDTH, N_BRANCH * D_MODEL)
N_IN = 5 * A_WIDTH + B_WIDTH + 3 * C_WIDTH + N_BRANCH * D_MODEL

N_EXPERTS = 16
EXPERT_FF = 1024
EC_CAPACITY = 2

kernel_name = "hybrid_hgrn2_pool_natten_ec_dit"


def rmsnorm(t, gain):
    tf = t.astype(jnp.float32)
    y = tf * lax.rsqrt(jnp.mean(tf * tf, axis=-1, keepdims=True) + EPS) * gain.astype(jnp.float32)
    return y.astype(t.dtype)


def modulate(h, shift, scale):
    return h * (1 + scale) + shift


def split_columns(u):
    offs = np.cumsum(np.array(IN_SPLITS))[:-1].tolist()
    return jnp.split(u, offs, axis=-1)


def to_heads(t, n_heads):
    b, l, w = t.shape
    return t.reshape(b, l, n_heads, w // n_heads).transpose(0, 2, 1, 3)


def flip_seq(t):
    return jnp.flip(t, axis=2)


def hgrn_gates(z, lb):
    z = z.astype(jnp.float32)
    lb = lb.astype(jnp.float32)
    logf = jnp.logaddexp(jnp.log(lb), jnp.log1p(-lb) + jax.nn.log_sigmoid(z))
    k = (1 - lb) * jax.nn.sigmoid(-z)
    return to_heads(logf, A_HEADS), to_heads(k, A_HEADS)


def hgrn_chunk_scan(q, logf, k, v, s0):
    bsz, h, n, dk = q.shape
    dv = v.shape[-1]
    nc = n // A_CHUNK

    def to_chunks(t):
        return jnp.moveaxis(t.reshape(bsz, h, nc, A_CHUNK, t.shape[-1]), 2, 0)

    lower = jnp.tril(jnp.ones((A_CHUNK, A_CHUNK), dtype=bool))

    def step(s, blk):
        qc, gc, kc, vc = blk
        b = jnp.cumsum(gc, axis=2)
        diff = jnp.where(lower[:, :, None], b[:, :, :, None, :] - b[:, :, None, :, :], -jnp.inf)
        scores = jnp.einsum('bhtk,bhsk,bhtsk->bhts', qc, kc, jnp.exp(diff))
        o = jnp.einsum('bhtk,bhkv->bhtv', qc * jnp.exp(b), s) + jnp.einsum('bhts,bhsv->bhtv', scores, vc)
        b_last = b[:, :, -1:, :]
        s_new = jnp.exp(b_last[:, :, 0, :, None]) * s + jnp.einsum('bhsk,bhsv->bhkv', kc * jnp.exp(b_last - b), vc)
        return s_new, o

    s_fin, o = lax.scan(step, s0, (to_chunks(q), to_chunks(logf), to_chunks(k), to_chunks(v)))
    o = jnp.moveaxis(o, 0, 2).reshape(bsz, h, n, dv)
    return o, s_fin


def hgrn_final_state(logf, k, v):
    b = jnp.cumsum(logf, axis=2)
    return jnp.einsum('bhsk,bhsv->bhkv', k * jnp.exp(b[:, :, -1:] - b), v)


def hgrn_bidir(u_q, u_ff, u_fb, u_i, lb_f, lb_b, s_f0, s_b0):
    q = to_heads(jax.nn.silu(u_q), A_HEADS)
    v = to_heads(u_i, A_HEADS)
    lf_f, k_f = hgrn_gates(u_ff, lb_f)
    lf_b, k_b = hgrn_gates(u_fb, lb_b)
    o_f, s_f = hgrn_chunk_scan(q, lf_f, k_f, v, s_f0)
    o_b, s_b = hgrn_chunk_scan(flip_seq(q), flip_seq(lf_b), flip_seq(k_b), flip_seq(v), s_b0)
    return o_f + flip_seq(o_b), s_f, s_b


def hgrn_context_states(u_ff, u_fb, u_i, lb_f, lb_b):
    v = to_heads(u_i, A_HEADS)
    lf_f, k_f = hgrn_gates(u_ff, lb_f)
    lf_b, k_b = hgrn_gates(u_fb, lb_b)
    s_f = hgrn_final_state(lf_f, k_f, v)
    s_b = hgrn_final_state(flip_seq(lf_b), flip_seq(k_b), flip_seq(v))
    return s_f, s_b


def hgrn_readout(o, g, gain):
    bsz, h, l, dv = o.shape
    o = o.astype(jnp.float32).transpose(0, 2, 1, 3)
    o = o * lax.rsqrt(jnp.mean(o * o, axis=-1, keepdims=True) + EPS) * gain.astype(jnp.float32).reshape(h, dv)
    return (o.reshape(bsz, l, h * dv) * jax.nn.silu(g.astype(jnp.float32))).astype(g.dtype)


def multiscale_pool(u, w_pool, scale):
    bsz, n, _ = u.shape
    uf = u.astype(jnp.float32)
    csum = jnp.concatenate([jnp.zeros((bsz, 1, B_WIDTH), jnp.float32), jnp.cumsum(uf, axis=1)], axis=1)
    pos = np.arange(n)
    diffs = []
    for gi, w in enumerate(POOL_WINDOWS):
        lo = np.clip(pos - w // 2, 0, n - 1)
        hi = np.clip(pos + w // 2 - 1, 0, n - 1)
        cnt = (hi - lo + 1).astype(np.float32)[None, :, None]
        seg = csum[:, :, gi * B_GROUP:(gi + 1) * B_GROUP]
        mean = (seg[:, hi + 1] - seg[:, lo]) / cnt
        diffs.append(mean - uf[:, :, gi * B_GROUP:(gi + 1) * B_GROUP])
    d = jnp.stack(diffs, axis=2)
    y = jnp.einsum('blgc,gce->blge', d, w_pool.astype(jnp.float32)).reshape(bsz, n, B_WIDTH)
    return (y * scale.astype(jnp.float32)).astype(u.dtype)


def axial_rope(t):
    n = t.shape[1]
    pos = np.arange(n)
    row = jnp.asarray(pos // GRID_W, jnp.float32)
    col = jnp.asarray(pos % GRID_W, jnp.float32)
    half = C_HEAD_DIM // 2
    inv_freq = ROPE_THETA ** (-jnp.arange(0, half, 2, dtype=jnp.float32) / half)

    def rot(u, p):
        ang = p[:, None] * inv_freq
        cos = jnp.cos(ang)[None, :, None, :]
        sin = jnp.sin(ang)[None, :, None, :]
        u1, u2 = u[..., :half // 2], u[..., half // 2:]
        return jnp.concatenate([u1 * cos - u2 * sin, u2 * cos + u1 * sin], axis=-1)

    tf = t.astype(jnp.float32)
    return jnp.concatenate([rot(tf[..., :half], row), rot(tf[..., half:], col)], axis=-1).astype(t.dtype)


def neighborhood_attention(q, k, v, k_ctx, v_ctx, rpb):
    bsz, n, h, dh = q.shape
    rows = n // GRID_W
    kr = min(NA_ROWS_MAX, rows)
    r = np.arange(rows)
    key_rows = np.clip(r - kr // 2, 0, rows - kr)[:, None] + np.arange(kr)[None, :]
    qcol = np.arange(GRID_W)
    col_start = np.clip(qcol - NA_COLS // 2, 0, GRID_W - NA_COLS)
    kcol = np.arange(GRID_W)
    col_mask = (kcol[None, :] >= col_start[:, None]) & (kcol[None, :] < col_start[:, None] + NA_COLS)
    mask = np.broadcast_to(col_mask[:, None, :], (GRID_W, kr, GRID_W)).reshape(GRID_W, kr * GRID_W)
    dr_idx = key_rows - r[:, None] + NA_ROWS_MAX - 1
    dc_idx = np.clip(kcol[None, :] - qcol[:, None] + NA_COLS - 1, 0, 2 * NA_COLS - 2)
    bias = rpb.astype(jnp.float32)[:, dr_idx][..., dc_idx]
    bias = bias.transpose(0, 1, 3, 2, 4).reshape(h, rows, GRID_W, kr * GRID_W)

    scale = dh ** -0.5
    qg = q.reshape(bsz, rows, GRID_W, h, dh)
    kg = k.reshape(bsz, rows, GRID_W, h, dh)[:, key_rows].reshape(bsz, rows, kr * GRID_W, h, dh)
    vg = v.reshape(bsz, rows, GRID_W, h, dh)[:, key_rows].reshape(bsz, rows, kr * GRID_W, h, dh)
    s_loc = jnp.einsum('brqhd,brnhd->bhrqn', qg, kg).astype(jnp.float32) * scale + bias[None]
    s_loc = jnp.where(mask, s_loc, -jnp.inf)
    s_ctx = jnp.einsum('brqhd,bchd->bhrqc', qg, k_ctx).astype(jnp.float32) * scale
    p = jax.nn.softmax(jnp.concatenate([s_loc, s_ctx], axis=-1), axis=-1).astype(v.dtype)
    n_loc = kr * GRID_W
    out = jnp.einsum('bhrqn,brnhd->brqhd', p[..., :n_loc], vg) + jnp.einsum('bhrqc,bchd->brqhd', p[..., n_loc:], v_ctx)
    return out.reshape(bsz, n, h * dh)


def context_attention(q, k, v):
    bsz, lc, h, dh = q.shape
    s = jnp.einsum('bqhd,bkhd->bhqk', q, k).astype(jnp.float32) * (dh ** -0.5)
    p = jax.nn.softmax(s, axis=-1).astype(v.dtype)
    return jnp.einsum('bhqk,bkhd->bqhd', p, v).reshape(bsz, lc, h * dh)


def merge_branches(ys, u_gate, w_branch, w_out):
    bsz, l, _ = u_gate.shape
    y = jnp.stack(ys, axis=2)
    proj = jnp.einsum('bljw,jwd->bljd', y, w_branch)
    gates = jax.nn.sigmoid(u_gate.reshape(bsz, l, N_BRANCH, D_MODEL))
    return jnp.einsum('bld,de->ble', jnp.sum(gates * proj, axis=2), w_out)


def expert_choice_ffn(h, w_router, w_gate, w_up, w_down):
    bsz, n, d = h.shape
    cap = EC_CAPACITY * n // N_EXPERTS
    aff = jax.nn.softmax((h @ w_router).astype(jnp.float32), axis=-1)
    g, idx = lax.top_k(jnp.swapaxes(aff, 1, 2), cap)
    xg = jax.vmap(lambda hb, ib: hb[ib])(h, idx)
    hid = jax.nn.silu(jnp.einsum('becd,edf->becf', xg, w_gate)) * jnp.einsum('becd,edf->becf', xg, w_up)
    y = jnp.einsum('becf,efd->becd', hid, w_down) * g[..., None].astype(h.dtype)
    return jax.vmap(lambda yb, ib: jnp.zeros((n, d), h.dtype).at[ib.reshape(-1)].add(yb.reshape(-1, d)))(y, idx)


def setup_inputs(seed: int = 0) -> dict:
    key = jax.random.key(seed)
    ks = jax.random.split(key, 21)
    f32 = jnp.float32

    def nrm(k, shape, s):
        return jax.random.normal(k, shape, f32) * s

    L, D = DEPTH, D_MODEL
    return {
        "x": nrm(ks[0], (BATCH, SEQ, D), 1.0),
        "c": nrm(ks[1], (BATCH, D), 1.0),
        "ctx": nrm(ks[2], (BATCH, CTX_LEN, D), 1.0),
        "c_ctx": nrm(ks[3], (D,), 1.0),
        "w_mod": nrm(ks[4], (L, D, 6 * D), 0.5 * D ** -0.5),
        "b_mod": nrm(ks[5], (L, 6 * D), 0.02),
        "g_norm1": 1.0 + nrm(ks[6], (L, D), 0.05),
        "w_in": nrm(ks[7], (L, D, N_IN), D ** -0.5),
        "lb_param": nrm(ks[8], (2, L, A_WIDTH), 1.0),
        "g_hgrn": 1.0 + nrm(ks[9], (L, A_WIDTH), 0.05),
        "w_pool": nrm(ks[10], (L, 4, B_GROUP, B_GROUP), B_GROUP ** -0.5),
        "pool_scale": 1.0 + nrm(ks[11], (L, B_WIDTH), 0.05),
        "rpb": nrm(ks[12], (L, C_HEADS, 2 * NA_ROWS_MAX - 1, 2 * NA_COLS - 1), 0.1),
        "w_branch": nrm(ks[13], (L, N_BRANCH, BRANCH_WIDTH, D), BRANCH_WIDTH ** -0.5),
        "w_out": nrm(ks[14], (L, D, D), D ** -0.5),
        "g_norm2": 1.0 + nrm(ks[15], (L, D), 0.05),
        "w_router": nrm(ks[16], (L, D, N_EXPERTS), D ** -0.5),
        "w_gate_e": nrm(ks[17], (L, N_EXPERTS, D, EXPERT_FF), D ** -0.5),
        "w_up_e": nrm(ks[18], (L, N_EXPERTS, D, EXPERT_FF), D ** -0.5),
        "w_down_e": nrm(ks[19], (L, N_EXPERTS, EXPERT_FF, D), EXPERT_FF ** -0.5),
        "g_final": 1.0 + nrm(ks[20], (D,), 0.05),
    }


def reference(x, c, ctx, c_ctx, w_mod, b_mod, g_norm1, w_in, lb_param, g_hgrn, w_pool, pool_scale, rpb,
              w_branch, w_out, g_norm2, w_router, w_gate_e, w_up_e, w_down_e, g_final):
    bsz, n, _ = x.shape
    lc = ctx.shape[1]
    lb_all = jnp.cumsum(jax.nn.softmax(lb_param.astype(jnp.float32), axis=1), axis=1)
    lb_all = lb_all - lb_all[:, :1]
    sc = jax.nn.silu(c)
    scc = jax.nn.silu(c_ctx)
    xc = ctx
    for l in range(DEPTH):
        last = l == DEPTH - 1
        mod_x = jnp.split((sc @ w_mod[l] + b_mod[l])[:, None, :], 6, axis=-1)
        mod_c = jnp.split(scc @ w_mod[l] + b_mod[l], 6, axis=-1)
        lb_f = lb_all[0, l]
        lb_b = lb_all[1, l]

        hc = modulate(rmsnorm(xc, g_norm1[l]), mod_c[0], mod_c[1])
        cq_a, cf_f, cf_b, ci_a, cg_a, cu_b, cq_c, ck_c, cv_c, cu_gate = split_columns(hc @ w_in[l])
        ck_h = ck_c.reshape(bsz, lc, C_HEADS, C_HEAD_DIM)
        cv_h = cv_c.reshape(bsz, lc, C_HEADS, C_HEAD_DIM)
        if last:
            s_f, s_b = hgrn_context_states(cf_f, cf_b, ci_a, lb_f, lb_b)
        else:
            zero = jnp.zeros((bsz, A_HEADS, A_KDIM, A_VDIM), jnp.float32)
            co_a, s_f, s_b = hgrn_bidir(cq_a, cf_f, cf_b, ci_a, lb_f, lb_b, zero, zero)
            cy_a = hgrn_readout(co_a, cg_a, g_hgrn[l])
            cy_b = multiscale_pool(cu_b, w_pool[l], pool_scale[l])
            cy_c = context_attention(cq_c.reshape(bsz, lc, C_HEADS, C_HEAD_DIM), ck_h, cv_h)
            xc_mid = xc + mod_c[2] * merge_branches((cy_a, cy_b, cy_c), cu_gate, w_branch[l], w_out[l])

        h = modulate(rmsnorm(x, g_norm1[l]), mod_x[0], mod_x[1])
        q_a, f_f, f_b, i_a, g_a, u_b, q_c, k_c, v_c, u_gate = split_columns(h @ w_in[l])
        o_a, _, _ = hgrn_bidir(q_a, f_f, f_b, i_a, lb_f, lb_b, s_f, s_b)
        y_a = hgrn_readout(o_a, g_a, g_hgrn[l])
        y_b = multiscale_pool(u_b, w_pool[l], pool_scale[l])
        qh = axial_rope(q_c.reshape(bsz, n, C_HEADS, C_HEAD_DIM))
        kh = axial_rope(k_c.reshape(bsz, n, C_HEADS, C_HEAD_DIM))
        vh = v_c.reshape(bsz, n, C_HEADS, C_HEAD_DIM)
        y_c = neighborhood_attention(qh, kh, vh, ck_h, cv_h, rpb[l])
        x = x + mod_x[2] * merge_branches((y_a, y_b, y_c), u_gate, w_branch[l], w_out[l])
        h2 = modulate(rmsnorm(x, g_norm2[l]), mod_x[3], mod_x[4])
        x = x + mod_x[5] * expert_choice_ffn(h2, w_router[l], w_gate_e[l], w_up_e[l], w_down_e[l])

        if not last:
            hc2 = modulate(rmsnorm(xc_mid, g_norm2[l]), mod_c[3], mod_c[4])
            xc = xc_mid + mod_c[5] * expert_choice_ffn(hc2, w_router[l], w_gate_e[l], w_up_e[l], w_down_e[l])
    return rmsnorm(x, g_final)
```

```python
import functools

import numpy as np
import jax
import jax.numpy as jnp
from jax import lax
from jax.experimental import pallas as pl
from jax.experimental.pallas import tpu as pltpu

F32 = jnp.float32
BF16 = jnp.bfloat16
EPS = 1e-6
HEAD_DIM = 128
CHUNK = 64
GRID_W = 64
POOL_WINDOWS = (2, 4, 8, 16)
POOL_PAD = 16
NA_ROWS_MAX = 8
NA_COLS = 16
NA_QROWS = 4
NA_KROWS = 12
ROPE_THETA = 10000.0
EC_CAPACITY = 2
NEG = -1e30
ROW_TILE = 256
VMEM_LIMIT = 56 * 2 ** 20


def _params(*sem):
    return pltpu.CompilerParams(dimension_semantics=sem, vmem_limit_bytes=VMEM_LIMIT)


def _sigmoid(x):
    return 1.0 / (1.0 + jnp.exp(-x))


def _lane_tile(n, limit):
    t = min(limit, n) // HEAD_DIM * HEAD_DIM
    while n % t:
        t -= HEAD_DIM
    return t


def _mod_kernel(c_ref, w_ref, b_ref, o_ref):
    c = c_ref[...]
    sc = (c * _sigmoid(c)).astype(BF16)
    o_ref[...] = jnp.dot(sc, w_ref[...].astype(BF16), preferred_element_type=F32) + b_ref[...]


def _modulation(c8, w_mod, b_mod):
    depth, d, n6 = w_mod.shape
    tn = _lane_tile(n6, 1024)
    return pl.pallas_call(
        _mod_kernel,
        grid=(depth, n6 // tn),
        in_specs=[pl.BlockSpec((8, d), lambda l, j: (0, 0)),
                  pl.BlockSpec((None, d, tn), lambda l, j: (l, 0, j)),
                  pl.BlockSpec((None, 1, tn), lambda l, j: (l, 0, j))],
        out_specs=pl.BlockSpec((None, 8, tn), lambda l, j: (l, 0, j)),
        out_shape=jax.ShapeDtypeStruct((depth, 8, n6), F32),
        compiler_params=_params("arbitrary", "arbitrary"),
        name="modulation",
    )(c8, w_mod, b_mod.reshape(depth, 1, n6))


def _norm_body(x_ref, g_ref, mod_ref, shift_idx):
    x = x_ref[...]
    y = x * lax.rsqrt(jnp.mean(x * x, axis=-1, keepdims=True) + EPS) * g_ref[...]
    if mod_ref is None:
        return y
    shift = mod_ref[shift_idx:shift_idx + 1, :]
    scale = mod_ref[shift_idx + 1:shift_idx + 2, :]
    return y * (1.0 + scale) + shift


def _norm_kernel(x_ref, g_ref, mod_ref, o_ref, *, shift_idx):
    o_ref[...] = _norm_body(x_ref, g_ref, mod_ref, shift_idx).astype(o_ref.dtype)


def _norm_router_kernel(x_ref, g_ref, mod_ref, wr_ref, o_ref, aff_ref, *, shift_idx):
    h = _norm_body(x_ref, g_ref, mod_ref, shift_idx)
    o_ref[...] = h.astype(o_ref.dtype)
    logits = jnp.dot(h, wr_ref[...], precision=lax.Precision.HIGHEST, preferred_element_type=F32)
    logits = logits[:, :aff_ref.shape[1]]
    e = jnp.exp(logits - jnp.max(logits, axis=-1, keepdims=True))
    aff_ref[...] = e / jnp.sum(e, axis=-1, keepdims=True)


def _final_norm_kernel(x_ref, g_ref, o_ref):
    o_ref[...] = _norm_body(x_ref, g_ref, None, 0)


def _norm_modulate(x, gain, modt, shift_idx, rows, w_router=None):
    d = x.shape[1]
    tm = ROW_TILE
    in_specs = [pl.BlockSpec((tm, d), lambda i: (i, 0)),
                pl.BlockSpec((1, d), lambda i: (0, 0)),
                pl.BlockSpec((None, 6, d), lambda i: (i, 0, 0))]
    out_specs = pl.BlockSpec((tm, d), lambda i: (i, 0))
    out_shape = jax.ShapeDtypeStruct((rows, d), BF16)
    if w_router is None:
        return pl.pallas_call(
            functools.partial(_norm_kernel, shift_idx=shift_idx),
            grid=(rows // tm,), in_specs=in_specs, out_specs=out_specs, out_shape=out_shape,
            compiler_params=_params("arbitrary"), name="norm_modulate",
        )(x, gain.reshape(1, d), modt)
    ne = w_router.shape[1]
    lanes = -(-ne // HEAD_DIM) * HEAD_DIM
    w_router = jnp.pad(w_router, ((0, 0), (0, lanes - ne)))
    return pl.pallas_call(
        functools.partial(_norm_router_kernel, shift_idx=shift_idx),
        grid=(rows // tm,),
        in_specs=in_specs + [pl.BlockSpec((d, lanes), lambda i: (0, 0))],
        out_specs=[out_specs, pl.BlockSpec((tm, ne), lambda i: (i, 0))],
        out_shape=[out_shape, jax.ShapeDtypeStruct((rows, ne), F32)],
        compiler_params=_params("arbitrary"), name="norm_modulate_router",
    )(x, gain.reshape(1, d), modt, w_router)


def _final_norm(x, gain, rows):
    d = x.shape[1]
    tm = ROW_TILE
    return pl.pallas_call(
        _final_norm_kernel,
        grid=(rows // tm,),
        in_specs=[pl.BlockSpec((tm, d), lambda i: (i, 0)), pl.BlockSpec((1, d), lambda i: (0, 0))],
        out_specs=pl.BlockSpec((tm, d), lambda i: (i, 0)),
        out_shape=jax.ShapeDtypeStruct((rows, d), F32),
        compiler_params=_params("arbitrary"), name="final_norm",
    )(x, gain.reshape(1, d))


def _mm_kernel(a_ref, w_ref, o_ref, wbf_ref):
    @pl.when(pl.program_id(1) == 0)
    def _():
        wbf_ref[...] = w_ref[...].astype(BF16)
    o_ref[...] = jnp.dot(a_ref[...], wbf_ref[...], preferred_element_type=F32).astype(o_ref.dtype)


def _mm_residual_kernel(a_ref, w_ref, x_ref, mod_ref, o_ref, wbf_ref, *, gate_idx):
    @pl.when(pl.program_id(1) == 0)
    def _():
        wbf_ref[...] = w_ref[...].astype(BF16)
    acc = jnp.dot(a_ref[...], wbf_ref[...], preferred_element_type=F32)
    o_ref[...] = x_ref[...] + mod_ref[gate_idx:gate_idx + 1, :] * acc


def _row_tile(n, ctx_rows):
    tm = 1024
    while n % tm or ctx_rows % tm:
        tm //= 2
    return tm


def _matmul(a, w_all, layer, col0, ncols, out_dtype, tm, tn):
    m, k = a.shape
    off = col0 // tn
    return pl.pallas_call(
        _mm_kernel,
        grid=(ncols // tn, m // tm),
        in_specs=[pl.BlockSpec((tm, k), lambda j, i: (i, 0)),
                  pl.BlockSpec((None, k, tn), lambda j, i: (layer, 0, j + off))],
        out_specs=pl.BlockSpec((tm, tn), lambda j, i: (i, j)),
        out_shape=jax.ShapeDtypeStruct((m, ncols), out_dtype),
        scratch_shapes=[pltpu.VMEM((k, tn), BF16)],
        compiler_params=_params("arbitrary", "arbitrary"), name="matmul",
    )(a, w_all)


def _matmul_residual(a, w_all, layer, x, modt, gate_idx, rows, tm, tn):
    k = a.shape[1]
    d = w_all.shape[2]
    per = tm // ROW_TILE
    return pl.pallas_call(
        functools.partial(_mm_residual_kernel, gate_idx=gate_idx),
        grid=(d // tn, rows // tm),
        in_specs=[pl.BlockSpec((tm, k), lambda j, i: (i, 0)),
                  pl.BlockSpec((None, k, tn), lambda j, i: (layer, 0, j)),
                  pl.BlockSpec((tm, tn), lambda j, i: (i, j)),
                  pl.BlockSpec((None, 6, tn), lambda j, i: (i * per, 0, j))],
        out_specs=pl.BlockSpec((tm, tn), lambda j, i: (i, j)),
        out_shape=jax.ShapeDtypeStruct((rows, d), F32),
        scratch_shapes=[pltpu.VMEM((k, tn), BF16)],
        compiler_params=_params("arbitrary", "arbitrary"), name="matmul_residual",
    )(a, w_all, x, modt)


def _merge_kernel(ya_ref, yb_ref, yc_ref, g0_ref, g1_ref, g2_ref, w_ref, o_ref, wbf_ref):
    @pl.when(pl.program_id(1) == 0)
    def _():
        wbf_ref[...] = w_ref[...].astype(BF16)
    acc = None
    for j, (y_ref, g_ref) in enumerate(((ya_ref, g0_ref), (yb_ref, g1_ref), (yc_ref, g2_ref))):
        p = jnp.dot(y_ref[...], wbf_ref[j], preferred_element_type=F32)
        t = _sigmoid(g_ref[...].astype(F32)) * p
        acc = t if acc is None else acc + t
    o_ref[...] = acc.astype(o_ref.dtype)


def _merge(ya, yb, yc, ub, gate_col0, w_branch, layer, rows, tm, tn):
    bw = ya.shape[1]
    d = w_branch.shape[3]
    y_spec = pl.BlockSpec((tm, bw), lambda j, i: (i, 0))

    def gate_spec(k):
        off = (gate_col0 + k * d) // tn
        return pl.BlockSpec((tm, tn), lambda j, i: (i, off + j))

    return pl.pallas_call(
        _merge_kernel,
        grid=(d // tn, rows // tm),
        in_specs=[y_spec, y_spec, y_spec, gate_spec(0), gate_spec(1), gate_spec(2),
                  pl.BlockSpec((None, 3, bw, tn), lambda j, i: (layer, 0, 0, j))],
        out_specs=pl.BlockSpec((tm, tn), lambda j, i: (i, j)),
        out_shape=jax.ShapeDtypeStruct((rows, d), BF16),
        scratch_shapes=[pltpu.VMEM((3, bw, tn), BF16)],
        compiler_params=_params("arbitrary", "arbitrary"), name="merge",
    )(ya, yb, yc, ub, ub, ub, w_branch)


_HGRN_BLOCKS = (32, 16, 8, 4, 2, 1)


def _hgrn_constants():
    c = CHUNK
    t = np.arange(c)
    tri = (t[:, None] >= t[None, :]).astype(np.float32)
    cms, masks = [tri], []
    for m in _HGRN_BLOCKS:
        ref = (t // (2 * m)) * (2 * m) + m - 1
        cms.append(tri[ref])
        same = (t[:, None] // (2 * m)) == (t[None, :] // (2 * m))
        masks.append(same & ((t[:, None] % (2 * m)) >= m) & ((t[None, :] % (2 * m)) < m))
    masks.append(np.eye(c, dtype=bool))
    cm_f = np.concatenate(cms, axis=0)
    mask_f = np.stack(masks).astype(np.float32)
    cm_b = cm_f.reshape(-1, c, c)[:, ::-1, ::-1].reshape(-1, c)
    mask_b = mask_f[:, ::-1, ::-1]
    return np.stack([cm_f, cm_b]), np.stack([mask_f, mask_b])


def _hgrn_chunk(q, z, v, lbp, cm, masks, st, last_row):
    nl = len(_HGRN_BLOCKS)
    e = jnp.exp(-jnp.abs(z))
    l1p = jnp.log1p(e)
    log_sig = jnp.minimum(z, 0.0) - l1p
    k = lbp[2:3, :] * jnp.exp(jnp.minimum(-z, 0.0) - l1p)
    a = lbp[0:1, :]
    cc = lbp[1:2, :] + log_sig
    logf = jnp.maximum(a, cc) + jnp.log1p(jnp.exp(-jnp.abs(a - cc)))
    g1 = logf.astype(BF16)
    r1 = logf - g1.astype(F32)
    g2 = r1.astype(BF16)
    g3 = (r1 - g2.astype(F32)).astype(BF16)
    rr = jnp.dot(cm, jnp.concatenate([g1, g2, g3], axis=1), preferred_element_type=F32)
    rr = rr[:, 0:HEAD_DIM] + rr[:, HEAD_DIM:2 * HEAD_DIM] + rr[:, 2 * HEAD_DIM:3 * HEAD_DIM]
    b = rr[0:CHUNK]
    refs = rr[CHUNK:].reshape(nl, CHUNK, HEAD_DIM)
    tot = b[last_row:last_row + 1, :]
    eq = jnp.exp(jnp.minimum(b[None] - refs, 0.0))
    ek = jnp.exp(jnp.minimum(refs - b[None], 0.0))
    qs = jnp.concatenate([q[None] * eq, q[None]], axis=0).astype(BF16)
    ks = jnp.concatenate([k[None] * ek, k[None]], axis=0).astype(BF16)
    sc = jnp.einsum("ltk,lsk->lts", qs, ks, preferred_element_type=F32)
    amat = jnp.sum(sc * masks, axis=0)
    vb = v.astype(BF16)
    o = jnp.dot(amat.astype(BF16), vb, preferred_element_type=F32)
    o = o + lax.dot_general((q * jnp.exp(b)).astype(BF16), st.astype(BF16), (((1,), (1,)), ((), ())),
                            preferred_element_type=F32)
    kd = (k * jnp.exp(tot - b)).astype(BF16)
    st_new = st * jnp.exp(tot) + lax.dot_general(vb, kd, (((0,), (0,)), ((), ())), preferred_element_type=F32)
    return o, st_new


def _hgrn_kernel(q_ref, ff_ref, fb_ref, i_ref, g_ref, lbf_ref, lbb_ref, gain_ref, cm_ref, mask_ref,
                 sf0_ref, sb0_ref, y_ref, sf_ref, sb_ref, of_scr, ob_scr, st_scr, *, nc):
    st_scr[0] = sf0_ref[...]
    st_scr[1] = sb0_ref[...]

    def body(c, carry):
        for direction, (z_ref, lb_ref, o_scr) in enumerate(((ff_ref, lbf_ref, of_scr), (fb_ref, lbb_ref, ob_scr))):
            chunk = c if direction == 0 else nc - 1 - c
            rows = pl.ds(pl.multiple_of(chunk * CHUNK, CHUNK), CHUNK)
            qr = q_ref[rows, :]
            o, st_new = _hgrn_chunk(qr * _sigmoid(qr), z_ref[rows, :], i_ref[rows, :], lb_ref[...],
                                    cm_ref[direction], mask_ref[direction], st_scr[direction],
                                    CHUNK - 1 if direction == 0 else 0)
            o_scr[rows, :] = o
            st_scr[direction] = st_new
        return carry

    lax.fori_loop(0, nc, body, 0)
    sf_ref[...] = st_scr[0]
    sb_ref[...] = st_scr[1]
    o = of_scr[...] + ob_scr[...]
    o = o * lax.rsqrt(jnp.mean(o * o, axis=-1, keepdims=True) + EPS) * gain_ref[...]
    g = g_ref[...]
    y_ref[...] = (o * (g * _sigmoid(g))).astype(y_ref.dtype)


def _hgrn(ua, row_block0, n, bsz, heads, lbp, gain, sf0, sb0):
    cm, masks = _hgrn_constants()
    cm = jnp.asarray(cm, BF16)
    masks = jnp.asarray(masks, F32)
    nc = n // CHUNK

    def col(k):
        return pl.BlockSpec((n, HEAD_DIM), lambda b, h: (row_block0 + b, k * heads + h))

    lb_spec = lambda d: pl.BlockSpec((None, None, 3, HEAD_DIM), lambda b, h: (d, h, 0, 0))
    st_spec = pl.BlockSpec((None, None, HEAD_DIM, HEAD_DIM), lambda b, h: (b, h, 0, 0))
    st_shape = jax.ShapeDtypeStruct((bsz, heads, HEAD_DIM, HEAD_DIM), F32)
    return pl.pallas_call(
        functools.partial(_hgrn_kernel, nc=nc),
        grid=(bsz, heads),
        in_specs=[col(0), col(1), col(2), col(3), col(4), lb_spec(0), lb_spec(1),
                  pl.BlockSpec((None, 1, HEAD_DIM), lambda b, h: (h, 0, 0)),
                  pl.BlockSpec(cm.shape, lambda b, h: (0, 0, 0)),
                  pl.BlockSpec(masks.shape, lambda b, h: (0, 0, 0, 0)),
                  st_spec, st_spec],
        out_specs=[pl.BlockSpec((n, HEAD_DIM), lambda b, h: (b, h)), st_spec, st_spec],
        out_shape=[jax.ShapeDtypeStruct((bsz * n, heads * HEAD_DIM), BF16), st_shape, st_shape],
        scratch_shapes=[pltpu.VMEM((n, HEAD_DIM), F32), pltpu.VMEM((n, HEAD_DIM), F32),
                        pltpu.VMEM((2, HEAD_DIM, HEAD_DIM), F32)],
        compiler_params=_params("arbitrary", "arbitrary"), name="hgrn2",
    )(ua, ua, ua, ua, ua, lbp, lbp, gain, cm, masks, sf0, sb0)


def _pool_kernel(u_ref, w_ref, s_ref, y_ref, pad_ref, *, n, group):
    pos = lax.broadcasted_iota(jnp.int32, (n, 1), 0)
    zeros = jnp.zeros((POOL_PAD, pad_ref.shape[1]), F32)
    pad_ref[0:POOL_PAD, :] = zeros
    pad_ref[POOL_PAD + n:2 * POOL_PAD + n, :] = zeros
    pad_ref[POOL_PAD:POOL_PAD + n, :] = u_ref[...].astype(F32)
    for gi, w in enumerate(POOL_WINDOWS):
        cols = slice(gi * group, (gi + 1) * group)
        acc = None
        for dlt in range(-(w // 2), w // 2):
            t = pad_ref[POOL_PAD + dlt:POOL_PAD + dlt + n, cols]
            acc = t if acc is None else acc + t
        lo = jnp.maximum(pos - w // 2, 0)
        hi = jnp.minimum(pos + w // 2 - 1, n - 1)
        cnt = (hi - lo + 1).astype(F32)
        dd = acc / cnt - pad_ref[POOL_PAD:POOL_PAD + n, cols]
        y = jnp.dot(dd.astype(BF16), w_ref[gi].astype(BF16), preferred_element_type=F32)
        y_ref[:, cols] = (y * s_ref[:, cols]).astype(y_ref.dtype)


def _pool(ub, row_block0, n, bsz, w_pool, scale, layer):
    group = w_pool.shape[-1]
    width = 4 * group
    return pl.pallas_call(
        functools.partial(_pool_kernel, n=n, group=group),
        grid=(bsz,),
        in_specs=[pl.BlockSpec((n, width), lambda b: (row_block0 + b, 0)),
                  pl.BlockSpec((None, 4, group, group), lambda b: (layer, 0, 0, 0)),
                  pl.BlockSpec((None, 1, width), lambda b: (layer, 0, 0))],
        out_specs=pl.BlockSpec((n, width), lambda b: (b, 0)),
        out_shape=jax.ShapeDtypeStruct((bsz * n, width), BF16),
        scratch_shapes=[pltpu.VMEM((n + 2 * POOL_PAD, width), F32)],
        compiler_params=_params("arbitrary"), name="pool",
    )(ub, w_pool, scale.reshape(scale.shape[0], 1, width))


def _na_tables(n):
    pos = np.arange(n)
    half = HEAD_DIM // 2
    inv_freq = ROPE_THETA ** (-np.arange(0, half, 2, dtype=np.float64) / half)
    lane = np.arange(HEAD_DIM)
    p = np.where(lane[None, :] < half, (pos // GRID_W)[:, None], (pos % GRID_W)[:, None]).astype(np.float64)
    ang = p * inv_freq[lane % (half // 2)][None, :]
    sign = np.where((lane % half) < half // 2, -1.0, 1.0)[None, :]
    return np.cos(ang).astype(np.float32), (np.sin(ang) * sign).astype(np.float32)


def _na_block_layout(rows):
    kr = NA_ROWS_MAX
    nblk = rows // NA_QROWS
    starts, patterns, types = [], [], []
    for j in range(nblk):
        u = int(np.clip(NA_QROWS * j - kr // 2, 0, rows - NA_KROWS))
        r = NA_QROWS * j + np.arange(NA_QROWS)
        start_r = np.clip(r - kr // 2, 0, rows - kr)
        kabs = u + np.arange(NA_KROWS)
        valid = (kabs[None, :] >= start_r[:, None]) & (kabs[None, :] < start_r[:, None] + kr)
        assert valid.sum(axis=1).min() == kr
        dr = np.clip(kabs[None, :] - r[:, None] + NA_ROWS_MAX - 1, 0, 2 * NA_ROWS_MAX - 2)
        key = (valid.tobytes(), dr.tobytes())
        keys = [p[0] for p in patterns]
        if key not in keys:
            patterns.append((key, valid, dr))
        types.append([p[0] for p in patterns].index(key))
        starts.append(u)
    return starts, types, [(p[1], p[2]) for p in patterns]


def _na_bias(rpb_l, patterns):
    qcol = np.arange(GRID_W)
    col_start = np.clip(qcol - NA_COLS // 2, 0, GRID_W - NA_COLS)
    kcol = np.arange(GRID_W)
    col_mask = (kcol[None, :] >= col_start[:, None]) & (kcol[None, :] < col_start[:, None] + NA_COLS)
    dc = np.clip(kcol[None, :] - qcol[:, None] + NA_COLS - 1, 0, 2 * NA_COLS - 2)
    out = []
    for valid, dr in patterns:
        bias = rpb_l.astype(F32)[:, dr][..., dc]
        ok = valid[:, :, None, None] & col_mask[None, None, :, :]
        bias = jnp.where(ok[None], bias, NEG).transpose(0, 1, 3, 2, 4)
        out.append(bias.reshape(bias.shape[0], NA_QROWS * GRID_W, NA_KROWS * GRID_W))
    return jnp.stack(out, axis=1)


def _softmax_pv(s_list, v_list):
    m = None
    for s in s_list:
        mm = jnp.max(s, axis=-1, keepdims=True)
        m = mm if m is None else jnp.maximum(m, mm)
    num, den = None, None
    for s, v in zip(s_list, v_list):
        p = jnp.exp(s - m)
        ssum = jnp.sum(p, axis=-1, keepdims=True)
        o = jnp.dot(p.astype(BF16), v, preferred_element_type=F32)
        num = o if num is None else num + o
        den = ssum if den is None else den + ssum
    return num / den


_NT = (((1,), (1,)), ((), ()))


def _na_kernel(q_ref, k_ref, v_ref, cq_ref, ck_ref, cv_ref, bias_ref, cos_ref, sin_ref, y_ref, cy_ref,
               qs_ref, ks_ref, *, starts, types):
    lane = lax.broadcasted_iota(jnp.int32, (1, HEAD_DIM), 1)
    first = (lane % (HEAD_DIM // 2)) < HEAD_DIM // 4
    scale = HEAD_DIM ** -0.5

    def rope(t):
        partner = jnp.where(first, pltpu.roll(t, HEAD_DIM - HEAD_DIM // 4, axis=1),
                            pltpu.roll(t, HEAD_DIM // 4, axis=1))
        return t * cos_ref[...] + partner * sin_ref[...]

    qs_ref[...] = (rope(q_ref[...].astype(F32)) * scale).astype(BF16)
    ks_ref[...] = rope(k_ref[...].astype(F32)).astype(BF16)
    ck = ck_ref[...]
    cv = cv_ref[...]
    qrows = NA_QROWS * GRID_W
    krows = NA_KROWS * GRID_W
    for j, (u, tp) in enumerate(zip(starts, types)):
        qb = qs_ref[j * qrows:(j + 1) * qrows, :]
        kb = ks_ref[u * GRID_W:u * GRID_W + krows, :]
        vb = v_ref[u * GRID_W:u * GRID_W + krows, :]
        s_loc = lax.dot_general(qb, kb, _NT, preferred_element_type=F32) + bias_ref[tp]
        s_ctx = lax.dot_general(qb, ck, _NT, preferred_element_type=F32)
        y_ref[j * qrows:(j + 1) * qrows, :] = _softmax_pv([s_loc, s_ctx], [vb, cv]).astype(y_ref.dtype)
    s = lax.dot_general(cq_ref[...], ck, _NT, preferred_element_type=F32) * scale
    cy_ref[...] = _softmax_pv([s], [cv]).astype(cy_ref.dtype)


def _attention(ub, col0, n, lc, bsz, heads, rpb_l):
    rows = n // GRID_W
    starts, types, patterns = _na_block_layout(rows)
    bias = _na_bias(rpb_l, patterns)
    cos, sin = _na_tables(n)
    cb = col0 // HEAD_DIM
    ctx0 = bsz * n // lc

    def lat(k):
        return pl.BlockSpec((n, HEAD_DIM), lambda h, b: (b, cb + k * heads + h))

    def ctx(k):
        return pl.BlockSpec((lc, HEAD_DIM), lambda h, b: (ctx0 + b, cb + k * heads + h))

    tab = pl.BlockSpec((n, HEAD_DIM), lambda h, b: (0, 0))
    return pl.pallas_call(
        functools.partial(_na_kernel, starts=tuple(starts), types=tuple(types)),
        grid=(heads, bsz),
        in_specs=[lat(0), lat(1), lat(2), ctx(0), ctx(1), ctx(2),
                  pl.BlockSpec((None,) + bias.shape[1:], lambda h, b: (h, 0, 0, 0)), tab, tab],
        out_specs=[pl.BlockSpec((n, HEAD_DIM), lambda h, b: (b, h)),
                   pl.BlockSpec((lc, HEAD_DIM), lambda h, b: (b, h))],
        out_shape=[jax.ShapeDtypeStruct((bsz * n, heads * HEAD_DIM), BF16),
                   jax.ShapeDtypeStruct((bsz * lc, heads * HEAD_DIM), BF16)],
        scratch_shapes=[pltpu.VMEM((n, HEAD_DIM), BF16), pltpu.VMEM((n, HEAD_DIM), BF16)],
        compiler_params=_params("arbitrary", "arbitrary"), name="attention",
    )(ub, ub, ub, ub, ub, ub, bias, jnp.asarray(cos), jnp.asarray(sin))


_RANK_ROWS = 128


def _rank_kernel(afft_ref, aff_ref, rank_ref, *, n, ne):
    t_idx = lax.broadcasted_iota(jnp.int32, (1, n), 1)
    for e in range(ne):
        row = afft_ref[e:e + 1, :]

        def body(c, acc, e=e, row=row):
            s0 = pl.multiple_of(c * _RANK_ROWS, _RANK_ROWS)
            col = aff_ref[pl.ds(s0, _RANK_ROWS), :][:, e:e + 1]
            s_idx = s0 + lax.broadcasted_iota(jnp.int32, (_RANK_ROWS, 1), 0)
            beats = (col > row) | ((col == row) & (s_idx < t_idx))
            return acc + jnp.sum(beats.astype(F32), axis=0, keepdims=True)

        rank = lax.fori_loop(0, n // _RANK_ROWS, body, jnp.zeros((1, n), F32))
        rank_ref[e:e + 1, :] = rank.astype(jnp.int32)


def _rank(aff, row0, n, bsz):
    ne = aff.shape[1]
    a = aff[row0:row0 + bsz * n].reshape(bsz, n, ne)
    at = jnp.swapaxes(a, 1, 2)
    return pl.pallas_call(
        functools.partial(_rank_kernel, n=n, ne=ne),
        grid=(bsz,),
        in_specs=[pl.BlockSpec((None, ne, n), lambda b: (b, 0, 0)),
                  pl.BlockSpec((None, n, ne), lambda b: (b, 0, 0))],
        out_specs=pl.BlockSpec((None, ne, n), lambda b: (b, 0, 0)),
        out_shape=jax.ShapeDtypeStruct((bsz, ne, n), jnp.int32),
        compiler_params=_params("arbitrary"), name="expert_rank",
    )(at, a), at


def _gather_kernel(rank_ref, afft_ref, h_ref, xg_ref, g_ref, *, cap):
    slot = lax.broadcasted_iota(jnp.int32, (cap, 1), 0)
    sel = rank_ref[...] == slot
    xg_ref[...] = jnp.dot(sel.astype(BF16), h_ref[...], preferred_element_type=F32).astype(xg_ref.dtype)
    g_ref[...] = jnp.sum(jnp.where(sel, afft_ref[...], 0.0), axis=1, keepdims=True)


def _gather(rank, afft, h2, row_block0, n, bsz, cap):
    ne = rank.shape[1]
    d = h2.shape[1]
    r4 = rank.reshape(bsz, ne, 1, n)
    a4 = afft.reshape(bsz, ne, 1, n)
    row_spec = pl.BlockSpec((None, None, 1, n), lambda b, e: (b, e, 0, 0))
    return pl.pallas_call(
        functools.partial(_gather_kernel, cap=cap),
        grid=(bsz, ne),
        in_specs=[row_spec, row_spec, pl.BlockSpec((n, d), lambda b, e: (row_block0 + b, 0))],
        out_specs=[pl.BlockSpec((None, cap, d), lambda b, e: (b, e, 0)),
                   pl.BlockSpec((None, cap, 1), lambda b, e: (b, e, 0))],
        out_shape=[jax.ShapeDtypeStruct((bsz, ne * cap, d), BF16),
                   jax.ShapeDtypeStruct((bsz, ne * cap, 1), F32)],
        compiler_params=_params("arbitrary", "arbitrary"), name="expert_gather",
    )(r4, a4, h2)


def _ffn_kernel(*refs, nseq, streams, nf):
    ns = len(streams)
    x_refs = refs[0:2 * ns:2]
    g_refs = refs[1:2 * ns:2]
    wg_ref, wu_ref, wd_ref = refs[2 * ns:2 * ns + 3]
    o_refs = refs[2 * ns + 3:3 * ns + 3]
    wgb, wub, wdb = refs[3 * ns + 3:3 * ns + 6]
    acc_refs = refs[3 * ns + 6:]
    f = pl.program_id(1)
    s = pl.program_id(2)

    @pl.when(s == 0)
    def _():
        wgb[...] = wg_ref[...].astype(BF16)
        wub[...] = wu_ref[...].astype(BF16)
        wdb[...] = wd_ref[...].astype(BF16)

    for k in range(ns):
        @pl.when((s >= k * nseq) & (s < (k + 1) * nseq))
        def _(k=k):
            b = s - k * nseq
            x = x_refs[k][...]
            hg = jnp.dot(x, wgb[...], preferred_element_type=F32)
            hu = jnp.dot(x, wub[...], preferred_element_type=F32)
            hid = (hg * _sigmoid(hg) * hu).astype(BF16)
            part = jnp.dot(hid, wdb[...], preferred_element_type=F32)

            @pl.when(f == 0)
            def _():
                acc_refs[k][b] = part

            @pl.when(f > 0)
            def _():
                acc_refs[k][b] += part

            @pl.when(f == nf - 1)
            def _():
                o_refs[k][...] = (acc_refs[k][b] * g_refs[k][...]).astype(o_refs[k].dtype)


def _expert_ffn(xgs, gates, caps, w_gate, w_up, w_down, layer, nseq):
    ns = len(xgs)
    _, ne, d, ff = w_gate.shape
    tf = min(512, ff)
    nf = ff // tf
    in_specs, out_specs, out_shape, acc_scr = [], [], [], []
    for k in range(ns):
        cap = caps[k]

        def seq_idx(e, f, s, k=k):
            return jnp.clip(s - k * nseq, 0, nseq - 1)

        in_specs.append(pl.BlockSpec((None, cap, d), lambda e, f, s, i=seq_idx: (i(e, f, s), e, 0)))
        in_specs.append(pl.BlockSpec((None, cap, 1), lambda e, f, s, i=seq_idx: (i(e, f, s), e, 0)))

        def out_idx(e, f, s, k=k):
            live = (f == nf - 1)
            return jnp.where(live, jnp.clip(s - k * nseq, 0, nseq - 1), 0), e, 0

        out_specs.append(pl.BlockSpec((None, cap, d), out_idx))
        out_shape.append(jax.ShapeDtypeStruct((nseq, ne * cap, d), BF16))
        acc_scr.append(pltpu.VMEM((nseq, cap, d), F32))
    in_specs += [pl.BlockSpec((None, None, d, tf), lambda e, f, s: (layer, e, 0, f)),
                 pl.BlockSpec((None, None, d, tf), lambda e, f, s: (layer, e, 0, f)),
                 pl.BlockSpec((None, None, tf, d), lambda e, f, s: (layer, e, f, 0))]
    args = []
    for k in range(ns):
        args += [xgs[k], gates[k]]
    return pl.pallas_call(
        functools.partial(_ffn_kernel, nseq=nseq, streams=tuple(range(ns)), nf=nf),
        grid=(ne, nf, ns * nseq),
        in_specs=in_specs, out_specs=out_specs, out_shape=out_shape,
        scratch_shapes=[pltpu.VMEM((d, tf), BF16), pltpu.VMEM((d, tf), BF16), pltpu.VMEM((tf, d), BF16)] + acc_scr,
        compiler_params=_params("arbitrary", "arbitrary", "arbitrary"), name="expert_ffn",
    )(*args, w_gate, w_up, w_down)


def _scatter_kernel(rank_ref, y_ref, x_ref, mod_ref, o_ref, *, ne, cap, gate_idx):
    rk = rank_ref[...]
    slot = lax.broadcasted_iota(jnp.int32, (1, cap), 1)
    acc = None
    for e in range(ne):
        sel = (rk[:, e:e + 1] == slot).astype(BF16)
        p = jnp.dot(sel, y_ref[e * cap:(e + 1) * cap, :], preferred_element_type=F32)
        acc = p if acc is None else acc + p
    o_ref[...] = x_ref[...] + mod_ref[gate_idx:gate_idx + 1, :] * acc


def _scatter(rank, y, x, modt, gate_idx, row0, n, bsz, cap, tn):
    ne = rank.shape[1]
    d = x.shape[1]
    tm = min(512, n)
    rank_t = jnp.swapaxes(rank, 1, 2)
    xb0 = row0 // tm
    mb0 = row0 // ROW_TILE
    per = tm // ROW_TILE
    return pl.pallas_call(
        functools.partial(_scatter_kernel, ne=ne, cap=cap, gate_idx=gate_idx),
        grid=(bsz, d // tn, n // tm),
        in_specs=[pl.BlockSpec((None, tm, ne), lambda b, j, i: (b, i, 0)),
                  pl.BlockSpec((None, ne * cap, tn), lambda b, j, i: (b, 0, j)),
                  pl.BlockSpec((tm, tn), lambda b, j, i: (xb0 + b * (n // tm) + i, j)),
                  pl.BlockSpec((None, 6, tn), lambda b, j, i: (mb0 + (b * (n // tm) + i) * per, 0, j))],
        out_specs=pl.BlockSpec((tm, tn), lambda b, j, i: (b * (n // tm) + i, j)),
        out_shape=jax.ShapeDtypeStruct((bsz * n, d), F32),
        compiler_params=_params("arbitrary", "arbitrary", "arbitrary"), name="expert_scatter",
    )(rank_t, y, x, modt)


def kernel(x, c, ctx, c_ctx, w_mod, b_mod, g_norm1, w_in, lb_param, g_hgrn, w_pool, pool_scale, rpb,
           w_branch, w_out, g_norm2, w_router, w_gate_e, w_up_e, w_down_e, g_final):
    bsz, n, d = x.shape
    lc = ctx.shape[1]
    depth = w_mod.shape[0]
    width = lb_param.shape[2]
    heads = width // HEAD_DIM
    ne = w_router.shape[2]
    n_lat = bsz * n
    n_ctx = bsz * lc
    total = n_lat + n_ctx
    assert bsz + 1 <= 8 and n % ROW_TILE == 0 and lc % ROW_TILE == 0 and n_lat % lc == 0
    assert w_pool.shape[-1] * 4 == width and w_branch.shape[2] == width
    tm = _row_tile(n, n_ctx)
    tn = min(512, d)

    lb_all = jnp.cumsum(jax.nn.softmax(lb_param.astype(F32), axis=1), axis=1)
    lb_all = lb_all - lb_all[:, :1]
    lbp_all = jnp.stack([jnp.log(lb_all), jnp.log1p(-lb_all), 1.0 - lb_all], axis=2)
    lbp_all = lbp_all.reshape(2, depth, 3, heads, HEAD_DIM).transpose(0, 1, 3, 2, 4)

    c8 = jnp.concatenate([c, c_ctx[None], jnp.zeros((8 - bsz - 1, d), F32)], axis=0)
    mod = _modulation(c8, w_mod, b_mod)
    tile_row = np.concatenate([np.repeat(np.arange(bsz), n // ROW_TILE), np.full(n_ctx // ROW_TILE, bsz)])

    xs = jnp.concatenate([x.reshape(n_lat, d), ctx.reshape(n_ctx, d)], axis=0)
    zero_state = jnp.zeros((bsz, heads, HEAD_DIM, HEAD_DIM), F32)
    cap = EC_CAPACITY * n // ne
    cap_c = EC_CAPACITY * lc // ne
    a_cols = 5 * width
    b_cols = w_in.shape[2] - a_cols
    gate_col0 = 4 * width

    for l in range(depth):
        last = l == depth - 1
        rows = n_lat if last else total
        modt = mod[l].reshape(8, 6, d)[tile_row]
        gain_h = g_hgrn[l].reshape(heads, 1, HEAD_DIM)

        h = _norm_modulate(xs, g_norm1[l], modt, 0, total)
        ua = _matmul(h, w_in, l, 0, a_cols, F32, tm, tn)
        ub = _matmul(h, w_in, l, a_cols, b_cols, BF16, tm, tn)

        cy_a, s_f, s_b = _hgrn(ua, n_lat // lc, lc, bsz, heads, lbp_all[:, l], gain_h, zero_state, zero_state)
        y_a, _, _ = _hgrn(ua, 0, n, bsz, heads, lbp_all[:, l], gain_h, s_f, s_b)
        y_b = _pool(ub, 0, n, bsz, w_pool, pool_scale, l)
        y_c, cy_c = _attention(ub, width, n, lc, bsz, heads, rpb[l])
        if not last:
            cy_b = _pool(ub, n_lat // lc, lc, bsz, w_pool, pool_scale, l)
            y_a = jnp.concatenate([y_a, cy_a], axis=0)
            y_b = jnp.concatenate([y_b, cy_b], axis=0)
            y_c = jnp.concatenate([y_c, cy_c], axis=0)

        merged = _merge(y_a, y_b, y_c, ub, gate_col0, w_branch, l, rows, tm, tn)
        x_mid = _matmul_residual(merged, w_out, l, xs, modt, 2, rows, tm, tn)

        h2, aff = _norm_modulate(x_mid, g_norm2[l], modt, 3, rows, w_router[l])
        rank, afft = _rank(aff, 0, n, bsz)
        xg, gate = _gather(rank, afft, h2, 0, n, bsz, cap)
        xgs, gates, caps = [xg], [gate], [cap]
        if not last:
            rank_c, afft_c = _rank(aff, n_lat, lc, bsz)
            xg_c, gate_c = _gather(rank_c, afft_c, h2, n_lat // lc, lc, bsz, cap_c)
            xgs, gates, caps = xgs + [xg_c], gates + [gate_c], caps + [cap_c]
        ys = _expert_ffn(xgs, gates, caps, w_gate_e, w_up_e, w_down_e, l, bsz)
        x_new = _scatter(rank, ys[0], x_mid, modt, 5, 0, n, bsz, cap, tn)
        if not last:
            xc_new = _scatter(rank_c, ys[1], x_mid, modt, 5, n_lat, lc, bsz, cap_c, tn)
            xs = jnp.concatenate([x_new, xc_new], axis=0)
        else:
            xs = x_new

    return _final_norm(xs, g_final, n_lat).reshape(bsz, n, d)
```

```python
import functools

import numpy as np
import jax
import jax.numpy as jnp
from jax import lax
from jax.experimental import pallas as pl
from jax.experimental.pallas import tpu as pltpu

F32 = jnp.float32
BF16 = jnp.bfloat16
EPS = 1e-6
LANES = 128
HEAD_DIM = 128
CHUNK = 64
HGRN_GROUP = 2
GRID_W = 64
POOL_WINDOWS = (2, 4, 8, 16)
POOL_PAD = 16
NA_ROWS_MAX = 8
NA_COLS = 16
NA_QROWS = 4
NA_KROWS = 12
ROPE_THETA = 10000.0
EC_CAPACITY = 2
FFN_ROWS = 512
NEG = -1e30
ROW_TILE = 256
LOG2E = 1.4426950408889634
VMEM_LIMIT = 56 * 2 ** 20


def _params(*sem):
    return pltpu.CompilerParams(dimension_semantics=sem, vmem_limit_bytes=VMEM_LIMIT)


def _sigmoid(x):
    return 1.0 / (1.0 + jnp.exp(-x))


def _neg_abs(x):
    bits = lax.bitcast_convert_type(x, jnp.uint32) | jnp.uint32(0x80000000)
    return lax.bitcast_convert_type(bits, F32)


def _lane_tile(n, limit):
    t = min(limit, n) // LANES * LANES
    while n % t:
        t -= LANES
    return t


def _row_tile(n, ctx_rows):
    tm = 1024
    while n % tm or ctx_rows % tm:
        tm //= 2
    return tm


class _Rows:
    def __init__(self, tm, n, bsz, n_lat, rows):
        self.tm, self.n, self.bsz = tm, n, bsz
        self.lat_tiles = n_lat // tm
        self.tiles = rows // tm
        self.ctx_tiles = max(self.tiles - self.lat_tiles, 1)

    def lat(self, i):
        return jnp.minimum(i, self.lat_tiles - 1)

    def ctx(self, i):
        return jnp.clip(i - self.lat_tiles, 0, self.ctx_tiles - 1)

    def mod_row(self, i):
        return jnp.minimum(i * self.tm // self.n, self.bsz)


def _pick(is_lat, lat_ref, ctx_ref):
    return jnp.where(is_lat, lat_ref[...], ctx_ref[...])


def _mod_kernel(c_ref, w_ref, b_ref, o_ref):
    c = c_ref[...]
    sc = (c * _sigmoid(c)).astype(BF16)
    o_ref[...] = jnp.dot(sc, w_ref[...].astype(BF16), preferred_element_type=F32) + b_ref[...]


def _modulation(c8, w_mod, b_mod):
    depth, d, n6 = w_mod.shape
    tn = _lane_tile(n6, 1024)
    return pl.pallas_call(
        _mod_kernel,
        grid=(depth, n6 // tn),
        in_specs=[pl.BlockSpec((8, d), lambda l, j: (0, 0)),
                  pl.BlockSpec((None, d, tn), lambda l, j: (l, 0, j)),
                  pl.BlockSpec((None, 1, tn), lambda l, j: (l, 0, j))],
        out_specs=pl.BlockSpec((None, 8, tn), lambda l, j: (l, 0, j)),
        out_shape=jax.ShapeDtypeStruct((depth, 8, n6), F32),
        compiler_params=_params("arbitrary", "arbitrary"),
        name="modulation",
    )(c8, w_mod, b_mod.reshape(depth, 1, n6))


def _rmsnorm(x, gain):
    return x * lax.rsqrt(jnp.mean(x * x, axis=-1, keepdims=True) + EPS) * gain


def _modulate(y, mod_ref, shift_idx):
    return y * (1.0 + mod_ref[shift_idx + 1:shift_idx + 2, :]) + mod_ref[shift_idx:shift_idx + 1, :]


def _norm1_kernel(xl_ref, xc_ref, g_ref, mod_ref, o_ref, *, lat_tiles):
    x = _pick(pl.program_id(0) < lat_tiles, xl_ref, xc_ref)
    o_ref[...] = _modulate(_rmsnorm(x, g_ref[...]), mod_ref, 0).astype(o_ref.dtype)


def _norm2_kernel(x_ref, g_ref, mod_ref, wr_ref, o_ref, aff_ref):
    h = _modulate(_rmsnorm(x_ref[...], g_ref[...]), mod_ref, 3)
    o_ref[...] = h.astype(o_ref.dtype)
    logits = jnp.dot(h, wr_ref[...], precision=lax.Precision.HIGHEST, preferred_element_type=F32)
    logits = logits[:, :aff_ref.shape[1]]
    e = jnp.exp(logits - jnp.max(logits, axis=-1, keepdims=True))
    aff_ref[...] = e / jnp.sum(e, axis=-1, keepdims=True)


def _final_norm_kernel(x_ref, g_ref, o_ref):
    o_ref[...] = _rmsnorm(x_ref[...], g_ref[...])


def _norm1(x_lat, x_ctx, gain, mod, layer, rt):
    d = x_lat.shape[1]
    return pl.pallas_call(
        functools.partial(_norm1_kernel, lat_tiles=rt.lat_tiles),
        grid=(rt.tiles,),
        in_specs=[pl.BlockSpec((rt.tm, d), lambda i: (rt.lat(i), 0)),
                  pl.BlockSpec((rt.tm, d), lambda i: (rt.ctx(i), 0)),
                  pl.BlockSpec((1, d), lambda i: (0, 0)),
                  pl.BlockSpec((None, None, 6, d), lambda i: (layer, rt.mod_row(i), 0, 0))],
        out_specs=pl.BlockSpec((rt.tm, d), lambda i: (i, 0)),
        out_shape=jax.ShapeDtypeStruct((rt.tiles * rt.tm, d), BF16),
        compiler_params=_params("arbitrary"), name="norm1",
    )(x_lat, x_ctx, gain.reshape(1, d), mod)


def _norm2(x, gain, mod, layer, rt, w_router_padded, ne):
    d = x.shape[1]
    rows = rt.tiles * rt.tm
    return pl.pallas_call(
        _norm2_kernel,
        grid=(rt.tiles,),
        in_specs=[pl.BlockSpec((rt.tm, d), lambda i: (i, 0)),
                  pl.BlockSpec((1, d), lambda i: (0, 0)),
                  pl.BlockSpec((None, None, 6, d), lambda i: (layer, rt.mod_row(i), 0, 0)),
                  pl.BlockSpec((None, d, w_router_padded.shape[2]), lambda i: (layer, 0, 0))],
        out_specs=[pl.BlockSpec((rt.tm, d), lambda i: (i, 0)), pl.BlockSpec((rt.tm, ne), lambda i: (i, 0))],
        out_shape=[jax.ShapeDtypeStruct((rows, d), BF16), jax.ShapeDtypeStruct((rows, ne), F32)],
        compiler_params=_params("arbitrary"), name="norm2_router",
    )(x, gain.reshape(1, d), mod, w_router_padded)


def _final_norm(x, gain):
    rows, d = x.shape
    tm = ROW_TILE
    return pl.pallas_call(
        _final_norm_kernel,
        grid=(rows // tm,),
        in_specs=[pl.BlockSpec((tm, d), lambda i: (i, 0)), pl.BlockSpec((1, d), lambda i: (0, 0))],
        out_specs=pl.BlockSpec((tm, d), lambda i: (i, 0)),
        out_shape=jax.ShapeDtypeStruct((rows, d), F32),
        compiler_params=_params("arbitrary"), name="final_norm",
    )(x, gain.reshape(1, d))


def _mm_kernel(a_ref, w_ref, o_ref, wbf_ref):
    @pl.when(pl.program_id(1) == 0)
    def _():
        wbf_ref[...] = w_ref[...].astype(BF16)
    o_ref[...] = jnp.dot(a_ref[...], wbf_ref[...], preferred_element_type=F32).astype(o_ref.dtype)


def _matmul(a, w_all, layer, col0, ncols, out_dtype, tm, tn):
    m, k = a.shape
    off = col0 // tn
    return pl.pallas_call(
        _mm_kernel,
        grid=(ncols // tn, m // tm),
        in_specs=[pl.BlockSpec((tm, k), lambda j, i: (i, 0)),
                  pl.BlockSpec((None, k, tn), lambda j, i: (layer, 0, j + off))],
        out_specs=pl.BlockSpec((tm, tn), lambda j, i: (i, j)),
        out_shape=jax.ShapeDtypeStruct((m, ncols), out_dtype),
        scratch_shapes=[pltpu.VMEM((k, tn), BF16)],
        compiler_params=_params("arbitrary", "arbitrary"), name="matmul",
    )(a, w_all)


def _out_proj_kernel(a_ref, w_ref, xl_ref, xc_ref, mod_ref, o_ref, wbf_ref, *, lat_tiles):
    @pl.when(pl.program_id(1) == 0)
    def _():
        wbf_ref[...] = w_ref[...].astype(BF16)
    acc = jnp.dot(a_ref[...], wbf_ref[...], preferred_element_type=F32)
    x = _pick(pl.program_id(1) < lat_tiles, xl_ref, xc_ref)
    o_ref[...] = x + mod_ref[2:3, :] * acc


def _out_proj(a, w_all, layer, x_lat, x_ctx, mod, rt, tn):
    k = a.shape[1]
    d = w_all.shape[2]
    tm = rt.tm
    return pl.pallas_call(
        functools.partial(_out_proj_kernel, lat_tiles=rt.lat_tiles),
        grid=(d // tn, rt.tiles),
        in_specs=[pl.BlockSpec((tm, k), lambda j, i: (i, 0)),
                  pl.BlockSpec((None, k, tn), lambda j, i: (layer, 0, j)),
                  pl.BlockSpec((tm, tn), lambda j, i: (rt.lat(i), j)),
                  pl.BlockSpec((tm, tn), lambda j, i: (rt.ctx(i), j)),
                  pl.BlockSpec((None, None, 6, tn), lambda j, i: (layer, rt.mod_row(i), 0, j))],
        out_specs=pl.BlockSpec((tm, tn), lambda j, i: (i, j)),
        out_shape=jax.ShapeDtypeStruct((rt.tiles * tm, d), F32),
        scratch_shapes=[pltpu.VMEM((k, tn), BF16)],
        compiler_params=_params("arbitrary", "arbitrary"), name="out_proj",
    )(a, w_all, x_lat, x_ctx, mod)


def _merge_kernel(ya_ref, yb_ref, yc_ref, cya_ref, cyb_ref, cyc_ref, g0_ref, g1_ref, g2_ref, w_ref, o_ref, wbf_ref,
                  *, lat_tiles):
    @pl.when(pl.program_id(1) == 0)
    def _():
        wbf_ref[...] = w_ref[...].astype(BF16)
    is_lat = pl.program_id(1) < lat_tiles
    acc = None
    for j, (y_ref, cy_ref, g_ref) in enumerate(((ya_ref, cya_ref, g0_ref), (yb_ref, cyb_ref, g1_ref),
                                                (yc_ref, cyc_ref, g2_ref))):
        p = jnp.dot(_pick(is_lat, y_ref, cy_ref), wbf_ref[j], preferred_element_type=F32)
        t = _sigmoid(g_ref[...].astype(F32)) * p
        acc = t if acc is None else acc + t
    o_ref[...] = acc.astype(o_ref.dtype)


def _merge(ys_lat, ys_ctx, ub, gate_col0, w_branch, layer, rt, tn):
    bw = ys_lat[0].shape[1]
    d = w_branch.shape[3]
    tm = rt.tm
    lat_spec = pl.BlockSpec((tm, bw), lambda j, i: (rt.lat(i), 0))
    ctx_spec = pl.BlockSpec((tm, bw), lambda j, i: (rt.ctx(i), 0))

    def gate_spec(k):
        off = (gate_col0 + k * d) // tn
        return pl.BlockSpec((tm, tn), lambda j, i: (i, off + j))

    return pl.pallas_call(
        functools.partial(_merge_kernel, lat_tiles=rt.lat_tiles),
        grid=(d // tn, rt.tiles),
        in_specs=[lat_spec] * 3 + [ctx_spec] * 3 + [gate_spec(0), gate_spec(1), gate_spec(2),
                  pl.BlockSpec((None, 3, bw, tn), lambda j, i: (layer, 0, 0, j))],
        out_specs=pl.BlockSpec((tm, tn), lambda j, i: (i, j)),
        out_shape=jax.ShapeDtypeStruct((rt.tiles * tm, d), BF16),
        scratch_shapes=[pltpu.VMEM((3, bw, tn), BF16)],
        compiler_params=_params("arbitrary", "arbitrary"), name="merge",
    )(*ys_lat, *ys_ctx, ub, ub, ub, w_branch)


_HGRN_BLOCKS = (32, 16, 8, 4, 2, 1)


def _hgrn_constants():
    c = CHUNK
    t = np.arange(c)
    tri = (t[:, None] >= t[None, :]).astype(np.float32)
    cms, masks = [tri], []
    for m in _HGRN_BLOCKS:
        ref = (t // (2 * m)) * (2 * m) + m - 1
        cms.append(tri[ref])
        same = (t[:, None] // (2 * m)) == (t[None, :] // (2 * m))
        masks.append(same & ((t[:, None] % (2 * m)) >= m) & ((t[None, :] % (2 * m)) < m))
    masks.append(np.eye(c, dtype=bool))
    cm_f = np.concatenate(cms, axis=0)
    mask_f = np.stack(masks).astype(np.float32)
    cm_b = cm_f.reshape(-1, c, c)[:, ::-1, ::-1].reshape(-1, c)
    mask_b = mask_f[:, ::-1, ::-1]
    cm = np.stack([cm_f, cm_b])
    return np.concatenate([cm, cm, cm], axis=2), np.stack([mask_f, mask_b])


def _hgrn_chunk(q, z, v, vb, lbp, cm3, masks, last_row):
    nl = len(_HGRN_BLOCKS)
    ez = jnp.exp(_neg_abs(z))
    one_p = 1.0 + ez
    log_sig = jnp.minimum(z, 0.0) - jnp.log(one_p)
    k = lbp[2:3, :] * jnp.where(z >= 0.0, ez, 1.0) / one_p
    a = lbp[0:1, :]
    cc = lbp[1:2, :] + log_sig
    logf = jnp.maximum(a, cc) + jnp.log(1.0 + jnp.exp(_neg_abs(a - cc)))
    logf2 = logf * LOG2E
    g1 = logf2.astype(BF16)
    r1 = logf2 - g1.astype(F32)
    g2 = r1.astype(BF16)
    g3 = (r1 - g2.astype(F32)).astype(BF16)
    rr = jnp.dot(cm3, jnp.concatenate([g1, g2, g3], axis=0), preferred_element_type=F32)
    b = rr[0:CHUNK]
    refs = rr[CHUNK:].reshape(nl, CHUNK, LANES)
    tot = b[last_row:last_row + 1, :]
    lvl = jnp.exp2(_neg_abs(b[None] - refs))
    qs = jnp.concatenate([q[None] * lvl, q[None]], axis=0).astype(BF16)
    ks = jnp.concatenate([k[None] * lvl, k[None]], axis=0).astype(BF16)
    sc = jnp.einsum("ltk,lsk->lts", qs, ks, preferred_element_type=F32)
    amat = jnp.sum(sc * masks, axis=0)
    o = jnp.dot(amat.astype(BF16), vb, preferred_element_type=F32)
    qt = (q * jnp.exp2(b)).astype(BF16)
    kd = (k * jnp.exp2(tot - b)).astype(BF16)
    upd = lax.dot_general(vb, kd, (((0,), (0,)), ((), ())), preferred_element_type=F32)
    return o, qt, upd, jnp.exp2(tot)


_NT = (((1,), (1,)), ((), ()))


def _hgrn_kernel(q_ref, ff_ref, fb_ref, i_ref, g_ref, lbf_ref, lbb_ref, gain_ref, cm_ref, mask_ref,
                 sf0_ref, sb0_ref, y_ref, sf_ref, sb_ref, o_scr, qt_scr, upd_scr, dec_scr, st_scr, *, nc):
    z_refs = (ff_ref, fb_ref)
    lb_refs = (lbf_ref, lbb_ref)

    def local_pass(it, carry):
        for sub in range(HGRN_GROUP):
            j = it * HGRN_GROUP + sub
            rows = pl.ds(pl.multiple_of(j * CHUNK, CHUNK), CHUNK)
            qr = q_ref[rows, :]
            q = qr * _sigmoid(qr)
            v = i_ref[rows, :]
            vb = v.astype(BF16)
            o_sum = None
            for d in range(2):
                o, qt, upd, dec = _hgrn_chunk(q, z_refs[d][rows, :], v, vb, lb_refs[d][...], cm_ref[d],
                                              mask_ref[d], CHUNK - 1 if d == 0 else 0)
                qt_scr[d, rows, :] = qt
                upd_scr[d, j] = upd
                dec_scr[d, pl.ds(j, 1), :] = dec
                o_sum = o if o_sum is None else o_sum + o
            o_scr[rows, :] = o_sum
        return carry

    lax.fori_loop(0, nc // HGRN_GROUP, local_pass, 0)

    st_scr[0] = sf0_ref[...]
    st_scr[1] = sb0_ref[...]

    def state_pass(it, carry):
        for d in range(2):
            j = it if d == 0 else nc - 1 - it
            rows = pl.ds(pl.multiple_of(j * CHUNK, CHUNK), CHUNK)
            st = st_scr[d]
            o_scr[rows, :] += lax.dot_general(qt_scr[d, rows, :], st.astype(BF16), _NT, preferred_element_type=F32)
            st_scr[d] = st * dec_scr[d, pl.ds(j, 1), :] + upd_scr[d, j]
        return carry

    lax.fori_loop(0, nc, state_pass, 0, unroll=2)
    sf_ref[...] = st_scr[0]
    sb_ref[...] = st_scr[1]
    o = o_scr[...]
    o = o * lax.rsqrt(jnp.mean(o * o, axis=-1, keepdims=True) + EPS) * gain_ref[...]
    g = g_ref[...]
    y_ref[...] = (o * (g * _sigmoid(g))).astype(y_ref.dtype)


def _hgrn(ua, row_block0, n, bsz, heads, lbp, gain, sf0, sb0, cm3, masks):
    nc = n // CHUNK
    assert nc % HGRN_GROUP == 0 and nc % 2 == 0

    def col(k):
        return pl.BlockSpec((n, LANES), lambda b, h: (row_block0 + b, k * heads + h))

    lb_spec = lambda d: pl.BlockSpec((None, None, 3, LANES), lambda b, h: (d, h, 0, 0))
    st_spec = pl.BlockSpec((None, None, LANES, LANES), lambda b, h: (b, h, 0, 0))
    st_shape = jax.ShapeDtypeStruct((bsz, heads, LANES, LANES), F32)
    return pl.pallas_call(
        functools.partial(_hgrn_kernel, nc=nc),
        grid=(bsz, heads),
        in_specs=[col(0), col(1), col(2), col(3), col(4), lb_spec(0), lb_spec(1),
                  pl.BlockSpec((None, 1, LANES), lambda b, h: (h, 0, 0)),
                  pl.BlockSpec(cm3.shape, lambda b, h: (0, 0, 0)),
                  pl.BlockSpec(masks.shape, lambda b, h: (0, 0, 0, 0)),
                  st_spec, st_spec],
        out_specs=[pl.BlockSpec((n, LANES), lambda b, h: (b, h)), st_spec, st_spec],
        out_shape=[jax.ShapeDtypeStruct((bsz * n, heads * LANES), BF16), st_shape, st_shape],
        scratch_shapes=[pltpu.VMEM((n, LANES), F32), pltpu.VMEM((2, n, LANES), BF16),
                        pltpu.VMEM((2, nc, LANES, LANES), F32), pltpu.VMEM((2, nc, LANES), F32),
                        pltpu.VMEM((2, LANES, LANES), F32)],
        compiler_params=_params("arbitrary", "arbitrary"), name="hgrn2",
    )(ua, ua, ua, ua, ua, lbp, lbp, gain, cm3, masks, sf0, sb0)


def _pool_kernel(u_ref, w_ref, s_ref, y_ref, pad_ref, *, n, group):
    pos = lax.broadcasted_iota(jnp.int32, (n, 1), 0)
    zeros = jnp.zeros((POOL_PAD, pad_ref.shape[1]), F32)
    pad_ref[0:POOL_PAD, :] = zeros
    pad_ref[POOL_PAD + n:2 * POOL_PAD + n, :] = zeros
    pad_ref[POOL_PAD:POOL_PAD + n, :] = u_ref[...].astype(F32)
    for gi, w in enumerate(POOL_WINDOWS):
        cols = slice(gi * group, (gi + 1) * group)
        acc = None
        for dlt in range(-(w // 2), w // 2):
            t = pad_ref[POOL_PAD + dlt:POOL_PAD + dlt + n, cols]
            acc = t if acc is None else acc + t
        lo = jnp.maximum(pos - w // 2, 0)
        hi = jnp.minimum(pos + w // 2 - 1, n - 1)
        cnt = (hi - lo + 1).astype(F32)
        dd = acc / cnt - pad_ref[POOL_PAD:POOL_PAD + n, cols]
        y = jnp.dot(dd.astype(BF16), w_ref[gi].astype(BF16), preferred_element_type=F32)
        y_ref[:, cols] = (y * s_ref[:, cols]).astype(y_ref.dtype)


def _pool(ub, row_block0, n, bsz, w_pool, scale, layer):
    group = w_pool.shape[-1]
    width = 4 * group
    return pl.pallas_call(
        functools.partial(_pool_kernel, n=n, group=group),
        grid=(bsz,),
        in_specs=[pl.BlockSpec((n, width), lambda b: (row_block0 + b, 0)),
                  pl.BlockSpec((None, 4, group, group), lambda b: (layer, 0, 0, 0)),
                  pl.BlockSpec((None, 1, width), lambda b: (layer, 0, 0))],
        out_specs=pl.BlockSpec((n, width), lambda b: (b, 0)),
        out_shape=jax.ShapeDtypeStruct((bsz * n, width), BF16),
        scratch_shapes=[pltpu.VMEM((n + 2 * POOL_PAD, width), F32)],
        compiler_params=_params("arbitrary"), name="pool",
    )(ub, w_pool, scale.reshape(scale.shape[0], 1, width))


def _na_tables(n):
    pos = np.arange(n)
    half = HEAD_DIM // 2
    inv_freq = ROPE_THETA ** (-np.arange(0, half, 2, dtype=np.float64) / half)
    lane = np.arange(HEAD_DIM)
    p = np.where(lane[None, :] < half, (pos // GRID_W)[:, None], (pos % GRID_W)[:, None]).astype(np.float64)
    ang = p * inv_freq[lane % (half // 2)][None, :]
    sign = np.where((lane % half) < half // 2, -1.0, 1.0)[None, :]
    return np.cos(ang).astype(np.float32), (np.sin(ang) * sign).astype(np.float32)


def _na_block_layout(rows):
    kr = NA_ROWS_MAX
    nblk = rows // NA_QROWS
    starts, patterns, types = [], [], []
    for j in range(nblk):
        u = int(np.clip(NA_QROWS * j - kr // 2, 0, rows - NA_KROWS))
        r = NA_QROWS * j + np.arange(NA_QROWS)
        start_r = np.clip(r - kr // 2, 0, rows - kr)
        kabs = u + np.arange(NA_KROWS)
        valid = (kabs[None, :] >= start_r[:, None]) & (kabs[None, :] < start_r[:, None] + kr)
        assert valid.sum(axis=1).min() == kr
        dr = np.clip(kabs[None, :] - r[:, None] + NA_ROWS_MAX - 1, 0, 2 * NA_ROWS_MAX - 2)
        key = (valid.tobytes(), dr.tobytes())
        keys = [p[0] for p in patterns]
        if key not in keys:
            patterns.append((key, valid, dr))
        types.append([p[0] for p in patterns].index(key))
        starts.append(u)
    return starts, types, [(p[1], p[2]) for p in patterns]


def _na_bias(rpb, patterns):
    qcol = np.arange(GRID_W)
    col_start = np.clip(qcol - NA_COLS // 2, 0, GRID_W - NA_COLS)
    kcol = np.arange(GRID_W)
    col_mask = (kcol[None, :] >= col_start[:, None]) & (kcol[None, :] < col_start[:, None] + NA_COLS)
    dc = np.clip(kcol[None, :] - qcol[:, None] + NA_COLS - 1, 0, 2 * NA_COLS - 2)
    col_hot = (dc[None] == np.arange(2 * NA_COLS - 1)[:, None, None]).astype(np.float32)
    by_col = jnp.einsum("lhab,bqk->lhaqk", rpb.astype(F32), col_hot, precision=lax.Precision.HIGHEST)
    out = []
    for valid, dr in patterns:
        row_hot = (dr[..., None] == np.arange(2 * NA_ROWS_MAX - 1)).astype(np.float32)
        bias = jnp.einsum("rca,lhaqk->lhrqck", row_hot, by_col, precision=lax.Precision.HIGHEST)
        ok = valid[:, None, :, None] & col_mask[None, :, None, :]
        bias = jnp.where(ok[None, None], bias, NEG)
        out.append(bias.reshape(bias.shape[:2] + (NA_QROWS * GRID_W, NA_KROWS * GRID_W)))
    return jnp.stack(out, axis=2)


def _softmax_pv(s_list, v_list):
    m = None
    for s in s_list:
        mm = jnp.max(s, axis=-1, keepdims=True)
        m = mm if m is None else jnp.maximum(m, mm)
    num, den = None, None
    for s, v in zip(s_list, v_list):
        p = jnp.exp(s - m)
        ssum = jnp.sum(p, axis=-1, keepdims=True)
        o = jnp.dot(p.astype(BF16), v, preferred_element_type=F32)
        num = o if num is None else num + o
        den = ssum if den is None else den + ssum
    return num / den


def _na_kernel(q_ref, k_ref, v_ref, cq_ref, ck_ref, cv_ref, bias_ref, cos_ref, sin_ref, y_ref, cy_ref,
               qs_ref, ks_ref, *, starts, types):
    lane = lax.broadcasted_iota(jnp.int32, (1, HEAD_DIM), 1)
    first = (lane % (HEAD_DIM // 2)) < HEAD_DIM // 4
    scale = HEAD_DIM ** -0.5

    def rope(t):
        partner = jnp.where(first, pltpu.roll(t, HEAD_DIM - HEAD_DIM // 4, axis=1),
                            pltpu.roll(t, HEAD_DIM // 4, axis=1))
        return t * cos_ref[...] + partner * sin_ref[...]

    qs_ref[...] = (rope(q_ref[...].astype(F32)) * scale).astype(BF16)
    ks_ref[...] = rope(k_ref[...].astype(F32)).astype(BF16)
    ck = ck_ref[...]
    cv = cv_ref[...]
    qrows = NA_QROWS * GRID_W
    krows = NA_KROWS * GRID_W
    for j, (u, tp) in enumerate(zip(starts, types)):
        qb = qs_ref[j * qrows:(j + 1) * qrows, :]
        kb = ks_ref[u * GRID_W:u * GRID_W + krows, :]
        vb = v_ref[u * GRID_W:u * GRID_W + krows, :]
        s_loc = lax.dot_general(qb, kb, _NT, preferred_element_type=F32) + bias_ref[tp]
        s_ctx = lax.dot_general(qb, ck, _NT, preferred_element_type=F32)
        y_ref[j * qrows:(j + 1) * qrows, :] = _softmax_pv([s_loc, s_ctx], [vb, cv]).astype(y_ref.dtype)
    s = lax.dot_general(cq_ref[...], ck, _NT, preferred_element_type=F32) * scale
    cy_ref[...] = _softmax_pv([s], [cv]).astype(cy_ref.dtype)


def _attention(ub, col0, n, lc, bsz, heads, bias, layer, layout):
    starts, types, _ = layout
    cos, sin = _na_tables(n)
    cb = col0 // HEAD_DIM
    ctx0 = bsz * n // lc

    def lat(k):
        return pl.BlockSpec((n, HEAD_DIM), lambda h, b: (b, cb + k * heads + h))

    def ctx(k):
        return pl.BlockSpec((lc, HEAD_DIM), lambda h, b: (ctx0 + b, cb + k * heads + h))

    tab = pl.BlockSpec((n, HEAD_DIM), lambda h, b: (0, 0))
    return pl.pallas_call(
        functools.partial(_na_kernel, starts=tuple(starts), types=tuple(types)),
        grid=(heads, bsz),
        in_specs=[lat(0), lat(1), lat(2), ctx(0), ctx(1), ctx(2),
                  pl.BlockSpec((None, None) + bias.shape[2:], lambda h, b: (layer, h, 0, 0, 0)), tab, tab],
        out_specs=[pl.BlockSpec((n, HEAD_DIM), lambda h, b: (b, h)),
                   pl.BlockSpec((lc, HEAD_DIM), lambda h, b: (b, h))],
        out_shape=[jax.ShapeDtypeStruct((bsz * n, heads * HEAD_DIM), BF16),
                   jax.ShapeDtypeStruct((bsz * lc, heads * HEAD_DIM), BF16)],
        scratch_shapes=[pltpu.VMEM((n, HEAD_DIM), BF16), pltpu.VMEM((n, HEAD_DIM), BF16)],
        compiler_params=_params("arbitrary", "arbitrary"), name="attention",
    )(ub, ub, ub, ub, ub, ub, bias, jnp.asarray(cos), jnp.asarray(sin))


def _route_kernel(afft_ref, slot_ref, *, n, ne, cap):
    bits = lax.bitcast_convert_type(afft_ref[...], jnp.int32)

    def bisect(i, thr):
        cand = thr | jnp.left_shift(jnp.int32(1), 30 - i)
        cnt = jnp.sum((bits >= cand).astype(F32), axis=1, keepdims=True)
        return jnp.where(cnt >= cap, cand, thr)

    thr = lax.fori_loop(0, 31, bisect, jnp.zeros((ne, 1), jnp.int32))
    above = bits > thr
    tied = bits == thr
    need = cap - jnp.sum(above.astype(F32), axis=1, keepdims=True)

    row = lax.broadcasted_iota(jnp.int32, (LANES, LANES), 0)
    colm = lax.broadcasted_iota(jnp.int32, (LANES, LANES), 1)
    before = (row < colm).astype(BF16)
    ones = jnp.ones((LANES, LANES), BF16)

    def prefix(x):
        xb = x.astype(BF16)
        outs, carry = [], jnp.zeros((ne, LANES), F32)
        for blk in range(n // LANES):
            xs = xb[:, blk * LANES:(blk + 1) * LANES]
            outs.append(jnp.dot(xs, before, preferred_element_type=F32) + carry)
            carry = carry + jnp.dot(xs, ones, preferred_element_type=F32)
        return jnp.concatenate(outs, axis=1)

    chosen = above | (tied & (prefix(tied.astype(F32)) < need))
    slot = prefix(chosen.astype(F32)).astype(jnp.int32)
    slot_ref[...] = jnp.where(chosen, slot, n)


def _route(aff, row0, n, bsz, cap):
    ne = aff.shape[1]
    at = jnp.swapaxes(aff[row0:row0 + bsz * n].reshape(bsz, n, ne), 1, 2)
    spec = pl.BlockSpec((None, ne, n), lambda b: (b, 0, 0))
    slot = pl.pallas_call(
        functools.partial(_route_kernel, n=n, ne=ne, cap=cap),
        grid=(bsz,), in_specs=[spec], out_specs=spec,
        out_shape=jax.ShapeDtypeStruct((bsz, ne, n), jnp.int32),
        compiler_params=_params("arbitrary"), name="expert_route",
    )(at)
    return at, slot


def _gather_kernel(slot_ref, afft_ref, h_ref, xg_ref, g_ref, *, cap):
    e = pl.program_id(1)
    want = lax.broadcasted_iota(jnp.int32, (cap, 1), 0)
    sel = slot_ref[pl.ds(e, 1), :] == want
    xg_ref[...] = jnp.dot(sel.astype(BF16), h_ref[...], preferred_element_type=F32).astype(xg_ref.dtype)
    g_ref[...] = jnp.sum(jnp.where(sel, afft_ref[pl.ds(e, 1), :], 0.0), axis=1, keepdims=True)


def _gather(slot, afft, h2, row_block0, n, bsz, cap):
    ne = slot.shape[1]
    d = h2.shape[1]
    row_spec = pl.BlockSpec((None, ne, n), lambda b, e: (b, 0, 0))
    return pl.pallas_call(
        functools.partial(_gather_kernel, cap=cap),
        grid=(bsz, ne),
        in_specs=[row_spec, row_spec, pl.BlockSpec((n, d), lambda b, e: (row_block0 + b, 0))],
        out_specs=[pl.BlockSpec((None, cap, d), lambda b, e: (e, b, 0)),
                   pl.BlockSpec((None, cap, 1), lambda b, e: (e, b, 0))],
        out_shape=[jax.ShapeDtypeStruct((ne, bsz * cap, d), BF16),
                   jax.ShapeDtypeStruct((ne, bsz * cap, 1), F32)],
        compiler_params=_params("arbitrary", "arbitrary"), name="expert_gather",
    )(slot, afft, h2)


def _ffn_kernel(*refs, ns, nf):
    x_refs = refs[0:2 * ns:2]
    g_refs = refs[1:2 * ns:2]
    wg_ref, wu_ref, wd_ref = refs[2 * ns:2 * ns + 3]
    o_refs = refs[2 * ns + 3:3 * ns + 3]
    wgb, wub, wdb = refs[3 * ns + 3:3 * ns + 6]
    acc_refs = refs[3 * ns + 6:]
    f = pl.program_id(1)
    wgb[...] = wg_ref[...].astype(BF16)
    wub[...] = wu_ref[...].astype(BF16)
    wdb[...] = wd_ref[...].astype(BF16)
    for k in range(ns):
        rows = x_refs[k].shape[0]
        step = min(FFN_ROWS, rows)
        for r0 in range(0, rows, step):
            rs = slice(r0, r0 + step)
            x = x_refs[k][rs, :]
            hg = jnp.dot(x, wgb[...], preferred_element_type=F32)
            hu = jnp.dot(x, wub[...], preferred_element_type=F32)
            hid = (hg * _sigmoid(hg) * hu).astype(BF16)
            part = jnp.dot(hid, wdb[...], preferred_element_type=F32)

            @pl.when(f == 0)
            def _(k=k, rs=rs, part=part):
                acc_refs[k][rs, :] = part

            @pl.when(f > 0)
            def _(k=k, rs=rs, part=part):
                acc_refs[k][rs, :] += part

            @pl.when(f == nf - 1)
            def _(k=k, rs=rs):
                o_refs[k][rs, :] = (acc_refs[k][rs, :] * g_refs[k][rs, :]).astype(o_refs[k].dtype)


def _expert_ffn(xgs, gates, w_gate, w_up, w_down, layer):
    ns = len(xgs)
    _, ne, d, ff = w_gate.shape
    tf = _lane_tile(ff, 256)
    nf = ff // tf
    in_specs, out_specs, out_shape, acc_scr, args = [], [], [], [], []
    for k in range(ns):
        rows = xgs[k].shape[1]
        in_specs.append(pl.BlockSpec((None, rows, d), lambda e, f: (e, 0, 0)))
        in_specs.append(pl.BlockSpec((None, rows, 1), lambda e, f: (e, 0, 0)))
        out_specs.append(pl.BlockSpec((None, rows, d), lambda e, f: (e, 0, 0)))
        out_shape.append(jax.ShapeDtypeStruct((ne, rows, d), BF16))
        acc_scr.append(pltpu.VMEM((rows, d), F32))
        args += [xgs[k], gates[k]]
    in_specs += [pl.BlockSpec((None, None, d, tf), lambda e, f: (layer, e, 0, f)),
                 pl.BlockSpec((None, None, d, tf), lambda e, f: (layer, e, 0, f)),
                 pl.BlockSpec((None, None, tf, d), lambda e, f: (layer, e, f, 0))]
    return pl.pallas_call(
        functools.partial(_ffn_kernel, ns=ns, nf=nf),
        grid=(ne, nf),
        in_specs=in_specs, out_specs=out_specs, out_shape=out_shape,
        scratch_shapes=[pltpu.VMEM((d, tf), BF16), pltpu.VMEM((d, tf), BF16), pltpu.VMEM((tf, d), BF16)] + acc_scr,
        compiler_params=_params("arbitrary", "arbitrary"), name="expert_ffn",
    )(*args, w_gate, w_up, w_down)


def _scatter_kernel(slot_ref, y_ref, x_ref, mod_ref, o_ref, *, ne, cap):
    sl = slot_ref[...]
    want = lax.broadcasted_iota(jnp.int32, (1, cap), 1)
    acc = None
    for e in range(ne):
        sel = (sl[:, e:e + 1] == want).astype(BF16)
        p = jnp.dot(sel, y_ref[e], preferred_element_type=F32)
        acc = p if acc is None else acc + p
    o_ref[...] = x_ref[...] + mod_ref[5:6, :] * acc


def _scatter(slot, y, x, mod, layer, mod_row0, row0, n, bsz, cap, tn):
    ne = slot.shape[1]
    d = x.shape[1]
    tm = min(512, n)
    slot_t = jnp.swapaxes(slot, 1, 2)
    xb0 = row0 // tm
    per_seq = n // tm
    return pl.pallas_call(
        functools.partial(_scatter_kernel, ne=ne, cap=cap),
        grid=(bsz, d // tn, per_seq),
        in_specs=[pl.BlockSpec((None, tm, ne), lambda b, j, i: (b, i, 0)),
                  pl.BlockSpec((ne, cap, tn), lambda b, j, i: (0, b, j)),
                  pl.BlockSpec((tm, tn), lambda b, j, i: (xb0 + b * per_seq + i, j)),
                  pl.BlockSpec((None, None, 6, tn), lambda b, j, i: (layer, mod_row0(b), 0, j))],
        out_specs=pl.BlockSpec((tm, tn), lambda b, j, i: (b * per_seq + i, j)),
        out_shape=jax.ShapeDtypeStruct((bsz * n, d), F32),
        compiler_params=_params("arbitrary", "arbitrary", "arbitrary"), name="expert_scatter",
    )(slot_t, y, x, mod)


def kernel(x, c, ctx, c_ctx, w_mod, b_mod, g_norm1, w_in, lb_param, g_hgrn, w_pool, pool_scale, rpb,
           w_branch, w_out, g_norm2, w_router, w_gate_e, w_up_e, w_down_e, g_final):
    bsz, n, d = x.shape
    lc = ctx.shape[1]
    depth = w_mod.shape[0]
    width = lb_param.shape[2]
    heads = width // HEAD_DIM
    ne = w_router.shape[2]
    n_lat = bsz * n
    n_ctx = bsz * lc
    total = n_lat + n_ctx
    assert bsz + 1 <= 8 and n % ROW_TILE == 0 and lc % ROW_TILE == 0 and n_lat % lc == 0
    assert w_pool.shape[-1] * 4 == width and w_branch.shape[2] == width
    tm = _row_tile(n, n_ctx)
    tn = min(512, d)

    lb_all = jnp.cumsum(jax.nn.softmax(lb_param.astype(F32), axis=1), axis=1)
    lb_all = lb_all - lb_all[:, :1]
    lbp_all = jnp.stack([jnp.log(lb_all), jnp.log1p(-lb_all), 1.0 - lb_all], axis=2)
    lbp_all = lbp_all.reshape(2, depth, 3, heads, LANES).transpose(0, 1, 3, 2, 4)

    c8 = jnp.concatenate([c, c_ctx[None], jnp.zeros((8 - bsz - 1, d), F32)], axis=0)
    mod = _modulation(c8, w_mod, b_mod).reshape(depth, 8, 6, d)

    cm3, masks = _hgrn_constants()
    cm3 = jnp.asarray(cm3, BF16)
    masks = jnp.asarray(masks, F32)
    na_layout = _na_block_layout(n // GRID_W)
    na_bias = _na_bias(rpb, na_layout[2])
    lanes_e = -(-ne // LANES) * LANES
    w_router_p = jnp.pad(w_router, ((0, 0), (0, 0), (0, lanes_e - ne)))

    x_lat = x.reshape(n_lat, d)
    x_ctx = ctx.reshape(n_ctx, d)
    zero_state = jnp.zeros((bsz, heads, LANES, LANES), F32)
    cap = EC_CAPACITY * n // ne
    cap_c = EC_CAPACITY * lc // ne
    a_cols = 5 * width
    b_cols = w_in.shape[2] - a_cols
    gate_col0 = 4 * width
    rt_all = _Rows(tm, n, bsz, n_lat, total)
    rt_lat = _Rows(tm, n, bsz, n_lat, n_lat)
    rn_all = _Rows(ROW_TILE, n, bsz, n_lat, total)
    rn_lat = _Rows(ROW_TILE, n, bsz, n_lat, n_lat)

    for l in range(depth):
        last = l == depth - 1
        rt, rn = (rt_lat, rn_lat) if last else (rt_all, rn_all)
        gain_h = g_hgrn[l].reshape(heads, 1, LANES)

        h = _norm1(x_lat, x_ctx, g_norm1[l], mod, l, rn_all)
        ua = _matmul(h, w_in, l, 0, a_cols, F32, tm, tn)
        ub = _matmul(h, w_in, l, a_cols, b_cols, BF16, tm, tn)

        cy_a, s_f, s_b = _hgrn(ua, n_lat // lc, lc, bsz, heads, lbp_all[:, l], gain_h, zero_state, zero_state,
                               cm3, masks)
        y_a, _, _ = _hgrn(ua, 0, n, bsz, heads, lbp_all[:, l], gain_h, s_f, s_b, cm3, masks)
        y_b = _pool(ub, 0, n, bsz, w_pool, pool_scale, l)
        y_c, cy_c = _attention(ub, width, n, lc, bsz, heads, na_bias, l, na_layout)
        ys_lat = (y_a, y_b, y_c)
        ys_ctx = ys_lat if last else (cy_a, _pool(ub, n_lat // lc, lc, bsz, w_pool, pool_scale, l), cy_c)

        merged = _merge(ys_lat, ys_ctx, ub, gate_col0, w_branch, l, rt, tn)
        x_mid = _out_proj(merged, w_out, l, x_lat, x_ctx, mod, rt, tn)

        h2, aff = _norm2(x_mid, g_norm2[l], mod, l, rn, w_router_p, ne)
        afft, slot = _route(aff, 0, n, bsz, cap)
        xg, gate = _gather(slot, afft, h2, 0, n, bsz, cap)
        xgs, gates = [xg], [gate]
        if not last:
            afft_c, slot_c = _route(aff, n_lat, lc, bsz, cap_c)
            xg_c, gate_c = _gather(slot_c, afft_c, h2, n_lat // lc, lc, bsz, cap_c)
            xgs, gates = xgs + [xg_c], gates + [gate_c]
        ys = _expert_ffn(xgs, gates, w_gate_e, w_up_e, w_down_e, l)
        x_lat = _scatter(slot, ys[0], x_mid, mod, l, lambda b: b, 0, n, bsz, cap, tn)
        if not last:
            x_ctx = _scatter(slot_c, ys[1], x_mid, mod, l, lambda b: bsz, n_lat, lc, bsz, cap_c, tn)

    return _final_norm(x_lat, g_final).reshape(bsz, n, d)
```

```python
import functools
import math

import numpy as np
import jax
import jax.numpy as jnp
from jax import lax
from jax.experimental import pallas as pl
from jax.experimental.pallas import tpu as pltpu

F32 = jnp.float32
BF16 = jnp.bfloat16
EPS = 1e-6
LANES = 128
HEAD_DIM = 128
CHUNK = 64
HGRN_GROUP = 4
GRID_W = 64
POOL_WINDOWS = (2, 4, 8, 16)
POOL_PAD = 16
NA_ROWS_MAX = 8
NA_COLS = 16
NA_QROWS = 4
NA_KROWS = 12
ROPE_THETA = 10000.0
EC_CAPACITY = 2
FFN_ROWS = 512
NEG = -1e30
ROW_TILE = 256
LOG2E = 1.4426950408889634
VMEM_LIMIT = 56 * 2 ** 20


def _params(*sem):
    return pltpu.CompilerParams(dimension_semantics=sem, vmem_limit_bytes=VMEM_LIMIT)


def _sigmoid(x):
    return 1.0 / (1.0 + jnp.exp(-x))


def _neg_abs(x):
    bits = lax.bitcast_convert_type(x, jnp.uint32) | jnp.uint32(0x80000000)
    return lax.bitcast_convert_type(bits, F32)


def _lane_tile(n, limit):
    t = min(limit, n) // LANES * LANES
    while n % t:
        t -= LANES
    return t


def _row_tile(n, ctx_rows):
    tm = 1024
    while n % tm or ctx_rows % tm:
        tm //= 2
    return tm


class _Rows:
    def __init__(self, tm, n, bsz, n_lat, rows):
        self.tm, self.n, self.bsz = tm, n, bsz
        self.lat_tiles = n_lat // tm
        self.tiles = rows // tm
        self.ctx_tiles = max(self.tiles - self.lat_tiles, 1)

    def lat(self, i):
        return jnp.minimum(i, self.lat_tiles - 1)

    def ctx(self, i):
        return jnp.clip(i - self.lat_tiles, 0, self.ctx_tiles - 1)

    def mod_row(self, i):
        return jnp.minimum(i * self.tm // self.n, self.bsz)


def _pick(is_lat, lat_ref, ctx_ref):
    return jnp.where(is_lat, lat_ref[...], ctx_ref[...])


def _mod_kernel(c_ref, w_ref, b_ref, o_ref):
    c = c_ref[...]
    sc = (c * _sigmoid(c)).astype(BF16)
    o_ref[...] = jnp.dot(sc, w_ref[...].astype(BF16), preferred_element_type=F32) + b_ref[...]


def _modulation(c8, w_mod, b_mod):
    depth, d, n6 = w_mod.shape
    tn = _lane_tile(n6, 1024)
    return pl.pallas_call(
        _mod_kernel,
        grid=(depth, n6 // tn),
        in_specs=[pl.BlockSpec((8, d), lambda l, j: (0, 0)),
                  pl.BlockSpec((None, d, tn), lambda l, j: (l, 0, j)),
                  pl.BlockSpec((None, 1, tn), lambda l, j: (l, 0, j))],
        out_specs=pl.BlockSpec((None, 8, tn), lambda l, j: (l, 0, j)),
        out_shape=jax.ShapeDtypeStruct((depth, 8, n6), F32),
        compiler_params=_params("arbitrary", "arbitrary"),
        name="modulation",
    )(c8, w_mod, b_mod.reshape(depth, 1, n6))


def _rmsnorm(x, gain):
    return x * lax.rsqrt(jnp.mean(x * x, axis=-1, keepdims=True) + EPS) * gain


def _modulate(y, mod_ref, shift_idx):
    return y * (1.0 + mod_ref[shift_idx + 1:shift_idx + 2, :]) + mod_ref[shift_idx:shift_idx + 1, :]


def _norm1_kernel(xl_ref, xc_ref, g_ref, mod_ref, o_ref, *, lat_tiles):
    x = _pick(pl.program_id(0) < lat_tiles, xl_ref, xc_ref)
    o_ref[...] = _modulate(_rmsnorm(x, g_ref[...]), mod_ref, 0).astype(o_ref.dtype)


def _norm2_kernel(x_ref, g_ref, mod_ref, wr_ref, o_ref, aff_ref):
    h = _modulate(_rmsnorm(x_ref[...], g_ref[...]), mod_ref, 3)
    o_ref[...] = h.astype(o_ref.dtype)
    logits = jnp.dot(h, wr_ref[...], precision=lax.Precision.HIGHEST, preferred_element_type=F32)
    logits = logits[:, :aff_ref.shape[1]]
    e = jnp.exp(logits - jnp.max(logits, axis=-1, keepdims=True))
    aff_ref[...] = e / jnp.sum(e, axis=-1, keepdims=True)


def _final_norm_kernel(x_ref, g_ref, o_ref):
    o_ref[...] = _rmsnorm(x_ref[...], g_ref[...])


def _norm1(x_lat, x_ctx, gain, mod, layer, rt):
    d = x_lat.shape[1]
    return pl.pallas_call(
        functools.partial(_norm1_kernel, lat_tiles=rt.lat_tiles),
        grid=(rt.tiles,),
        in_specs=[pl.BlockSpec((rt.tm, d), lambda i: (rt.lat(i), 0)),
                  pl.BlockSpec((rt.tm, d), lambda i: (rt.ctx(i), 0)),
                  pl.BlockSpec((1, d), lambda i: (0, 0)),
                  pl.BlockSpec((None, None, 6, d), lambda i: (layer, rt.mod_row(i), 0, 0))],
        out_specs=pl.BlockSpec((rt.tm, d), lambda i: (i, 0)),
        out_shape=jax.ShapeDtypeStruct((rt.tiles * rt.tm, d), BF16),
        compiler_params=_params("arbitrary"), name="norm1",
    )(x_lat, x_ctx, gain.reshape(1, d), mod)


def _norm2(x, gain, mod, layer, rt, w_router_padded, ne):
    d = x.shape[1]
    rows = rt.tiles * rt.tm
    return pl.pallas_call(
        _norm2_kernel,
        grid=(rt.tiles,),
        in_specs=[pl.BlockSpec((rt.tm, d), lambda i: (i, 0)),
                  pl.BlockSpec((1, d), lambda i: (0, 0)),
                  pl.BlockSpec((None, None, 6, d), lambda i: (layer, rt.mod_row(i), 0, 0)),
                  pl.BlockSpec((None, d, w_router_padded.shape[2]), lambda i: (layer, 0, 0))],
        out_specs=[pl.BlockSpec((rt.tm, d), lambda i: (i, 0)), pl.BlockSpec((rt.tm, ne), lambda i: (i, 0))],
        out_shape=[jax.ShapeDtypeStruct((rows, d), BF16), jax.ShapeDtypeStruct((rows, ne), F32)],
        compiler_params=_params("arbitrary"), name="norm2_router",
    )(x, gain.reshape(1, d), mod, w_router_padded)


def _final_norm(x, gain):
    rows, d = x.shape
    tm = ROW_TILE
    return pl.pallas_call(
        _final_norm_kernel,
        grid=(rows // tm,),
        in_specs=[pl.BlockSpec((tm, d), lambda i: (i, 0)), pl.BlockSpec((1, d), lambda i: (0, 0))],
        out_specs=pl.BlockSpec((tm, d), lambda i: (i, 0)),
        out_shape=jax.ShapeDtypeStruct((rows, d), F32),
        compiler_params=_params("arbitrary"), name="final_norm",
    )(x, gain.reshape(1, d))


def _mm_kernel(a_ref, w_ref, o_ref, wbf_ref):
    @pl.when(pl.program_id(1) == 0)
    def _():
        wbf_ref[...] = w_ref[...].astype(BF16)
    o_ref[...] = jnp.dot(a_ref[...], wbf_ref[...], preferred_element_type=F32).astype(o_ref.dtype)


def _matmul(a, w_all, layer, col0, ncols, out_dtype, tm, tn):
    m, k = a.shape
    off = col0 // tn
    return pl.pallas_call(
        _mm_kernel,
        grid=(ncols // tn, m // tm),
        in_specs=[pl.BlockSpec((tm, k), lambda j, i: (i, 0)),
                  pl.BlockSpec((None, k, tn), lambda j, i: (layer, 0, j + off))],
        out_specs=pl.BlockSpec((tm, tn), lambda j, i: (i, j)),
        out_shape=jax.ShapeDtypeStruct((m, ncols), out_dtype),
        scratch_shapes=[pltpu.VMEM((k, tn), BF16)],
        compiler_params=_params("arbitrary", "arbitrary"), name="matmul",
    )(a, w_all)


def _out_proj_kernel(a_ref, w_ref, xl_ref, xc_ref, mod_ref, o_ref, wbf_ref, *, lat_tiles):
    @pl.when(pl.program_id(1) == 0)
    def _():
        wbf_ref[...] = w_ref[...].astype(BF16)
    acc = jnp.dot(a_ref[...], wbf_ref[...], preferred_element_type=F32)
    x = _pick(pl.program_id(1) < lat_tiles, xl_ref, xc_ref)
    o_ref[...] = x + mod_ref[2:3, :] * acc


def _out_proj(a, w_all, layer, x_lat, x_ctx, mod, rt, tn):
    k = a.shape[1]
    d = w_all.shape[2]
    tm = rt.tm
    return pl.pallas_call(
        functools.partial(_out_proj_kernel, lat_tiles=rt.lat_tiles),
        grid=(d // tn, rt.tiles),
        in_specs=[pl.BlockSpec((tm, k), lambda j, i: (i, 0)),
                  pl.BlockSpec((None, k, tn), lambda j, i: (layer, 0, j)),
                  pl.BlockSpec((tm, tn), lambda j, i: (rt.lat(i), j)),
                  pl.BlockSpec((tm, tn), lambda j, i: (rt.ctx(i), j)),
                  pl.BlockSpec((None, None, 6, tn), lambda j, i: (layer, rt.mod_row(i), 0, j))],
        out_specs=pl.BlockSpec((tm, tn), lambda j, i: (i, j)),
        out_shape=jax.ShapeDtypeStruct((rt.tiles * tm, d), F32),
        scratch_shapes=[pltpu.VMEM((k, tn), BF16)],
        compiler_params=_params("arbitrary", "arbitrary"), name="out_proj",
    )(a, w_all, x_lat, x_ctx, mod)


def _merge_kernel(ya_ref, yb_ref, yc_ref, cya_ref, cyb_ref, cyc_ref, g0_ref, g1_ref, g2_ref, w_ref, o_ref, wbf_ref,
                  *, lat_tiles):
    @pl.when(pl.program_id(1) == 0)
    def _():
        wbf_ref[...] = w_ref[...].astype(BF16)
    is_lat = pl.program_id(1) < lat_tiles
    acc = None
    for j, (y_ref, cy_ref, g_ref) in enumerate(((ya_ref, cya_ref, g0_ref), (yb_ref, cyb_ref, g1_ref),
                                                (yc_ref, cyc_ref, g2_ref))):
        p = jnp.dot(_pick(is_lat, y_ref, cy_ref), wbf_ref[j], preferred_element_type=F32)
        t = _sigmoid(g_ref[...].astype(F32)) * p
        acc = t if acc is None else acc + t
    o_ref[...] = acc.astype(o_ref.dtype)


def _merge(ys_lat, ys_ctx, ub, gate_col0, w_branch, layer, rt, tn):
    bw = ys_lat[0].shape[1]
    d = w_branch.shape[3]
    tm = rt.tm
    lat_spec = pl.BlockSpec((tm, bw), lambda j, i: (rt.lat(i), 0))
    ctx_spec = pl.BlockSpec((tm, bw), lambda j, i: (rt.ctx(i), 0))

    def gate_spec(k):
        off = (gate_col0 + k * d) // tn
        return pl.BlockSpec((tm, tn), lambda j, i: (i, off + j))

    return pl.pallas_call(
        functools.partial(_merge_kernel, lat_tiles=rt.lat_tiles),
        grid=(d // tn, rt.tiles),
        in_specs=[lat_spec] * 3 + [ctx_spec] * 3 + [gate_spec(0), gate_spec(1), gate_spec(2),
                  pl.BlockSpec((None, 3, bw, tn), lambda j, i: (layer, 0, 0, j))],
        out_specs=pl.BlockSpec((tm, tn), lambda j, i: (i, j)),
        out_shape=jax.ShapeDtypeStruct((rt.tiles * tm, d), BF16),
        scratch_shapes=[pltpu.VMEM((3, bw, tn), BF16)],
        compiler_params=_params("arbitrary", "arbitrary"), name="merge",
    )(*ys_lat, *ys_ctx, ub, ub, ub, w_branch)


_HGRN_BLOCKS = (32, 16, 8, 4, 2, 1)


def _hgrn_constants():
    c = CHUNK
    t = np.arange(c)
    tri = (t[:, None] >= t[None, :]).astype(np.float32)
    cms, masks, upper = [tri], [], []
    for m in _HGRN_BLOCKS:
        ref = (t // (2 * m)) * (2 * m) + m - 1
        cms.append(tri[ref])
        same = (t[:, None] // (2 * m)) == (t[None, :] // (2 * m))
        up = (t % (2 * m)) >= m
        masks.append(same & (up[:, None] != up[None, :]))
        upper.append(np.broadcast_to(up[:, None], (c, LANES)))
    masks.append(np.eye(c, dtype=bool))
    cm_f = np.concatenate(cms, axis=0)
    cm_b = cm_f.reshape(-1, c, c)[:, ::-1, ::-1].reshape(-1, c)
    cm = np.stack([cm_f, cm_b])
    return (np.concatenate([cm, cm, cm], axis=2), np.stack(masks).astype(np.float32),
            np.stack(upper).astype(np.float32))


def _hgrn_gates(z, lbp):
    ez = jnp.exp(_neg_abs(z))
    one_p = 1.0 + ez
    log_sig = jnp.minimum(z, 0.0) - jnp.log(one_p)
    k = lbp[2:3, :] * jnp.where(z >= 0.0, ez, 1.0) / one_p
    a = lbp[0:1, :]
    cc = lbp[1:2, :] + log_sig
    logf = jnp.maximum(a, cc) + jnp.log(1.0 + jnp.exp(_neg_abs(a - cc)))
    return logf * LOG2E, k


def _split3(x):
    g1 = x.astype(BF16)
    r1 = x - g1.astype(F32)
    g2 = r1.astype(BF16)
    g3 = (r1 - g2.astype(F32)).astype(BF16)
    return jnp.concatenate([g1, g2, g3], axis=0)


_NT = (((1,), (1,)), ((), ()))
_TN = (((0,), (0,)), ((), ()))


def _hgrn_kernel(q_ref, ff_ref, fb_ref, i_ref, g_ref, lbf_ref, lbb_ref, gain_ref, cm_ref, mask_ref, up_ref,
                 sf0_ref, sb0_ref, y_ref, sf_ref, sb_ref, o_scr, qt_scr, upd_scr, dec_scr, st_scr, *, nc):
    z_refs = (ff_ref, fb_ref)
    lb_refs = (lbf_ref, lbb_ref)
    nl = len(_HGRN_BLOCKS)
    last_row = (CHUNK - 1, 0)

    def local_pass(it, carry):
        rows = [pl.ds(pl.multiple_of((it * HGRN_GROUP + s) * CHUNK, CHUNK), CHUNK) for s in range(HGRN_GROUP)]
        gates = [[_hgrn_gates(z_refs[d][r, :], lb_refs[d][...]) for r in rows] for d in range(2)]
        sums = [jnp.dot(cm_ref[d], jnp.concatenate([_split3(lf) for lf, _ in gates[d]], axis=1),
                        preferred_element_type=F32) for d in range(2)]
        up = up_ref[...] > 0.5
        for s, r in enumerate(rows):
            lanes = slice(s * LANES, (s + 1) * LANES)
            qr = q_ref[r, :]
            q = qr * _sigmoid(qr)
            vb = i_ref[r, :].astype(BF16)
            b = [sums[d][0:CHUNK, lanes] for d in range(2)]
            k = [gates[d][s][1] for d in range(2)]
            lvl = [jnp.exp2(_neg_abs(b[d][None] - sums[d][CHUNK:, lanes].reshape(nl, CHUNK, LANES)))
                   for d in range(2)]
            q_dec = jnp.where(up, lvl[0], lvl[1])
            k_dec = jnp.where(up, k[1][None] * lvl[1], k[0][None] * lvl[0])
            qs = jnp.concatenate([q[None] * q_dec, q[None]], axis=0).astype(BF16)
            ks = jnp.concatenate([k_dec, (k[0] + k[1])[None]], axis=0).astype(BF16)
            sc = jnp.einsum("ltk,lsk->lts", qs, ks, preferred_element_type=F32)
            amat = jnp.sum(sc * mask_ref[...], axis=0)
            o_scr[r, :] = jnp.dot(amat.astype(BF16), vb, preferred_element_type=F32)
            tot = [b[d][last_row[d]:last_row[d] + 1, :] for d in range(2)]
            kd = jnp.concatenate([(k[d] * jnp.exp2(tot[d] - b[d])).astype(BF16) for d in range(2)], axis=1)
            upd = lax.dot_general(vb, kd, _TN, preferred_element_type=F32)
            j = it * HGRN_GROUP + s
            for d in range(2):
                qt_scr[d, r, :] = (q * jnp.exp2(b[d])).astype(BF16)
                upd_scr[d, j] = upd[:, d * LANES:(d + 1) * LANES]
                dec_scr[d, pl.ds(j, 1), :] = jnp.exp2(tot[d])
        return carry

    lax.fori_loop(0, nc // HGRN_GROUP, local_pass, 0)

    st_scr[0] = sf0_ref[...]
    st_scr[1] = sb0_ref[...]

    def state_pass(it, carry):
        for d in range(2):
            j = it if d == 0 else nc - 1 - it
            rows = pl.ds(pl.multiple_of(j * CHUNK, CHUNK), CHUNK)
            st = st_scr[d]
            o_scr[rows, :] += lax.dot_general(qt_scr[d, rows, :], st.astype(BF16), _NT, preferred_element_type=F32)
            st_scr[d] = st * dec_scr[d, pl.ds(j, 1), :] + upd_scr[d, j]
        return carry

    lax.fori_loop(0, nc, state_pass, 0, unroll=2)
    sf_ref[...] = st_scr[0]
    sb_ref[...] = st_scr[1]
    o = o_scr[...]
    o = o * lax.rsqrt(jnp.mean(o * o, axis=-1, keepdims=True) + EPS) * gain_ref[...]
    g = g_ref[...]
    y_ref[...] = (o * (g * _sigmoid(g))).astype(y_ref.dtype)


def _hgrn(ua, row_block0, n, bsz, heads, lbp, gain, sf0, sb0, consts):
    nc = n // CHUNK
    assert nc % HGRN_GROUP == 0 and nc % 2 == 0

    def col(k):
        return pl.BlockSpec((n, LANES), lambda b, h: (row_block0 + b, k * heads + h))

    lb_spec = lambda d: pl.BlockSpec((None, None, 3, LANES), lambda b, h: (d, h, 0, 0))
    st_spec = pl.BlockSpec((None, None, LANES, LANES), lambda b, h: (b, h, 0, 0))
    st_shape = jax.ShapeDtypeStruct((bsz, heads, LANES, LANES), F32)
    const_specs = [pl.BlockSpec(a.shape, lambda b, h: (0, 0, 0)) for a in consts]
    return pl.pallas_call(
        functools.partial(_hgrn_kernel, nc=nc),
        grid=(bsz, heads),
        in_specs=[col(0), col(1), col(2), col(3), col(4), lb_spec(0), lb_spec(1),
                  pl.BlockSpec((None, 1, LANES), lambda b, h: (h, 0, 0))] + const_specs + [st_spec, st_spec],
        out_specs=[pl.BlockSpec((n, LANES), lambda b, h: (b, h)), st_spec, st_spec],
        out_shape=[jax.ShapeDtypeStruct((bsz * n, heads * LANES), BF16), st_shape, st_shape],
        scratch_shapes=[pltpu.VMEM((n, LANES), F32), pltpu.VMEM((2, n, LANES), BF16),
                        pltpu.VMEM((2, nc, LANES, LANES), F32), pltpu.VMEM((2, nc, LANES), F32),
                        pltpu.VMEM((2, LANES, LANES), F32)],
        compiler_params=_params("arbitrary", "arbitrary"), name="hgrn2",
    )(ua, ua, ua, ua, ua, lbp, lbp, gain, *consts, sf0, sb0)


def _pool_kernel(u_ref, w_ref, s_ref, y_ref, pad_ref, *, n, group):
    pos = lax.broadcasted_iota(jnp.int32, (n, 1), 0)
    zeros = jnp.zeros((POOL_PAD, pad_ref.shape[1]), F32)
    pad_ref[0:POOL_PAD, :] = zeros
    pad_ref[POOL_PAD + n:2 * POOL_PAD + n, :] = zeros
    pad_ref[POOL_PAD:POOL_PAD + n, :] = u_ref[...].astype(F32)
    for gi, w in enumerate(POOL_WINDOWS):
        cols = slice(gi * group, (gi + 1) * group)
        acc = None
        for dlt in range(-(w // 2), w // 2):
            t = pad_ref[POOL_PAD + dlt:POOL_PAD + dlt + n, cols]
            acc = t if acc is None else acc + t
        lo = jnp.maximum(pos - w // 2, 0)
        hi = jnp.minimum(pos + w // 2 - 1, n - 1)
        cnt = (hi - lo + 1).astype(F32)
        dd = acc / cnt - pad_ref[POOL_PAD:POOL_PAD + n, cols]
        y = jnp.dot(dd.astype(BF16), w_ref[gi].astype(BF16), preferred_element_type=F32)
        y_ref[:, cols] = (y * s_ref[:, cols]).astype(y_ref.dtype)


def _pool(ub, row_block0, n, bsz, w_pool, scale, layer):
    group = w_pool.shape[-1]
    width = 4 * group
    return pl.pallas_call(
        functools.partial(_pool_kernel, n=n, group=group),
        grid=(bsz,),
        in_specs=[pl.BlockSpec((n, width), lambda b: (row_block0 + b, 0)),
                  pl.BlockSpec((None, 4, group, group), lambda b: (layer, 0, 0, 0)),
                  pl.BlockSpec((None, 1, width), lambda b: (layer, 0, 0))],
        out_specs=pl.BlockSpec((n, width), lambda b: (b, 0)),
        out_shape=jax.ShapeDtypeStruct((bsz * n, width), BF16),
        scratch_shapes=[pltpu.VMEM((n + 2 * POOL_PAD, width), F32)],
        compiler_params=_params("arbitrary"), name="pool",
    )(ub, w_pool, scale.reshape(scale.shape[0], 1, width))


def _na_tables(n):
    pos = np.arange(n)
    half = HEAD_DIM // 2
    inv_freq = ROPE_THETA ** (-np.arange(0, half, 2, dtype=np.float64) / half)
    lane = np.arange(HEAD_DIM)
    p = np.where(lane[None, :] < half, (pos // GRID_W)[:, None], (pos % GRID_W)[:, None]).astype(np.float64)
    ang = p * inv_freq[lane % (half // 2)][None, :]
    sign = np.where((lane % half) < half // 2, -1.0, 1.0)[None, :]
    return np.cos(ang).astype(np.float32), (np.sin(ang) * sign).astype(np.float32)


def _na_block_layout(rows):
    kr = NA_ROWS_MAX
    nblk = rows // NA_QROWS
    starts, patterns, types = [], [], []
    for j in range(nblk):
        u = int(np.clip(NA_QROWS * j - kr // 2, 0, rows - NA_KROWS))
        r = NA_QROWS * j + np.arange(NA_QROWS)
        start_r = np.clip(r - kr // 2, 0, rows - kr)
        kabs = u + np.arange(NA_KROWS)
        valid = (kabs[None, :] >= start_r[:, None]) & (kabs[None, :] < start_r[:, None] + kr)
        assert valid.sum(axis=1).min() == kr
        dr = np.clip(kabs[None, :] - r[:, None] + NA_ROWS_MAX - 1, 0, 2 * NA_ROWS_MAX - 2)
        key = (valid.tobytes(), dr.tobytes())
        keys = [p[0] for p in patterns]
        if key not in keys:
            patterns.append((key, valid, dr))
        types.append([p[0] for p in patterns].index(key))
        starts.append(u)
    return starts, types, [(p[1], p[2]) for p in patterns]


NA_DR = 2 * NA_ROWS_MAX - 1


def _na_bias_table(rpb):
    qcol = np.arange(GRID_W)
    col_start = np.clip(qcol - NA_COLS // 2, 0, GRID_W - NA_COLS)
    kcol = np.arange(GRID_W)
    col_mask = (kcol[None, :] >= col_start[:, None]) & (kcol[None, :] < col_start[:, None] + NA_COLS)
    dc = np.clip(kcol[None, :] - qcol[:, None] + NA_COLS - 1, 0, 2 * NA_COLS - 2)
    col_hot = (dc[None] == np.arange(2 * NA_COLS - 1)[:, None, None]).astype(np.float32)
    by_col = jnp.einsum("lhab,bqk->lhaqk", rpb.astype(F32), col_hot, precision=lax.Precision.HIGHEST)
    by_col = jnp.where(col_mask, by_col, NEG)
    masked = jnp.full(by_col.shape[:2] + (1, GRID_W, GRID_W), NEG, F32)
    table = jnp.concatenate([by_col, masked], axis=2)
    return jnp.concatenate([table, table], axis=-1)


def _na_pieces(patterns):
    out = []
    for valid, dr in patterns:
        idx = np.where(valid, dr, NA_DR)
        out.append(tuple(tuple((int(idx[r, 2 * p]), int(idx[r, 2 * p + 1])) for p in range(NA_KROWS // 2))
                         for r in range(NA_QROWS)))
    return tuple(out)


def _softmax_pv(s_list, v_list):
    m = None
    for s in s_list:
        mm = jnp.max(s, axis=-1, keepdims=True)
        m = mm if m is None else jnp.maximum(m, mm)
    num, den = None, None
    for s, v in zip(s_list, v_list):
        p = jnp.exp(s - m)
        ssum = jnp.sum(p, axis=-1, keepdims=True)
        o = jnp.dot(p.astype(BF16), v, preferred_element_type=F32)
        num = o if num is None else num + o
        den = ssum if den is None else den + ssum
    return num / den


def _na_kernel(q_ref, k_ref, v_ref, cq_ref, ck_ref, cv_ref, tab_ref, cos_ref, sin_ref, y_ref, cy_ref,
               qs_ref, ks_ref, bias_ref, *, starts, types, pieces):
    lane = lax.broadcasted_iota(jnp.int32, (1, HEAD_DIM), 1)
    first = (lane % (HEAD_DIM // 2)) < HEAD_DIM // 4
    scale = HEAD_DIM ** -0.5

    @pl.when(pl.program_id(1) == 0)
    def _():
        even = lane < GRID_W
        for tp, by_row in enumerate(pieces):
            for r, by_pair in enumerate(by_row):
                for p, (ie, io) in enumerate(by_pair):
                    bias_ref[tp, r * GRID_W:(r + 1) * GRID_W, 2 * p * GRID_W:2 * (p + 1) * GRID_W] = (
                        jnp.where(even, tab_ref[ie], tab_ref[io]))

    def rope(t):
        partner = jnp.where(first, pltpu.roll(t, HEAD_DIM - HEAD_DIM // 4, axis=1),
                            pltpu.roll(t, HEAD_DIM // 4, axis=1))
        return t * cos_ref[...] + partner * sin_ref[...]

    qs_ref[...] = (rope(q_ref[...].astype(F32)) * scale).astype(BF16)
    ks_ref[...] = rope(k_ref[...].astype(F32)).astype(BF16)
    ck = ck_ref[...]
    cv = cv_ref[...]
    qrows = NA_QROWS * GRID_W
    krows = NA_KROWS * GRID_W
    for j, (u, tp) in enumerate(zip(starts, types)):
        qb = qs_ref[j * qrows:(j + 1) * qrows, :]
        kb = ks_ref[u * GRID_W:u * GRID_W + krows, :]
        vb = v_ref[u * GRID_W:u * GRID_W + krows, :]
        s_loc = lax.dot_general(qb, kb, _NT, preferred_element_type=F32) + bias_ref[tp]
        s_ctx = lax.dot_general(qb, ck, _NT, preferred_element_type=F32)
        y_ref[j * qrows:(j + 1) * qrows, :] = _softmax_pv([s_loc, s_ctx], [vb, cv]).astype(y_ref.dtype)
    s = lax.dot_general(cq_ref[...], ck, _NT, preferred_element_type=F32) * scale
    cy_ref[...] = _softmax_pv([s], [cv]).astype(cy_ref.dtype)


def _attention(ub, col0, n, lc, bsz, heads, table, layer, layout):
    starts, types, patterns = layout
    cos, sin = _na_tables(n)
    cb = col0 // HEAD_DIM
    ctx0 = bsz * n // lc

    def lat(k):
        return pl.BlockSpec((n, HEAD_DIM), lambda h, b: (b, cb + k * heads + h))

    def ctx(k):
        return pl.BlockSpec((lc, HEAD_DIM), lambda h, b: (ctx0 + b, cb + k * heads + h))

    tab = pl.BlockSpec((n, HEAD_DIM), lambda h, b: (0, 0))
    return pl.pallas_call(
        functools.partial(_na_kernel, starts=tuple(starts), types=tuple(types), pieces=_na_pieces(patterns)),
        grid=(heads, bsz),
        in_specs=[lat(0), lat(1), lat(2), ctx(0), ctx(1), ctx(2),
                  pl.BlockSpec((None, None) + table.shape[2:], lambda h, b: (layer, h, 0, 0, 0)), tab, tab],
        out_specs=[pl.BlockSpec((n, HEAD_DIM), lambda h, b: (b, h)),
                   pl.BlockSpec((lc, HEAD_DIM), lambda h, b: (b, h))],
        out_shape=[jax.ShapeDtypeStruct((bsz * n, heads * HEAD_DIM), BF16),
                   jax.ShapeDtypeStruct((bsz * lc, heads * HEAD_DIM), BF16)],
        scratch_shapes=[pltpu.VMEM((n, HEAD_DIM), BF16), pltpu.VMEM((n, HEAD_DIM), BF16),
                        pltpu.VMEM((len(patterns), NA_QROWS * GRID_W, NA_KROWS * GRID_W), F32)],
        compiler_params=_params("arbitrary", "arbitrary"), name="attention",
    )(ub, ub, ub, ub, ub, ub, table, jnp.asarray(cos), jnp.asarray(sin))


def _route_kernel(afft_ref, slot_ref, *, n, ne, cap):
    bits = lax.bitcast_convert_type(afft_ref[...], jnp.int32)

    def bisect(i, thr):
        cand = thr | jnp.left_shift(jnp.int32(1), 30 - i)
        cnt = jnp.sum((bits >= cand).astype(F32), axis=1, keepdims=True)
        return jnp.where(cnt >= cap, cand, thr)

    thr = lax.fori_loop(0, 31, bisect, jnp.zeros((ne, 1), jnp.int32))
    above = bits > thr
    tied = bits == thr
    need = cap - jnp.sum(above.astype(F32), axis=1, keepdims=True)

    row = lax.broadcasted_iota(jnp.int32, (LANES, LANES), 0)
    colm = lax.broadcasted_iota(jnp.int32, (LANES, LANES), 1)
    before = (row < colm).astype(BF16)
    ones = jnp.ones((LANES, LANES), BF16)

    def prefix(x):
        xb = x.astype(BF16)
        outs, carry = [], jnp.zeros((ne, LANES), F32)
        for blk in range(n // LANES):
            xs = xb[:, blk * LANES:(blk + 1) * LANES]
            outs.append(jnp.dot(xs, before, preferred_element_type=F32) + carry)
            carry = carry + jnp.dot(xs, ones, preferred_element_type=F32)
        return jnp.concatenate(outs, axis=1)

    chosen = above | (tied & (prefix(tied.astype(F32)) < need))
    slot = prefix(chosen.astype(F32)).astype(jnp.int32)
    slot_ref[...] = jnp.where(chosen, slot, n)


def _route(aff, row0, n, bsz, cap):
    ne = aff.shape[1]
    at = jnp.swapaxes(aff[row0:row0 + bsz * n].reshape(bsz, n, ne), 1, 2)
    spec = pl.BlockSpec((None, ne, n), lambda b: (b, 0, 0))
    slot = pl.pallas_call(
        functools.partial(_route_kernel, n=n, ne=ne, cap=cap),
        grid=(bsz,), in_specs=[spec], out_specs=spec,
        out_shape=jax.ShapeDtypeStruct((bsz, ne, n), jnp.int32),
        compiler_params=_params("arbitrary"), name="expert_route",
    )(at)
    return at, slot


def _gather_kernel(slot_ref, afft_ref, h_ref, xg_ref, g_ref, *, cap):
    e = pl.program_id(1)
    want = lax.broadcasted_iota(jnp.int32, (cap, 1), 0)
    sel = slot_ref[pl.ds(e, 1), :] == want
    xg_ref[...] = jnp.dot(sel.astype(BF16), h_ref[...], preferred_element_type=F32).astype(xg_ref.dtype)
    g_ref[...] = jnp.sum(jnp.where(sel, afft_ref[pl.ds(e, 1), :], 0.0), axis=1, keepdims=True)


def _gather(slot, afft, h2, row_block0, n, bsz, cap):
    ne = slot.shape[1]
    d = h2.shape[1]
    row_spec = pl.BlockSpec((None, ne, n), lambda b, e: (b, 0, 0))
    return pl.pallas_call(
        functools.partial(_gather_kernel, cap=cap),
        grid=(bsz, ne),
        in_specs=[row_spec, row_spec, pl.BlockSpec((n, d), lambda b, e: (row_block0 + b, 0))],
        out_specs=[pl.BlockSpec((None, cap, d), lambda b, e: (e, b, 0)),
                   pl.BlockSpec((None, cap, 1), lambda b, e: (e, b, 0))],
        out_shape=[jax.ShapeDtypeStruct((ne, bsz * cap, d), BF16),
                   jax.ShapeDtypeStruct((ne, bsz * cap, 1), F32)],
        compiler_params=_params("arbitrary", "arbitrary"), name="expert_gather",
    )(slot, afft, h2)


def _ffn_kernel(*refs, ns, nf):
    x_refs = refs[0:2 * ns:2]
    g_refs = refs[1:2 * ns:2]
    wg_ref, wu_ref, wd_ref = refs[2 * ns:2 * ns + 3]
    o_refs = refs[2 * ns + 3:3 * ns + 3]
    wgb, wub, wdb = refs[3 * ns + 3:3 * ns + 6]
    acc_refs = refs[3 * ns + 6:]
    f = pl.program_id(1)
    wgb[...] = wg_ref[...].astype(BF16)
    wub[...] = wu_ref[...].astype(BF16)
    wdb[...] = wd_ref[...].astype(BF16)
    @pl.when(f == 0)
    def _():
        for acc in acc_refs:
            acc[...] = jnp.zeros(acc.shape, F32)

    for k in range(ns):
        rows = x_refs[k].shape[0]
        step = min(FFN_ROWS, rows)
        for r0 in range(0, rows, step):
            rs = slice(r0, r0 + step)
            x = x_refs[k][rs, :]
            hg = jnp.dot(x, wgb[...], preferred_element_type=F32)
            hu = jnp.dot(x, wub[...], preferred_element_type=F32)
            hid = (hg * _sigmoid(hg) * hu).astype(BF16)
            acc_refs[k][rs, :] += jnp.dot(hid, wdb[...], preferred_element_type=F32)

    @pl.when(f == nf - 1)
    def _():
        for k in range(ns):
            o_refs[k][...] = (acc_refs[k][...] * g_refs[k][...]).astype(o_refs[k].dtype)


def _expert_ffn(xgs, gates, w_gate, w_up, w_down, layer):
    ns = len(xgs)
    _, ne, d, ff = w_gate.shape
    tf = _lane_tile(ff, 256)
    nf = ff // tf
    in_specs, out_specs, out_shape, acc_scr, args = [], [], [], [], []
    for k in range(ns):
        rows = xgs[k].shape[1]
        in_specs.append(pl.BlockSpec((None, rows, d), lambda e, f: (e, 0, 0)))
        in_specs.append(pl.BlockSpec((None, rows, 1), lambda e, f: (e, 0, 0)))
        out_specs.append(pl.BlockSpec((None, rows, d), lambda e, f: (e, 0, 0)))
        out_shape.append(jax.ShapeDtypeStruct((ne, rows, d), BF16))
        acc_scr.append(pltpu.VMEM((rows, d), F32))
        args += [xgs[k], gates[k]]
    in_specs += [pl.BlockSpec((None, None, d, tf), lambda e, f: (layer, e, 0, f)),
                 pl.BlockSpec((None, None, d, tf), lambda e, f: (layer, e, 0, f)),
                 pl.BlockSpec((None, None, tf, d), lambda e, f: (layer, e, f, 0))]
    return pl.pallas_call(
        functools.partial(_ffn_kernel, ns=ns, nf=nf),
        grid=(ne, nf),
        in_specs=in_specs, out_specs=out_specs, out_shape=out_shape,
        scratch_shapes=[pltpu.VMEM((d, tf), BF16), pltpu.VMEM((d, tf), BF16), pltpu.VMEM((tf, d), BF16)] + acc_scr,
        compiler_params=_params("arbitrary", "arbitrary"), name="expert_ffn",
    )(*args, w_gate, w_up, w_down)


def _scatter_kernel(slot_ref, y_ref, x_ref, mod_ref, o_ref, *, ne, cap):
    sl = slot_ref[...]
    want = lax.broadcasted_iota(jnp.int32, (1, cap), 1)
    acc = None
    for e in range(ne):
        sel = (sl[:, e:e + 1] == want).astype(BF16)
        p = jnp.dot(sel, y_ref[e], preferred_element_type=F32)
        acc = p if acc is None else acc + p
    o_ref[...] = x_ref[...] + mod_ref[5:6, :] * acc


def _scatter(slot, y, x, mod, layer, mod_row0, row0, n, bsz, cap, tn):
    ne = slot.shape[1]
    d = x.shape[1]
    tm = min(512, n)
    slot_t = jnp.swapaxes(slot, 1, 2)
    xb0 = row0 // tm
    per_seq = n // tm
    return pl.pallas_call(
        functools.partial(_scatter_kernel, ne=ne, cap=cap),
        grid=(bsz, d // tn, per_seq),
        in_specs=[pl.BlockSpec((None, tm, ne), lambda b, j, i: (b, i, 0)),
                  pl.BlockSpec((ne, cap, tn), lambda b, j, i: (0, b, j)),
                  pl.BlockSpec((tm, tn), lambda b, j, i: (xb0 + b * per_seq + i, j)),
                  pl.BlockSpec((None, None, 6, tn), lambda b, j, i: (layer, mod_row0(b), 0, j))],
        out_specs=pl.BlockSpec((tm, tn), lambda b, j, i: (b * per_seq + i, j)),
        out_shape=jax.ShapeDtypeStruct((bsz * n, d), F32),
        compiler_params=_params("arbitrary", "arbitrary", "arbitrary"), name="expert_scatter",
    )(slot_t, y, x, mod)


def kernel(x, c, ctx, c_ctx, w_mod, b_mod, g_norm1, w_in, lb_param, g_hgrn, w_pool, pool_scale, rpb,
           w_branch, w_out, g_norm2, w_router, w_gate_e, w_up_e, w_down_e, g_final):
    bsz, n, d = x.shape
    lc = ctx.shape[1]
    depth = w_mod.shape[0]
    width = lb_param.shape[2]
    heads = width // HEAD_DIM
    ne = w_router.shape[2]
    n_lat = bsz * n
    n_ctx = bsz * lc
    total = n_lat + n_ctx
    assert bsz + 1 <= 8 and n % ROW_TILE == 0 and lc % ROW_TILE == 0 and n_lat % lc == 0
    assert w_pool.shape[-1] * 4 == width and w_branch.shape[2] == width
    tm = _row_tile(n, n_ctx)
    tn = min(512, d)

    lb_all = jnp.cumsum(jax.nn.softmax(lb_param.astype(F32), axis=1), axis=1)
    lb_all = lb_all - lb_all[:, :1]
    lbp_all = jnp.stack([jnp.log(lb_all), jnp.log1p(-lb_all), 1.0 - lb_all], axis=2)
    lbp_all = lbp_all.reshape(2, depth, 3, heads, LANES).transpose(0, 1, 3, 2, 4)

    c8 = jnp.concatenate([c, c_ctx[None], jnp.zeros((8 - bsz - 1, d), F32)], axis=0)
    mod = _modulation(c8, w_mod, b_mod).reshape(depth, 8, 6, d)

    cm3, pair_masks, upper = _hgrn_constants()
    hgrn_consts = (jnp.asarray(cm3, BF16), jnp.asarray(pair_masks, F32), jnp.asarray(upper, F32))
    na_layout = _na_block_layout(n // GRID_W)
    na_bias = _na_bias_table(rpb)
    lanes_e = -(-ne // LANES) * LANES
    w_router_p = jnp.pad(w_router, ((0, 0), (0, 0), (0, lanes_e - ne)))

    x_lat = x.reshape(n_lat, d)
    x_ctx = ctx.reshape(n_ctx, d)
    zero_state = jnp.zeros((bsz, heads, LANES, LANES), F32)
    cap = EC_CAPACITY * n // ne
    cap_c = EC_CAPACITY * lc // ne
    a_cols = 5 * width
    b_cols = w_in.shape[2] - a_cols
    tn_in = _lane_tile(math.gcd(a_cols, b_cols), 1024)
    gate_col0 = 4 * width
    rt_all = _Rows(tm, n, bsz, n_lat, total)
    rt_lat = _Rows(tm, n, bsz, n_lat, n_lat)
    rn_all = _Rows(ROW_TILE, n, bsz, n_lat, total)
    rn_lat = _Rows(ROW_TILE, n, bsz, n_lat, n_lat)

    for l in range(depth):
        last = l == depth - 1
        rt, rn = (rt_lat, rn_lat) if last else (rt_all, rn_all)
        gain_h = g_hgrn[l].reshape(heads, 1, LANES)

        h = _norm1(x_lat, x_ctx, g_norm1[l], mod, l, rn_all)
        ua = _matmul(h, w_in, l, 0, a_cols, F32, tm, tn_in)
        ub = _matmul(h, w_in, l, a_cols, b_cols, BF16, tm, tn_in)

        cy_a, s_f, s_b = _hgrn(ua, n_lat // lc, lc, bsz, heads, lbp_all[:, l], gain_h, zero_state, zero_state,
                               hgrn_consts)
        y_a, _, _ = _hgrn(ua, 0, n, bsz, heads, lbp_all[:, l], gain_h, s_f, s_b, hgrn_consts)
        y_b = _pool(ub, 0, n, bsz, w_pool, pool_scale, l)
        y_c, cy_c = _attention(ub, width, n, lc, bsz, heads, na_bias, l, na_layout)
        ys_lat = (y_a, y_b, y_c)
        ys_ctx = ys_lat if last else (cy_a, _pool(ub, n_lat // lc, lc, bsz, w_pool, pool_scale, l), cy_c)

        merged = _merge(ys_lat, ys_ctx, ub, gate_col0, w_branch, l, rt, tn)
        x_mid = _out_proj(merged, w_out, l, x_lat, x_ctx, mod, rt, tn)

        h2, aff = _norm2(x_mid, g_norm2[l], mod, l, rn, w_router_p, ne)
        afft, slot = _route(aff, 0, n, bsz, cap)
        xg, gate = _gather(slot, afft, h2, 0, n, bsz, cap)
        xgs, gates = [xg], [gate]
        if not last:
            afft_c, slot_c = _route(aff, n_lat, lc, bsz, cap_c)
            xg_c, gate_c = _gather(slot_c, afft_c, h2, n_lat // lc, lc, bsz, cap_c)
            xgs, gates = xgs + [xg_c], gates + [gate_c]
        ys = _expert_ffn(xgs, gates, w_gate_e, w_up_e, w_down_e, l)
        x_lat = _scatter(slot, ys[0], x_mid, mod, l, lambda b: b, 0, n, bsz, cap, tn)
        if not last:
            x_ctx = _scatter(slot_c, ys[1], x_mid, mod, l, lambda b: bsz, n_lat, lc, bsz, cap_c, tn)

    return _final_norm(x_lat, g_final).reshape(bsz, n, d)
```

```python
import functools
import math

import numpy as np
import jax
import jax.numpy as jnp
from jax import lax
from jax.experimental import pallas as pl
from jax.experimental.pallas import tpu as pltpu

F32 = jnp.float32
BF16 = jnp.bfloat16
EPS = 1e-6
LANES = 128
HEAD_DIM = 128
CHUNK = 64
HGRN_GROUP = 4
GRID_W = 64
POOL_WINDOWS = (2, 4, 8, 16)
POOL_PAD = 16
NA_ROWS_MAX = 8
NA_COLS = 16
NA_QROWS = 4
NA_KROWS = 12
ROPE_THETA = 10000.0
EC_CAPACITY = 2
FFN_ROWS = 512
NEG = -1e30
ROW_TILE = 256
LOG2E = 1.4426950408889634
VMEM_LIMIT = 56 * 2 ** 20


def _params(*sem):
    return pltpu.CompilerParams(dimension_semantics=sem, vmem_limit_bytes=VMEM_LIMIT)


def _sigmoid(x):
    return 1.0 / (1.0 + jnp.exp(-x))


def _neg_abs(x):
    bits = lax.bitcast_convert_type(x, jnp.uint32) | jnp.uint32(0x80000000)
    return lax.bitcast_convert_type(bits, F32)


def _lane_tile(n, limit):
    t = min(limit, n) // LANES * LANES
    while n % t:
        t -= LANES
    return t


def _row_tile(n, ctx_rows):
    tm = 1024
    while n % tm or ctx_rows % tm:
        tm //= 2
    return tm


class _Rows:
    def __init__(self, tm, n, bsz, n_lat, rows):
        self.tm, self.n, self.bsz = tm, n, bsz
        self.lat_tiles = n_lat // tm
        self.tiles = rows // tm
        self.ctx_tiles = max(self.tiles - self.lat_tiles, 1)

    def lat(self, i):
        return jnp.minimum(i, self.lat_tiles - 1)

    def ctx(self, i):
        return jnp.clip(i - self.lat_tiles, 0, self.ctx_tiles - 1)

    def mod_row(self, i):
        return jnp.minimum(i * self.tm // self.n, self.bsz)


def _pick(is_lat, lat_ref, ctx_ref):
    return jnp.where(is_lat, lat_ref[...], ctx_ref[...])


def _mod_kernel(c_ref, w_ref, b_ref, o_ref):
    c = c_ref[...]
    sc = (c * _sigmoid(c)).astype(BF16)
    o_ref[...] = jnp.dot(sc, w_ref[...].astype(BF16), preferred_element_type=F32) + b_ref[...]


def _modulation(c8, w_mod, b_mod):
    depth, d, n6 = w_mod.shape
    tn = _lane_tile(n6, 1024)
    return pl.pallas_call(
        _mod_kernel,
        grid=(depth, n6 // tn),
        in_specs=[pl.BlockSpec((8, d), lambda l, j: (0, 0)),
                  pl.BlockSpec((None, d, tn), lambda l, j: (l, 0, j)),
                  pl.BlockSpec((None, 1, tn), lambda l, j: (l, 0, j))],
        out_specs=pl.BlockSpec((None, 8, tn), lambda l, j: (l, 0, j)),
        out_shape=jax.ShapeDtypeStruct((depth, 8, n6), F32),
        compiler_params=_params("arbitrary", "arbitrary"),
        name="modulation",
    )(c8, w_mod, b_mod.reshape(depth, 1, n6))


def _rmsnorm(x, gain):
    return x * lax.rsqrt(jnp.mean(x * x, axis=-1, keepdims=True) + EPS) * gain


def _modulate(y, mod_ref, shift_idx):
    return y * (1.0 + mod_ref[shift_idx + 1:shift_idx + 2, :]) + mod_ref[shift_idx:shift_idx + 1, :]


def _norm1_kernel(xl_ref, xc_ref, g_ref, mod_ref, o_ref, *, lat_tiles):
    x = _pick(pl.program_id(0) < lat_tiles, xl_ref, xc_ref)
    o_ref[...] = _modulate(_rmsnorm(x, g_ref[...]), mod_ref, 0).astype(o_ref.dtype)


def _norm2_kernel(x_ref, g_ref, mod_ref, wr_ref, o_ref, aff_ref):
    h = _modulate(_rmsnorm(x_ref[...], g_ref[...]), mod_ref, 3)
    o_ref[...] = h.astype(o_ref.dtype)
    logits = jnp.dot(h, wr_ref[...], precision=lax.Precision.HIGHEST, preferred_element_type=F32)
    logits = logits[:, :aff_ref.shape[1]]
    e = jnp.exp(logits - jnp.max(logits, axis=-1, keepdims=True))
    aff_ref[...] = e / jnp.sum(e, axis=-1, keepdims=True)


def _final_norm_kernel(x_ref, g_ref, o_ref):
    o_ref[...] = _rmsnorm(x_ref[...], g_ref[...])


def _norm1(x_lat, x_ctx, gain, mod, layer, rt):
    d = x_lat.shape[1]
    return pl.pallas_call(
        functools.partial(_norm1_kernel, lat_tiles=rt.lat_tiles),
        grid=(rt.tiles,),
        in_specs=[pl.BlockSpec((rt.tm, d), lambda i: (rt.lat(i), 0)),
                  pl.BlockSpec((rt.tm, d), lambda i: (rt.ctx(i), 0)),
                  pl.BlockSpec((1, d), lambda i: (0, 0)),
                  pl.BlockSpec((None, None, 6, d), lambda i: (layer, rt.mod_row(i), 0, 0))],
        out_specs=pl.BlockSpec((rt.tm, d), lambda i: (i, 0)),
        out_shape=jax.ShapeDtypeStruct((rt.tiles * rt.tm, d), BF16),
        compiler_params=_params("arbitrary"), name="norm1",
    )(x_lat, x_ctx, gain.reshape(1, d), mod)


def _norm2(x, gain, mod, layer, rt, w_router_padded, ne):
    d = x.shape[1]
    rows = rt.tiles * rt.tm
    return pl.pallas_call(
        _norm2_kernel,
        grid=(rt.tiles,),
        in_specs=[pl.BlockSpec((rt.tm, d), lambda i: (i, 0)),
                  pl.BlockSpec((1, d), lambda i: (0, 0)),
                  pl.BlockSpec((None, None, 6, d), lambda i: (layer, rt.mod_row(i), 0, 0)),
                  pl.BlockSpec((None, d, w_router_padded.shape[2]), lambda i: (layer, 0, 0))],
        out_specs=[pl.BlockSpec((rt.tm, d), lambda i: (i, 0)), pl.BlockSpec((rt.tm, ne), lambda i: (i, 0))],
        out_shape=[jax.ShapeDtypeStruct((rows, d), BF16), jax.ShapeDtypeStruct((rows, ne), F32)],
        compiler_params=_params("arbitrary"), name="norm2_router",
    )(x, gain.reshape(1, d), mod, w_router_padded)


def _final_norm(x, gain):
    rows, d = x.shape
    tm = ROW_TILE
    return pl.pallas_call(
        _final_norm_kernel,
        grid=(rows // tm,),
        in_specs=[pl.BlockSpec((tm, d), lambda i: (i, 0)), pl.BlockSpec((1, d), lambda i: (0, 0))],
        out_specs=pl.BlockSpec((tm, d), lambda i: (i, 0)),
        out_shape=jax.ShapeDtypeStruct((rows, d), F32),
        compiler_params=_params("arbitrary"), name="final_norm",
    )(x, gain.reshape(1, d))


def _mm_kernel(a_ref, w_ref, o_ref, wbf_ref):
    @pl.when(pl.program_id(1) == 0)
    def _():
        wbf_ref[...] = w_ref[...].astype(BF16)
    o_ref[...] = jnp.dot(a_ref[...], wbf_ref[...], preferred_element_type=F32).astype(o_ref.dtype)


def _matmul(a, w_all, layer, col0, ncols, out_dtype, tm, tn):
    m, k = a.shape
    off = col0 // tn
    return pl.pallas_call(
        _mm_kernel,
        grid=(ncols // tn, m // tm),
        in_specs=[pl.BlockSpec((tm, k), lambda j, i: (i, 0)),
                  pl.BlockSpec((None, k, tn), lambda j, i: (layer, 0, j + off))],
        out_specs=pl.BlockSpec((tm, tn), lambda j, i: (i, j)),
        out_shape=jax.ShapeDtypeStruct((m, ncols), out_dtype),
        scratch_shapes=[pltpu.VMEM((k, tn), BF16)],
        compiler_params=_params("arbitrary", "arbitrary"), name="matmul",
    )(a, w_all)


def _out_proj_kernel(a_ref, w_ref, xl_ref, xc_ref, mod_ref, o_ref, wbf_ref, *, lat_tiles):
    @pl.when(pl.program_id(1) == 0)
    def _():
        wbf_ref[...] = w_ref[...].astype(BF16)
    acc = jnp.dot(a_ref[...], wbf_ref[...], preferred_element_type=F32)
    x = _pick(pl.program_id(1) < lat_tiles, xl_ref, xc_ref)
    o_ref[...] = x + mod_ref[2:3, :] * acc


def _out_proj(a, w_all, layer, x_lat, x_ctx, mod, rt, tn):
    k = a.shape[1]
    d = w_all.shape[2]
    tm = rt.tm
    return pl.pallas_call(
        functools.partial(_out_proj_kernel, lat_tiles=rt.lat_tiles),
        grid=(d // tn, rt.tiles),
        in_specs=[pl.BlockSpec((tm, k), lambda j, i: (i, 0)),
                  pl.BlockSpec((None, k, tn), lambda j, i: (layer, 0, j)),
                  pl.BlockSpec((tm, tn), lambda j, i: (rt.lat(i), j)),
                  pl.BlockSpec((tm, tn), lambda j, i: (rt.ctx(i), j)),
                  pl.BlockSpec((None, None, 6, tn), lambda j, i: (layer, rt.mod_row(i), 0, j))],
        out_specs=pl.BlockSpec((tm, tn), lambda j, i: (i, j)),
        out_shape=jax.ShapeDtypeStruct((rt.tiles * tm, d), F32),
        scratch_shapes=[pltpu.VMEM((k, tn), BF16)],
        compiler_params=_params("arbitrary", "arbitrary"), name="out_proj",
    )(a, w_all, x_lat, x_ctx, mod)


def _merge_kernel(ya_ref, yb_ref, yc_ref, cya_ref, cyb_ref, cyc_ref, g0_ref, g1_ref, g2_ref, w_ref, o_ref, wbf_ref,
                  *, lat_tiles):
    @pl.when(pl.program_id(1) == 0)
    def _():
        wbf_ref[...] = w_ref[...].astype(BF16)
    is_lat = pl.program_id(1) < lat_tiles
    acc = None
    for j, (y_ref, cy_ref, g_ref) in enumerate(((ya_ref, cya_ref, g0_ref), (yb_ref, cyb_ref, g1_ref),
                                                (yc_ref, cyc_ref, g2_ref))):
        p = jnp.dot(_pick(is_lat, y_ref, cy_ref), wbf_ref[j], preferred_element_type=F32)
        t = _sigmoid(g_ref[...].astype(F32)) * p
        acc = t if acc is None else acc + t
    o_ref[...] = acc.astype(o_ref.dtype)


def _merge(ys_lat, ys_ctx, ub, gate_col0, w_branch, layer, rt, tn):
    bw = ys_lat[0].shape[1]
    d = w_branch.shape[3]
    tm = rt.tm
    lat_spec = pl.BlockSpec((tm, bw), lambda j, i: (rt.lat(i), 0))
    ctx_spec = pl.BlockSpec((tm, bw), lambda j, i: (rt.ctx(i), 0))

    def gate_spec(k):
        off = (gate_col0 + k * d) // tn
        return pl.BlockSpec((tm, tn), lambda j, i: (i, off + j))

    return pl.pallas_call(
        functools.partial(_merge_kernel, lat_tiles=rt.lat_tiles),
        grid=(d // tn, rt.tiles),
        in_specs=[lat_spec] * 3 + [ctx_spec] * 3 + [gate_spec(0), gate_spec(1), gate_spec(2),
                  pl.BlockSpec((None, 3, bw, tn), lambda j, i: (layer, 0, 0, j))],
        out_specs=pl.BlockSpec((tm, tn), lambda j, i: (i, j)),
        out_shape=jax.ShapeDtypeStruct((rt.tiles * tm, d), BF16),
        scratch_shapes=[pltpu.VMEM((3, bw, tn), BF16)],
        compiler_params=_params("arbitrary", "arbitrary"), name="merge",
    )(*ys_lat, *ys_ctx, ub, ub, ub, w_branch)


_HGRN_BLOCKS = (32, 16, 8, 4, 2, 1)


def _hgrn_constants():
    c = CHUNK
    t = np.arange(c)
    tri_f = (t[:, None] >= t[None, :]).astype(np.float32)
    tri_b = np.ascontiguousarray(tri_f[::-1, ::-1])
    sel_f, sel_b, masks, upper = [tri_f], [tri_b], [], []
    for m in _HGRN_BLOCKS:
        blk = (t // (2 * m)) * (2 * m)
        up = (t % (2 * m)) >= m
        sel_f.append(tri_f - tri_f[blk + m - 1])
        sel_b.append(tri_b - tri_b[blk + m])
        same = (t[:, None] // (2 * m)) == (t[None, :] // (2 * m))
        masks.append(same & (up[:, None] != up[None, :]))
        upper.append(np.broadcast_to(up[:, None], (c, LANES)))
    masks.append(np.eye(c, dtype=bool))
    sel = np.stack([np.concatenate(sel_f, axis=0), np.concatenate(sel_b, axis=0)])
    return (np.concatenate([sel, sel, sel], axis=2), np.stack(masks).astype(np.float32),
            np.stack(upper).astype(np.float32))


def _hgrn_log2_gate(z, lbp):
    zl = z * LOG2E
    log_sig = jnp.minimum(zl, 0.0) - jnp.log2(1.0 + jnp.exp2(_neg_abs(zl)))
    a = lbp[0:1, :]
    cc = lbp[1:2, :] + log_sig
    return jnp.maximum(a, cc) + jnp.log2(1.0 + jnp.exp2(_neg_abs(a - cc)))


def _split3(x):
    g1 = x.astype(BF16)
    r1 = x - g1.astype(F32)
    g2 = r1.astype(BF16)
    g3 = (r1 - g2.astype(F32)).astype(BF16)
    return jnp.concatenate([g1, g2, g3], axis=0)


_NT = (((1,), (1,)), ((), ()))
_TN = (((0,), (0,)), ((), ()))


def _hgrn_kernel(q_ref, ff_ref, fb_ref, i_ref, g_ref, lbf_ref, lbb_ref, gain_ref, cm_ref, mask_ref, up_ref,
                 sf0_ref, sb0_ref, y_ref, sf_ref, sb_ref, o_scr, qt_scr, upd_scr, dec_scr, st_scr, *, nc):
    z_refs = (ff_ref, fb_ref)
    lb_refs = (lbf_ref, lbb_ref)
    nl = len(_HGRN_BLOCKS)
    last_row = (CHUNK - 1, 0)

    def local_pass(it, carry):
        rows = [pl.ds(pl.multiple_of((it * HGRN_GROUP + s) * CHUNK, CHUNK), CHUNK) for s in range(HGRN_GROUP)]
        logf = [[_hgrn_log2_gate(z_refs[d][r, :], lb_refs[d][...]) for r in rows] for d in range(2)]
        sums = [jnp.dot(cm_ref[d], jnp.concatenate([_split3(lf) for lf in logf[d]], axis=1),
                        preferred_element_type=F32) for d in range(2)]
        up = up_ref[...] > 0.5
        for s, r in enumerate(rows):
            lanes = slice(s * LANES, (s + 1) * LANES)
            qr = q_ref[r, :]
            q = qr * _sigmoid(qr)
            vb = i_ref[r, :].astype(BF16)
            b = [sums[d][0:CHUNK, lanes] for d in range(2)]
            lvl = [jnp.exp2(_neg_abs(sums[d][CHUNK:, lanes].reshape(nl, CHUNK, LANES))) for d in range(2)]
            k = [1.0 - jnp.exp2(logf[d][s]) for d in range(2)]
            q_dec = jnp.where(up, lvl[0], lvl[1])
            k_dec = jnp.where(up, k[1][None] * lvl[1], k[0][None] * lvl[0])
            qs = jnp.concatenate([q[None] * q_dec, q[None]], axis=0).astype(BF16)
            ks = jnp.concatenate([k_dec, (k[0] + k[1])[None]], axis=0).astype(BF16)
            sc = jnp.einsum("ltk,lsk->lts", qs, ks, preferred_element_type=F32)
            amat = jnp.sum(sc * mask_ref[...], axis=0)
            o_scr[r, :] = jnp.dot(amat.astype(BF16), vb, preferred_element_type=F32)
            tot = [b[d][last_row[d]:last_row[d] + 1, :] for d in range(2)]
            kd = jnp.concatenate([(k[d] * jnp.exp2(tot[d] - b[d])).astype(BF16) for d in range(2)], axis=1)
            upd = lax.dot_general(vb, kd, _TN, preferred_element_type=F32)
            j = it * HGRN_GROUP + s
            for d in range(2):
                qt_scr[d, r, :] = (q * jnp.exp2(b[d])).astype(BF16)
                upd_scr[d, j] = upd[:, d * LANES:(d + 1) * LANES]
                dec_scr[d, pl.ds(j, 1), :] = jnp.exp2(tot[d])
        return carry

    lax.fori_loop(0, nc // HGRN_GROUP, local_pass, 0)

    st_scr[0] = sf0_ref[...]
    st_scr[1] = sb0_ref[...]

    def state_pass(it, carry):
        for d in range(2):
            j = it if d == 0 else nc - 1 - it
            rows = pl.ds(pl.multiple_of(j * CHUNK, CHUNK), CHUNK)
            st = st_scr[d]
            o_scr[rows, :] += lax.dot_general(qt_scr[d, rows, :], st.astype(BF16), _NT, preferred_element_type=F32)
            st_scr[d] = st * dec_scr[d, pl.ds(j, 1), :] + upd_scr[d, j]
        return carry

    lax.fori_loop(0, nc, state_pass, 0, unroll=min(8, nc))
    sf_ref[...] = st_scr[0]
    sb_ref[...] = st_scr[1]
    o = o_scr[...]
    o = o * lax.rsqrt(jnp.mean(o * o, axis=-1, keepdims=True) + EPS) * gain_ref[...]
    g = g_ref[...]
    y_ref[...] = (o * (g * _sigmoid(g))).astype(y_ref.dtype)


def _hgrn(ua, row_block0, n, bsz, heads, lbp, gain, sf0, sb0, consts):
    nc = n // CHUNK
    assert nc % HGRN_GROUP == 0 and nc % 2 == 0

    def col(k):
        return pl.BlockSpec((n, LANES), lambda b, h: (row_block0 + b, k * heads + h))

    lb_spec = lambda d: pl.BlockSpec((None, None, 2, LANES), lambda b, h: (d, h, 0, 0))
    st_spec = pl.BlockSpec((None, None, LANES, LANES), lambda b, h: (b, h, 0, 0))
    st_shape = jax.ShapeDtypeStruct((bsz, heads, LANES, LANES), F32)
    const_specs = [pl.BlockSpec(a.shape, lambda b, h, nd=a.ndim: (0,) * nd) for a in consts]
    return pl.pallas_call(
        functools.partial(_hgrn_kernel, nc=nc),
        grid=(bsz, heads),
        in_specs=[col(0), col(1), col(2), col(3), col(4), lb_spec(0), lb_spec(1),
                  pl.BlockSpec((None, 1, LANES), lambda b, h: (h, 0, 0))] + const_specs + [st_spec, st_spec],
        out_specs=[pl.BlockSpec((n, LANES), lambda b, h: (b, h)), st_spec, st_spec],
        out_shape=[jax.ShapeDtypeStruct((bsz * n, heads * LANES), BF16), st_shape, st_shape],
        scratch_shapes=[pltpu.VMEM((n, LANES), F32), pltpu.VMEM((2, n, LANES), BF16),
                        pltpu.VMEM((2, nc, LANES, LANES), F32), pltpu.VMEM((2, nc, LANES), F32),
                        pltpu.VMEM((2, LANES, LANES), F32)],
        compiler_params=_params("arbitrary", "arbitrary"), name="hgrn2",
    )(ua, ua, ua, ua, ua, lbp, lbp, gain, *consts, sf0, sb0)


def _pool_kernel(u_ref, w_ref, s_ref, y_ref, pad_ref, *, n, group):
    pos = lax.broadcasted_iota(jnp.int32, (n, 1), 0)
    zeros = jnp.zeros((POOL_PAD, pad_ref.shape[1]), F32)
    pad_ref[0:POOL_PAD, :] = zeros
    pad_ref[POOL_PAD + n:2 * POOL_PAD + n, :] = zeros
    pad_ref[POOL_PAD:POOL_PAD + n, :] = u_ref[...].astype(F32)
    for gi, w in enumerate(POOL_WINDOWS):
        cols = slice(gi * group, (gi + 1) * group)
        acc = None
        for dlt in range(-(w // 2), w // 2):
            t = pad_ref[POOL_PAD + dlt:POOL_PAD + dlt + n, cols]
            acc = t if acc is None else acc + t
        lo = jnp.maximum(pos - w // 2, 0)
        hi = jnp.minimum(pos + w // 2 - 1, n - 1)
        cnt = (hi - lo + 1).astype(F32)
        dd = acc / cnt - pad_ref[POOL_PAD:POOL_PAD + n, cols]
        y = jnp.dot(dd.astype(BF16), w_ref[gi].astype(BF16), preferred_element_type=F32)
        y_ref[:, cols] = (y * s_ref[:, cols]).astype(y_ref.dtype)


def _pool(ub, row_block0, n, bsz, w_pool, scale, layer):
    group = w_pool.shape[-1]
    width = 4 * group
    return pl.pallas_call(
        functools.partial(_pool_kernel, n=n, group=group),
        grid=(bsz,),
        in_specs=[pl.BlockSpec((n, width), lambda b: (row_block0 + b, 0)),
                  pl.BlockSpec((None, 4, group, group), lambda b: (layer, 0, 0, 0)),
                  pl.BlockSpec((None, 1, width), lambda b: (layer, 0, 0))],
        out_specs=pl.BlockSpec((n, width), lambda b: (b, 0)),
        out_shape=jax.ShapeDtypeStruct((bsz * n, width), BF16),
        scratch_shapes=[pltpu.VMEM((n + 2 * POOL_PAD, width), F32)],
        compiler_params=_params("arbitrary"), name="pool",
    )(ub, w_pool, scale.reshape(scale.shape[0], 1, width))


def _na_tables(n):
    pos = np.arange(n)
    half = HEAD_DIM // 2
    inv_freq = ROPE_THETA ** (-np.arange(0, half, 2, dtype=np.float64) / half)
    lane = np.arange(HEAD_DIM)
    p = np.where(lane[None, :] < half, (pos // GRID_W)[:, None], (pos % GRID_W)[:, None]).astype(np.float64)
    ang = p * inv_freq[lane % (half // 2)][None, :]
    sign = np.where((lane % half) < half // 2, -1.0, 1.0)[None, :]
    return np.cos(ang).astype(np.float32), (np.sin(ang) * sign).astype(np.float32)


def _na_block_layout(rows):
    kr = NA_ROWS_MAX
    nblk = rows // NA_QROWS
    starts, patterns, types = [], [], []
    for j in range(nblk):
        u = int(np.clip(NA_QROWS * j - kr // 2, 0, rows - NA_KROWS))
        r = NA_QROWS * j + np.arange(NA_QROWS)
        start_r = np.clip(r - kr // 2, 0, rows - kr)
        kabs = u + np.arange(NA_KROWS)
        valid = (kabs[None, :] >= start_r[:, None]) & (kabs[None, :] < start_r[:, None] + kr)
        assert valid.sum(axis=1).min() == kr
        dr = np.clip(kabs[None, :] - r[:, None] + NA_ROWS_MAX - 1, 0, 2 * NA_ROWS_MAX - 2)
        key = (valid.tobytes(), dr.tobytes())
        keys = [p[0] for p in patterns]
        if key not in keys:
            patterns.append((key, valid, dr))
        types.append([p[0] for p in patterns].index(key))
        starts.append(u)
    return starts, types, [(p[1], p[2]) for p in patterns]


NA_DR = 2 * NA_ROWS_MAX - 1


def _na_bias_table(rpb):
    qcol = np.arange(GRID_W)
    col_start = np.clip(qcol - NA_COLS // 2, 0, GRID_W - NA_COLS)
    kcol = np.arange(GRID_W)
    col_mask = (kcol[None, :] >= col_start[:, None]) & (kcol[None, :] < col_start[:, None] + NA_COLS)
    dc = np.clip(kcol[None, :] - qcol[:, None] + NA_COLS - 1, 0, 2 * NA_COLS - 2)
    col_hot = (dc[None] == np.arange(2 * NA_COLS - 1)[:, None, None]).astype(np.float32)
    by_col = jnp.einsum("lhab,bqk->lhaqk", rpb.astype(F32), col_hot, precision=lax.Precision.HIGHEST)
    by_col = jnp.where(col_mask, by_col, NEG)
    masked = jnp.full(by_col.shape[:2] + (1, GRID_W, GRID_W), NEG, F32)
    table = jnp.concatenate([by_col, masked], axis=2)
    return jnp.concatenate([table, table], axis=-1)


def _na_pieces(patterns):
    out = []
    for valid, dr in patterns:
        idx = np.where(valid, dr, NA_DR)
        out.append(tuple(tuple((int(idx[r, 2 * p]), int(idx[r, 2 * p + 1])) for p in range(NA_KROWS // 2))
                         for r in range(NA_QROWS)))
    return tuple(out)


def _softmax_pv(s_list, v_list):
    m = None
    for s in s_list:
        mm = jnp.max(s, axis=-1, keepdims=True)
        m = mm if m is None else jnp.maximum(m, mm)
    num, den = None, None
    for s, v in zip(s_list, v_list):
        p = jnp.exp(s - m)
        ssum = jnp.sum(p, axis=-1, keepdims=True)
        o = jnp.dot(p.astype(BF16), v, preferred_element_type=F32)
        num = o if num is None else num + o
        den = ssum if den is None else den + ssum
    return num / den


def _na_kernel(q_ref, k_ref, v_ref, cq_ref, ck_ref, cv_ref, tab_ref, cos_ref, sin_ref, y_ref, cy_ref,
               qs_ref, ks_ref, bias_ref, *, starts, types, pieces):
    lane = lax.broadcasted_iota(jnp.int32, (1, HEAD_DIM), 1)
    first = (lane % (HEAD_DIM // 2)) < HEAD_DIM // 4
    scale = HEAD_DIM ** -0.5

    @pl.when(pl.program_id(1) == 0)
    def _():
        even = lane < GRID_W
        for tp, by_row in enumerate(pieces):
            for r, by_pair in enumerate(by_row):
                for p, (ie, io) in enumerate(by_pair):
                    bias_ref[tp, r * GRID_W:(r + 1) * GRID_W, 2 * p * GRID_W:2 * (p + 1) * GRID_W] = (
                        jnp.where(even, tab_ref[ie], tab_ref[io]))

    def rope(t):
        partner = jnp.where(first, pltpu.roll(t, HEAD_DIM - HEAD_DIM // 4, axis=1),
                            pltpu.roll(t, HEAD_DIM // 4, axis=1))
        return t * cos_ref[...] + partner * sin_ref[...]

    qs_ref[...] = (rope(q_ref[...].astype(F32)) * scale).astype(BF16)
    ks_ref[...] = rope(k_ref[...].astype(F32)).astype(BF16)
    ck = ck_ref[...]
    cv = cv_ref[...]
    qrows = NA_QROWS * GRID_W
    krows = NA_KROWS * GRID_W
    for j, (u, tp) in enumerate(zip(starts, types)):
        qb = qs_ref[j * qrows:(j + 1) * qrows, :]
        kb = ks_ref[u * GRID_W:u * GRID_W + krows, :]
        vb = v_ref[u * GRID_W:u * GRID_W + krows, :]
        s_loc = lax.dot_general(qb, kb, _NT, preferred_element_type=F32) + bias_ref[tp]
        s_ctx = lax.dot_general(qb, ck, _NT, preferred_element_type=F32)
        y_ref[j * qrows:(j + 1) * qrows, :] = _softmax_pv([s_loc, s_ctx], [vb, cv]).astype(y_ref.dtype)
    s = lax.dot_general(cq_ref[...], ck, _NT, preferred_element_type=F32) * scale
    cy_ref[...] = _softmax_pv([s], [cv]).astype(cy_ref.dtype)


def _attention(ub, col0, n, lc, bsz, heads, table, layer, layout):
    starts, types, patterns = layout
    cos, sin = _na_tables(n)
    cb = col0 // HEAD_DIM
    ctx0 = bsz * n // lc

    def lat(k):
        return pl.BlockSpec((n, HEAD_DIM), lambda h, b: (b, cb + k * heads + h))

    def ctx(k):
        return pl.BlockSpec((lc, HEAD_DIM), lambda h, b: (ctx0 + b, cb + k * heads + h))

    tab = pl.BlockSpec((n, HEAD_DIM), lambda h, b: (0, 0))
    return pl.pallas_call(
        functools.partial(_na_kernel, starts=tuple(starts), types=tuple(types), pieces=_na_pieces(patterns)),
        grid=(heads, bsz),
        in_specs=[lat(0), lat(1), lat(2), ctx(0), ctx(1), ctx(2),
                  pl.BlockSpec((None, None) + table.shape[2:], lambda h, b: (layer, h, 0, 0, 0)), tab, tab],
        out_specs=[pl.BlockSpec((n, HEAD_DIM), lambda h, b: (b, h)),
                   pl.BlockSpec((lc, HEAD_DIM), lambda h, b: (b, h))],
        out_shape=[jax.ShapeDtypeStruct((bsz * n, heads * HEAD_DIM), BF16),
                   jax.ShapeDtypeStruct((bsz * lc, heads * HEAD_DIM), BF16)],
        scratch_shapes=[pltpu.VMEM((n, HEAD_DIM), BF16), pltpu.VMEM((n, HEAD_DIM), BF16),
                        pltpu.VMEM((len(patterns), NA_QROWS * GRID_W, NA_KROWS * GRID_W), F32)],
        compiler_params=_params("arbitrary", "arbitrary"), name="attention",
    )(ub, ub, ub, ub, ub, ub, table, jnp.asarray(cos), jnp.asarray(sin))


def _route_kernel(afft_ref, slot_ref, *, n, ne, cap):
    bits = lax.bitcast_convert_type(afft_ref[...], jnp.int32)

    def bisect(i, thr):
        cand = thr | jnp.left_shift(jnp.int32(1), 30 - i)
        cnt = jnp.sum((bits >= cand).astype(F32), axis=1, keepdims=True)
        return jnp.where(cnt >= cap, cand, thr)

    thr = lax.fori_loop(0, 31, bisect, jnp.zeros((ne, 1), jnp.int32))
    above = bits > thr
    tied = bits == thr
    need = cap - jnp.sum(above.astype(F32), axis=1, keepdims=True)

    row = lax.broadcasted_iota(jnp.int32, (LANES, LANES), 0)
    colm = lax.broadcasted_iota(jnp.int32, (LANES, LANES), 1)
    before = (row < colm).astype(BF16)
    ones = jnp.ones((LANES, LANES), BF16)

    def prefix(x):
        xb = x.astype(BF16)
        outs, carry = [], jnp.zeros((ne, LANES), F32)
        for blk in range(n // LANES):
            xs = xb[:, blk * LANES:(blk + 1) * LANES]
            outs.append(jnp.dot(xs, before, preferred_element_type=F32) + carry)
            carry = carry + jnp.dot(xs, ones, preferred_element_type=F32)
        return jnp.concatenate(outs, axis=1)

    chosen = above | (tied & (prefix(tied.astype(F32)) < need))
    slot = prefix(chosen.astype(F32)).astype(jnp.int32)
    slot_ref[...] = jnp.where(chosen, slot, n)


def _route(aff, row0, n, bsz, cap):
    ne = aff.shape[1]
    at = jnp.swapaxes(aff[row0:row0 + bsz * n].reshape(bsz, n, ne), 1, 2)
    spec = pl.BlockSpec((None, ne, n), lambda b: (b, 0, 0))
    slot = pl.pallas_call(
        functools.partial(_route_kernel, n=n, ne=ne, cap=cap),
        grid=(bsz,), in_specs=[spec], out_specs=spec,
        out_shape=jax.ShapeDtypeStruct((bsz, ne, n), jnp.int32),
        compiler_params=_params("arbitrary"), name="expert_route",
    )(at)
    return at, slot


def _gather_kernel(slot_ref, afft_ref, h_ref, xg_ref, g_ref, *, cap):
    e = pl.program_id(1)
    want = lax.broadcasted_iota(jnp.int32, (cap, 1), 0)
    sel = slot_ref[pl.ds(e, 1), :] == want
    xg_ref[...] = jnp.dot(sel.astype(BF16), h_ref[...], preferred_element_type=F32).astype(xg_ref.dtype)
    g_ref[...] = jnp.sum(jnp.where(sel, afft_ref[pl.ds(e, 1), :], 0.0), axis=1, keepdims=True)


def _gather(slot, afft, h2, row_block0, n, bsz, cap):
    ne = slot.shape[1]
    d = h2.shape[1]
    row_spec = pl.BlockSpec((None, ne, n), lambda b, e: (b, 0, 0))
    return pl.pallas_call(
        functools.partial(_gather_kernel, cap=cap),
        grid=(bsz, ne),
        in_specs=[row_spec, row_spec, pl.BlockSpec((n, d), lambda b, e: (row_block0 + b, 0))],
        out_specs=[pl.BlockSpec((None, cap, d), lambda b, e: (e, b, 0)),
                   pl.BlockSpec((None, cap, 1), lambda b, e: (e, b, 0))],
        out_shape=[jax.ShapeDtypeStruct((ne, bsz * cap, d), BF16),
                   jax.ShapeDtypeStruct((ne, bsz * cap, 1), F32)],
        compiler_params=_params("arbitrary", "arbitrary"), name="expert_gather",
    )(slot, afft, h2)


def _ffn_kernel(*refs, ns, nf):
    x_refs = refs[0:2 * ns:2]
    g_refs = refs[1:2 * ns:2]
    wg_ref, wu_ref, wd_ref = refs[2 * ns:2 * ns + 3]
    o_refs = refs[2 * ns + 3:3 * ns + 3]
    wgb, wub, wdb = refs[3 * ns + 3:3 * ns + 6]
    acc_refs = refs[3 * ns + 6:]
    f = pl.program_id(1)
    wgb[...] = wg_ref[...].astype(BF16)
    wub[...] = wu_ref[...].astype(BF16)
    wdb[...] = wd_ref[...].astype(BF16)
    @pl.when(f == 0)
    def _():
        for acc in acc_refs:
            acc[...] = jnp.zeros(acc.shape, F32)

    for k in range(ns):
        rows = x_refs[k].shape[0]
        step = min(FFN_ROWS, rows)
        for r0 in range(0, rows, step):
            rs = slice(r0, r0 + step)
            x = x_refs[k][rs, :]
            hg = jnp.dot(x, wgb[...], preferred_element_type=F32)
            hu = jnp.dot(x, wub[...], preferred_element_type=F32)
            hid = (hg * _sigmoid(hg) * hu).astype(BF16)
            acc_refs[k][rs, :] += jnp.dot(hid, wdb[...], preferred_element_type=F32)

    @pl.when(f == nf - 1)
    def _():
        for k in range(ns):
            o_refs[k][...] = (acc_refs[k][...] * g_refs[k][...]).astype(o_refs[k].dtype)


def _expert_ffn(xgs, gates, w_gate, w_up, w_down, layer):
    ns = len(xgs)
    _, ne, d, ff = w_gate.shape
    tf = _lane_tile(ff, 256)
    nf = ff // tf
    in_specs, out_specs, out_shape, acc_scr, args = [], [], [], [], []
    for k in range(ns):
        rows = xgs[k].shape[1]
        in_specs.append(pl.BlockSpec((None, rows, d), lambda e, f: (e, 0, 0)))
        in_specs.append(pl.BlockSpec((None, rows, 1), lambda e, f: (e, 0, 0)))
        out_specs.append(pl.BlockSpec((None, rows, d), lambda e, f: (e, 0, 0)))
        out_shape.append(jax.ShapeDtypeStruct((ne, rows, d), BF16))
        acc_scr.append(pltpu.VMEM((rows, d), F32))
        args += [xgs[k], gates[k]]
    in_specs += [pl.BlockSpec((None, None, d, tf), lambda e, f: (layer, e, 0, f)),
                 pl.BlockSpec((None, None, d, tf), lambda e, f: (layer, e, 0, f)),
                 pl.BlockSpec((None, None, tf, d), lambda e, f: (layer, e, f, 0))]
    return pl.pallas_call(
        functools.partial(_ffn_kernel, ns=ns, nf=nf),
        grid=(ne, nf),
        in_specs=in_specs, out_specs=out_specs, out_shape=out_shape,
        scratch_shapes=[pltpu.VMEM((d, tf), BF16), pltpu.VMEM((d, tf), BF16), pltpu.VMEM((tf, d), BF16)] + acc_scr,
        compiler_params=_params("arbitrary", "arbitrary"), name="expert_ffn",
    )(*args, w_gate, w_up, w_down)


def _scatter_kernel(slot_ref, y_ref, x_ref, mod_ref, o_ref, *, ne, cap):
    sl = slot_ref[...]
    want = lax.broadcasted_iota(jnp.int32, (1, cap), 1)
    acc = None
    for e in range(ne):
        sel = (sl[:, e:e + 1] == want).astype(BF16)
        p = jnp.dot(sel, y_ref[e], preferred_element_type=F32)
        acc = p if acc is None else acc + p
    o_ref[...] = x_ref[...] + mod_ref[5:6, :] * acc


def _scatter(slot, y, x, mod, layer, mod_row0, row0, n, bsz, cap, tn):
    ne = slot.shape[1]
    d = x.shape[1]
    tm = min(512, n)
    slot_t = jnp.swapaxes(slot, 1, 2)
    xb0 = row0 // tm
    per_seq = n // tm
    return pl.pallas_call(
        functools.partial(_scatter_kernel, ne=ne, cap=cap),
        grid=(bsz, d // tn, per_seq),
        in_specs=[pl.BlockSpec((None, tm, ne), lambda b, j, i: (b, i, 0)),
                  pl.BlockSpec((ne, cap, tn), lambda b, j, i: (0, b, j)),
                  pl.BlockSpec((tm, tn), lambda b, j, i: (xb0 + b * per_seq + i, j)),
                  pl.BlockSpec((None, None, 6, tn), lambda b, j, i: (layer, mod_row0(b), 0, j))],
        out_specs=pl.BlockSpec((tm, tn), lambda b, j, i: (b * per_seq + i, j)),
        out_shape=jax.ShapeDtypeStruct((bsz * n, d), F32),
        compiler_params=_params("arbitrary", "arbitrary", "arbitrary"), name="expert_scatter",
    )(slot_t, y, x, mod)


def kernel(x, c, ctx, c_ctx, w_mod, b_mod, g_norm1, w_in, lb_param, g_hgrn, w_pool, pool_scale, rpb,
           w_branch, w_out, g_norm2, w_router, w_gate_e, w_up_e, w_down_e, g_final):
    bsz, n, d = x.shape
    lc = ctx.shape[1]
    depth = w_mod.shape[0]
    width = lb_param.shape[2]
    heads = width // HEAD_DIM
    ne = w_router.shape[2]
    n_lat = bsz * n
    n_ctx = bsz * lc
    total = n_lat + n_ctx
    assert bsz + 1 <= 8 and n % ROW_TILE == 0 and lc % ROW_TILE == 0 and n_lat % lc == 0
    assert w_pool.shape[-1] * 4 == width and w_branch.shape[2] == width
    tm = _row_tile(n, n_ctx)
    tn = min(512, d)

    lb_all = jnp.cumsum(jax.nn.softmax(lb_param.astype(F32), axis=1), axis=1)
    lb_all = lb_all - lb_all[:, :1]
    lbp_all = jnp.stack([jnp.log(lb_all), jnp.log1p(-lb_all)], axis=2) * LOG2E
    lbp_all = lbp_all.reshape(2, depth, 2, heads, LANES).transpose(0, 1, 3, 2, 4)

    c8 = jnp.concatenate([c, c_ctx[None], jnp.zeros((8 - bsz - 1, d), F32)], axis=0)
    mod = _modulation(c8, w_mod, b_mod).reshape(depth, 8, 6, d)

    cm3, pair_masks, upper = _hgrn_constants()
    hgrn_consts = (jnp.asarray(cm3, BF16), jnp.asarray(pair_masks, F32), jnp.asarray(upper, F32))
    na_layout = _na_block_layout(n // GRID_W)
    na_bias = _na_bias_table(rpb)
    lanes_e = -(-ne // LANES) * LANES
    w_router_p = jnp.pad(w_router, ((0, 0), (0, 0), (0, lanes_e - ne)))

    x_lat = x.reshape(n_lat, d)
    x_ctx = ctx.reshape(n_ctx, d)
    zero_state = jnp.zeros((bsz, heads, LANES, LANES), F32)
    cap = EC_CAPACITY * n // ne
    cap_c = EC_CAPACITY * lc // ne
    a_cols = 5 * width
    b_cols = w_in.shape[2] - a_cols
    tn_in = _lane_tile(math.gcd(a_cols, b_cols), 1024)
    gate_col0 = 4 * width
    rt_all = _Rows(tm, n, bsz, n_lat, total)
    rt_lat = _Rows(tm, n, bsz, n_lat, n_lat)
    rn_all = _Rows(ROW_TILE, n, bsz, n_lat, total)
    rn_lat = _Rows(ROW_TILE, n, bsz, n_lat, n_lat)
    tm_out, tn_out = max(tm // 2, ROW_TILE), _lane_tile(d, 1024)
    ro_all = _Rows(tm_out, n, bsz, n_lat, total)
    ro_lat = _Rows(tm_out, n, bsz, n_lat, n_lat)

    for l in range(depth):
        last = l == depth - 1
        rt, rn, ro = (rt_lat, rn_lat, ro_lat) if last else (rt_all, rn_all, ro_all)
        gain_h = g_hgrn[l].reshape(heads, 1, LANES)

        h = _norm1(x_lat, x_ctx, g_norm1[l], mod, l, rn_all)
        ua = _matmul(h, w_in, l, 0, a_cols, F32, tm, tn_in)
        ub = _matmul(h, w_in, l, a_cols, b_cols, BF16, tm, tn_in)

        cy_a, s_f, s_b = _hgrn(ua, n_lat // lc, lc, bsz, heads, lbp_all[:, l], gain_h, zero_state, zero_state,
                               hgrn_consts)
        y_a, _, _ = _hgrn(ua, 0, n, bsz, heads, lbp_all[:, l], gain_h, s_f, s_b, hgrn_consts)
        y_b = _pool(ub, 0, n, bsz, w_pool, pool_scale, l)
        y_c, cy_c = _attention(ub, width, n, lc, bsz, heads, na_bias, l, na_layout)
        ys_lat = (y_a, y_b, y_c)
        ys_ctx = ys_lat if last else (cy_a, _pool(ub, n_lat // lc, lc, bsz, w_pool, pool_scale, l), cy_c)

        merged = _merge(ys_lat, ys_ctx, ub, gate_col0, w_branch, l, rt, tn)
        x_mid = _out_proj(merged, w_out, l, x_lat, x_ctx, mod, ro, tn_out)

        h2, aff = _norm2(x_mid, g_norm2[l], mod, l, rn, w_router_p, ne)
        afft, slot = _route(aff, 0, n, bsz, cap)
        xg, gate = _gather(slot, afft, h2, 0, n, bsz, cap)
        xgs, gates = [xg], [gate]
        if not last:
            afft_c, slot_c = _route(aff, n_lat, lc, bsz, cap_c)
            xg_c, gate_c = _gather(slot_c, afft_c, h2, n_lat // lc, lc, bsz, cap_c)
            xgs, gates = xgs + [xg_c], gates + [gate_c]
        ys = _expert_ffn(xgs, gates, w_gate_e, w_up_e, w_down_e, l)
        x_lat = _scatter(slot, ys[0], x_mid, mod, l, lambda b: b, 0, n, bsz, cap, tn)
        if not last:
            x_ctx = _scatter(slot_c, ys[1], x_mid, mod, l, lambda b: bsz, n_lat, lc, bsz, cap_c, tn)

    return _final_norm(x_lat, g_final).reshape(bsz, n, d)
```

```python
import functools
import math

import numpy as np
import jax
import jax.numpy as jnp
from jax import lax
from jax.experimental import pallas as pl
from jax.experimental.pallas import tpu as pltpu

F32 = jnp.float32
BF16 = jnp.bfloat16
EPS = 1e-6
LANES = 128
HEAD_DIM = 128
CHUNK = 64
HGRN_GROUP = 4
GRID_W = 64
POOL_WINDOWS = (2, 4, 8, 16)
POOL_PAD = 16
NA_ROWS_MAX = 8
NA_COLS = 16
NA_QROWS = 4
NA_KROWS = 12
ROPE_THETA = 10000.0
EC_CAPACITY = 2
GATHER_ROWS = 512
FFN_ROWS = 512
NEG = -1e30
ROW_TILE = 256
NORM_ROWS = 512
LOG2E = 1.4426950408889634
VMEM_LIMIT = 56 * 2 ** 20


def _params(*sem):
    return pltpu.CompilerParams(dimension_semantics=sem, vmem_limit_bytes=VMEM_LIMIT)


def _sigmoid(x):
    return 1.0 / (1.0 + jnp.exp(-x))


def _neg_abs(x):
    bits = lax.bitcast_convert_type(x, jnp.uint32) | jnp.uint32(0x80000000)
    return lax.bitcast_convert_type(bits, F32)


def _lane_tile(n, limit):
    t = min(limit, n) // LANES * LANES
    while n % t:
        t -= LANES
    return t


def _row_tile(n, ctx_rows):
    tm = 1024
    while n % tm or ctx_rows % tm:
        tm //= 2
    return tm


class _Rows:
    def __init__(self, tm, n, bsz, n_lat, rows):
        self.tm, self.n, self.bsz = tm, n, bsz
        self.lat_tiles = n_lat // tm
        self.tiles = rows // tm
        self.ctx_tiles = max(self.tiles - self.lat_tiles, 1)

    def lat(self, i):
        return jnp.minimum(i, self.lat_tiles - 1)

    def ctx(self, i):
        return jnp.clip(i - self.lat_tiles, 0, self.ctx_tiles - 1)

    def mod_row(self, i):
        return jnp.minimum(i * self.tm // self.n, self.bsz)


def _pick(is_lat, lat_ref, ctx_ref):
    return jnp.where(is_lat, lat_ref[...], ctx_ref[...])


def _mod_kernel(c_ref, w_ref, b_ref, o_ref):
    c = c_ref[...]
    sc = (c * _sigmoid(c)).astype(BF16)
    o_ref[...] = jnp.dot(sc, w_ref[...].astype(BF16), preferred_element_type=F32) + b_ref[...]


def _modulation(c8, w_mod, b_mod):
    depth, d, n6 = w_mod.shape
    tn = _lane_tile(n6, 1024)
    return pl.pallas_call(
        _mod_kernel,
        grid=(depth, n6 // tn),
        in_specs=[pl.BlockSpec((8, d), lambda l, j: (0, 0)),
                  pl.BlockSpec((None, d, tn), lambda l, j: (l, 0, j)),
                  pl.BlockSpec((None, 1, tn), lambda l, j: (l, 0, j))],
        out_specs=pl.BlockSpec((None, 8, tn), lambda l, j: (l, 0, j)),
        out_shape=jax.ShapeDtypeStruct((depth, 8, n6), F32),
        compiler_params=_params("arbitrary", "arbitrary"),
        name="modulation",
    )(c8, w_mod, b_mod.reshape(depth, 1, n6))


def _rmsnorm(x, gain):
    return x * lax.rsqrt(jnp.mean(x * x, axis=-1, keepdims=True) + EPS) * gain


def _modulate(y, mod_ref, shift_idx):
    return y * (1.0 + mod_ref[shift_idx + 1:shift_idx + 2, :]) + mod_ref[shift_idx:shift_idx + 1, :]


def _norm1_kernel(xl_ref, xc_ref, g_ref, mod_ref, o_ref, *, lat_tiles):
    x = _pick(pl.program_id(0) < lat_tiles, xl_ref, xc_ref)
    o_ref[...] = _modulate(_rmsnorm(x, g_ref[...]), mod_ref, 0).astype(o_ref.dtype)


def _norm2_kernel(x_ref, g_ref, mod_ref, wr_ref, o_ref, aff_ref):
    h = _modulate(_rmsnorm(x_ref[...], g_ref[...]), mod_ref, 3)
    o_ref[...] = h.astype(o_ref.dtype)
    logits = jnp.dot(h, wr_ref[...], precision=lax.Precision.HIGHEST, preferred_element_type=F32)
    logits = logits[:, :aff_ref.shape[1]]
    e = jnp.exp(logits - jnp.max(logits, axis=-1, keepdims=True))
    aff_ref[...] = e / jnp.sum(e, axis=-1, keepdims=True)


def _final_norm_kernel(x_ref, g_ref, o_ref):
    o_ref[...] = _rmsnorm(x_ref[...], g_ref[...])


def _norm1(x_lat, x_ctx, gain, mod, layer, rt):
    d = x_lat.shape[1]
    return pl.pallas_call(
        functools.partial(_norm1_kernel, lat_tiles=rt.lat_tiles),
        grid=(rt.tiles,),
        in_specs=[pl.BlockSpec((rt.tm, d), lambda i: (rt.lat(i), 0)),
                  pl.BlockSpec((rt.tm, d), lambda i: (rt.ctx(i), 0)),
                  pl.BlockSpec((1, d), lambda i: (0, 0)),
                  pl.BlockSpec((None, None, 6, d), lambda i: (layer, rt.mod_row(i), 0, 0))],
        out_specs=pl.BlockSpec((rt.tm, d), lambda i: (i, 0)),
        out_shape=jax.ShapeDtypeStruct((rt.tiles * rt.tm, d), BF16),
        compiler_params=_params("arbitrary"), name="norm1",
    )(x_lat, x_ctx, gain.reshape(1, d), mod)


def _norm2(x, gain, mod, layer, rt, w_router_padded, ne):
    d = x.shape[1]
    rows = rt.tiles * rt.tm
    return pl.pallas_call(
        _norm2_kernel,
        grid=(rt.tiles,),
        in_specs=[pl.BlockSpec((rt.tm, d), lambda i: (i, 0)),
                  pl.BlockSpec((1, d), lambda i: (0, 0)),
                  pl.BlockSpec((None, None, 6, d), lambda i: (layer, rt.mod_row(i), 0, 0)),
                  pl.BlockSpec((None, d, w_router_padded.shape[2]), lambda i: (layer, 0, 0))],
        out_specs=[pl.BlockSpec((rt.tm, d), lambda i: (i, 0)), pl.BlockSpec((rt.tm, ne), lambda i: (i, 0))],
        out_shape=[jax.ShapeDtypeStruct((rows, d), BF16), jax.ShapeDtypeStruct((rows, ne), F32)],
        compiler_params=_params("arbitrary"), name="norm2_router",
    )(x, gain.reshape(1, d), mod, w_router_padded)


def _final_norm(x, gain, tm):
    rows, d = x.shape
    return pl.pallas_call(
        _final_norm_kernel,
        grid=(rows // tm,),
        in_specs=[pl.BlockSpec((tm, d), lambda i: (i, 0)), pl.BlockSpec((1, d), lambda i: (0, 0))],
        out_specs=pl.BlockSpec((tm, d), lambda i: (i, 0)),
        out_shape=jax.ShapeDtypeStruct((rows, d), F32),
        compiler_params=_params("arbitrary"), name="final_norm",
    )(x, gain.reshape(1, d))


def _mm_kernel(a_ref, w_ref, o_ref, wbf_ref):
    @pl.when(pl.program_id(1) == 0)
    def _():
        wbf_ref[...] = w_ref[...].astype(BF16)
    o_ref[...] = jnp.dot(a_ref[...], wbf_ref[...], preferred_element_type=F32).astype(o_ref.dtype)


def _matmul(a, w_all, layer, col0, ncols, out_dtype, tm, tn):
    m, k = a.shape
    off = col0 // tn
    return pl.pallas_call(
        _mm_kernel,
        grid=(ncols // tn, m // tm),
        in_specs=[pl.BlockSpec((tm, k), lambda j, i: (i, 0)),
                  pl.BlockSpec((None, k, tn), lambda j, i: (layer, 0, j + off))],
        out_specs=pl.BlockSpec((tm, tn), lambda j, i: (i, j)),
        out_shape=jax.ShapeDtypeStruct((m, ncols), out_dtype),
        scratch_shapes=[pltpu.VMEM((k, tn), BF16)],
        compiler_params=_params("arbitrary", "arbitrary"), name="matmul",
    )(a, w_all)


def _out_proj_kernel(a_ref, w_ref, xl_ref, xc_ref, mod_ref, o_ref, wbf_ref, *, lat_tiles):
    @pl.when(pl.program_id(1) == 0)
    def _():
        wbf_ref[...] = w_ref[...].astype(BF16)
    acc = jnp.dot(a_ref[...], wbf_ref[...], preferred_element_type=F32)
    x = _pick(pl.program_id(1) < lat_tiles, xl_ref, xc_ref)
    o_ref[...] = x + mod_ref[2:3, :] * acc


def _out_proj(a, w_all, layer, x_lat, x_ctx, mod, rt, tn):
    k = a.shape[1]
    d = w_all.shape[2]
    tm = rt.tm
    return pl.pallas_call(
        functools.partial(_out_proj_kernel, lat_tiles=rt.lat_tiles),
        grid=(d // tn, rt.tiles),
        in_specs=[pl.BlockSpec((tm, k), lambda j, i: (i, 0)),
                  pl.BlockSpec((None, k, tn), lambda j, i: (layer, 0, j)),
                  pl.BlockSpec((tm, tn), lambda j, i: (rt.lat(i), j)),
                  pl.BlockSpec((tm, tn), lambda j, i: (rt.ctx(i), j)),
                  pl.BlockSpec((None, None, 6, tn), lambda j, i: (layer, rt.mod_row(i), 0, j))],
        out_specs=pl.BlockSpec((tm, tn), lambda j, i: (i, j)),
        out_shape=jax.ShapeDtypeStruct((rt.tiles * tm, d), F32),
        scratch_shapes=[pltpu.VMEM((k, tn), BF16)],
        compiler_params=_params("arbitrary", "arbitrary"), name="out_proj",
    )(a, w_all, x_lat, x_ctx, mod)


def _merge_kernel(ya_ref, yb_ref, yc_ref, cya_ref, cyb_ref, cyc_ref, g0_ref, g1_ref, g2_ref, w_ref, o_ref, wbf_ref,
                  *, lat_tiles):
    @pl.when(pl.program_id(1) == 0)
    def _():
        wbf_ref[...] = w_ref[...].astype(BF16)
    is_lat = pl.program_id(1) < lat_tiles
    acc = None
    for j, (y_ref, cy_ref, g_ref) in enumerate(((ya_ref, cya_ref, g0_ref), (yb_ref, cyb_ref, g1_ref),
                                                (yc_ref, cyc_ref, g2_ref))):
        p = jnp.dot(_pick(is_lat, y_ref, cy_ref), wbf_ref[j], preferred_element_type=F32)
        t = _sigmoid(g_ref[...].astype(F32)) * p
        acc = t if acc is None else acc + t
    o_ref[...] = acc.astype(o_ref.dtype)


def _merge(ys_lat, ys_ctx, ub, gate_col0, w_branch, layer, rt, tn):
    bw = ys_lat[0].shape[1]
    d = w_branch.shape[3]
    tm = rt.tm
    lat_spec = pl.BlockSpec((tm, bw), lambda j, i: (rt.lat(i), 0))
    ctx_spec = pl.BlockSpec((tm, bw), lambda j, i: (rt.ctx(i), 0))

    def gate_spec(k):
        off = (gate_col0 + k * d) // tn
        return pl.BlockSpec((tm, tn), lambda j, i: (i, off + j))

    return pl.pallas_call(
        functools.partial(_merge_kernel, lat_tiles=rt.lat_tiles),
        grid=(d // tn, rt.tiles),
        in_specs=[lat_spec] * 3 + [ctx_spec] * 3 + [gate_spec(0), gate_spec(1), gate_spec(2),
                  pl.BlockSpec((None, 3, bw, tn), lambda j, i: (layer, 0, 0, j))],
        out_specs=pl.BlockSpec((tm, tn), lambda j, i: (i, j)),
        out_shape=jax.ShapeDtypeStruct((rt.tiles * tm, d), BF16),
        scratch_shapes=[pltpu.VMEM((3, bw, tn), BF16)],
        compiler_params=_params("arbitrary", "arbitrary"), name="merge",
    )(*ys_lat, *ys_ctx, ub, ub, ub, w_branch)


_HGRN_BLOCKS = (32, 16, 8, 4, 2, 1)


def _hgrn_constants():
    c = CHUNK
    t = np.arange(c)
    tri_f = (t[:, None] >= t[None, :]).astype(np.float32)
    tri_b = np.ascontiguousarray(tri_f[::-1, ::-1])
    sel_f, sel_b, masks, upper = [tri_f], [tri_b], [], []
    for m in _HGRN_BLOCKS:
        blk = (t // (2 * m)) * (2 * m)
        up = (t % (2 * m)) >= m
        sel_f.append(tri_f - tri_f[blk + m - 1])
        sel_b.append(tri_b - tri_b[blk + m])
        same = (t[:, None] // (2 * m)) == (t[None, :] // (2 * m))
        masks.append(same & (up[:, None] != up[None, :]))
        upper.append(np.broadcast_to(up[:, None], (c, LANES)))
    masks.append(np.eye(c, dtype=bool))
    sel = np.stack([np.concatenate(sel_f, axis=0), np.concatenate(sel_b, axis=0)])
    return (np.concatenate([sel, sel, sel], axis=2), np.stack(masks).astype(np.float32),
            np.stack(upper).astype(np.float32))


def _hgrn_log2_gate(z, lbp):
    zl = z * LOG2E
    log_sig = jnp.minimum(zl, 0.0) - jnp.log2(1.0 + jnp.exp2(_neg_abs(zl)))
    a = lbp[0:1, :]
    cc = lbp[1:2, :] + log_sig
    return jnp.maximum(a, cc) + jnp.log2(1.0 + jnp.exp2(_neg_abs(a - cc)))


def _split3(x):
    g1 = x.astype(BF16)
    r1 = x - g1.astype(F32)
    g2 = r1.astype(BF16)
    g3 = (r1 - g2.astype(F32)).astype(BF16)
    return jnp.concatenate([g1, g2, g3], axis=0)


_NT = (((1,), (1,)), ((), ()))
_TN = (((0,), (0,)), ((), ()))


def _hgrn_kernel(q_ref, ff_ref, fb_ref, i_ref, g_ref, lbf_ref, lbb_ref, gain_ref, cm_ref, mask_ref, up_ref,
                 sf0_ref, sb0_ref, y_ref, sf_ref, sb_ref, o_scr, qt_scr, upd_scr, dec_scr, st_scr, *, nc):
    z_refs = (ff_ref, fb_ref)
    lb_refs = (lbf_ref, lbb_ref)
    nl = len(_HGRN_BLOCKS)
    last_row = (CHUNK - 1, 0)

    def local_pass(it, carry):
        rows = [pl.ds(pl.multiple_of((it * HGRN_GROUP + s) * CHUNK, CHUNK), CHUNK) for s in range(HGRN_GROUP)]
        logf = [[_hgrn_log2_gate(z_refs[d][r, :], lb_refs[d][...]) for r in rows] for d in range(2)]
        sums = [jnp.dot(cm_ref[d], jnp.concatenate([_split3(lf) for lf in logf[d]], axis=1),
                        preferred_element_type=F32) for d in range(2)]
        up = up_ref[...] > 0.5
        for s, r in enumerate(rows):
            lanes = slice(s * LANES, (s + 1) * LANES)
            qr = q_ref[r, :]
            q = qr * _sigmoid(qr)
            vb = i_ref[r, :].astype(BF16)
            b = [sums[d][0:CHUNK, lanes] for d in range(2)]
            lvl = [jnp.exp2(_neg_abs(sums[d][CHUNK:, lanes].reshape(nl, CHUNK, LANES))) for d in range(2)]
            k = [1.0 - jnp.exp2(logf[d][s]) for d in range(2)]
            q_dec = jnp.where(up, lvl[0], lvl[1])
            k_dec = jnp.where(up, k[1][None] * lvl[1], k[0][None] * lvl[0])
            qs = jnp.concatenate([q[None] * q_dec, q[None]], axis=0).astype(BF16)
            ks = jnp.concatenate([k_dec, (k[0] + k[1])[None]], axis=0).astype(BF16)
            sc = jnp.einsum("ltk,lsk->lts", qs, ks, preferred_element_type=F32)
            amat = jnp.sum(sc * mask_ref[...], axis=0)
            o_scr[r, :] = jnp.dot(amat.astype(BF16), vb, preferred_element_type=F32)
            tot = [b[d][last_row[d]:last_row[d] + 1, :] for d in range(2)]
            kd = jnp.concatenate([(k[d] * jnp.exp2(tot[d] - b[d])).astype(BF16) for d in range(2)], axis=1)
            upd = lax.dot_general(vb, kd, _TN, preferred_element_type=F32)
            j = it * HGRN_GROUP + s
            for d in range(2):
                qt_scr[d, r, :] = (q * jnp.exp2(b[d])).astype(BF16)
                upd_scr[d, j] = upd[:, d * LANES:(d + 1) * LANES]
                dec_scr[d, pl.ds(j, 1), :] = jnp.exp2(tot[d])
        return carry

    lax.fori_loop(0, nc // HGRN_GROUP, local_pass, 0)

    st_scr[0] = sf0_ref[...]
    st_scr[1] = sb0_ref[...]

    def state_pass(it, carry):
        for d in range(2):
            j = it if d == 0 else nc - 1 - it
            rows = pl.ds(pl.multiple_of(j * CHUNK, CHUNK), CHUNK)
            st = st_scr[d]
            o_scr[rows, :] += lax.dot_general(qt_scr[d, rows, :], st.astype(BF16), _NT, preferred_element_type=F32)
            st_scr[d] = st * dec_scr[d, pl.ds(j, 1), :] + upd_scr[d, j]
        return carry

    lax.fori_loop(0, nc, state_pass, 0, unroll=min(8, nc))
    sf_ref[...] = st_scr[0]
    sb_ref[...] = st_scr[1]
    o = o_scr[...]
    o = o * lax.rsqrt(jnp.mean(o * o, axis=-1, keepdims=True) + EPS) * gain_ref[...]
    g = g_ref[...]
    y_ref[...] = (o * (g * _sigmoid(g))).astype(y_ref.dtype)


def _hgrn(ua, row_block0, n, bsz, heads, lbp, gain, sf0, sb0, consts):
    nc = n // CHUNK
    assert nc % HGRN_GROUP == 0 and nc % 2 == 0

    def col(k):
        return pl.BlockSpec((n, LANES), lambda b, h: (row_block0 + b, k * heads + h))

    lb_spec = lambda d: pl.BlockSpec((None, None, 2, LANES), lambda b, h: (d, h, 0, 0))
    st_spec = pl.BlockSpec((None, None, LANES, LANES), lambda b, h: (b, h, 0, 0))
    st_shape = jax.ShapeDtypeStruct((bsz, heads, LANES, LANES), F32)
    const_specs = [pl.BlockSpec(a.shape, lambda b, h, nd=a.ndim: (0,) * nd) for a in consts]
    return pl.pallas_call(
        functools.partial(_hgrn_kernel, nc=nc),
        grid=(bsz, heads),
        in_specs=[col(0), col(1), col(2), col(3), col(4), lb_spec(0), lb_spec(1),
                  pl.BlockSpec((None, 1, LANES), lambda b, h: (h, 0, 0))] + const_specs + [st_spec, st_spec],
        out_specs=[pl.BlockSpec((n, LANES), lambda b, h: (b, h)), st_spec, st_spec],
        out_shape=[jax.ShapeDtypeStruct((bsz * n, heads * LANES), BF16), st_shape, st_shape],
        scratch_shapes=[pltpu.VMEM((n, LANES), F32), pltpu.VMEM((2, n, LANES), BF16),
                        pltpu.VMEM((2, nc, LANES, LANES), F32), pltpu.VMEM((2, nc, LANES), F32),
                        pltpu.VMEM((2, LANES, LANES), F32)],
        compiler_params=_params("arbitrary", "arbitrary"), name="hgrn2",
    )(ua, ua, ua, ua, ua, lbp, lbp, gain, *consts, sf0, sb0)


def _pool_kernel(u_ref, w_ref, s_ref, y_ref, pad_ref, *, n, group):
    pos = lax.broadcasted_iota(jnp.int32, (n, 1), 0)
    zeros = jnp.zeros((POOL_PAD, pad_ref.shape[1]), F32)
    pad_ref[0:POOL_PAD, :] = zeros
    pad_ref[POOL_PAD + n:2 * POOL_PAD + n, :] = zeros
    pad_ref[POOL_PAD:POOL_PAD + n, :] = u_ref[...].astype(F32)
    for gi, w in enumerate(POOL_WINDOWS):
        cols = slice(gi * group, (gi + 1) * group)
        acc = None
        for dlt in range(-(w // 2), w // 2):
            t = pad_ref[POOL_PAD + dlt:POOL_PAD + dlt + n, cols]
            acc = t if acc is None else acc + t
        lo = jnp.maximum(pos - w // 2, 0)
        hi = jnp.minimum(pos + w // 2 - 1, n - 1)
        cnt = (hi - lo + 1).astype(F32)
        dd = acc / cnt - pad_ref[POOL_PAD:POOL_PAD + n, cols]
        y = jnp.dot(dd.astype(BF16), w_ref[gi].astype(BF16), preferred_element_type=F32)
        y_ref[:, cols] = (y * s_ref[:, cols]).astype(y_ref.dtype)


def _pool(ub, row_block0, n, bsz, w_pool, scale, layer):
    group = w_pool.shape[-1]
    width = 4 * group
    return pl.pallas_call(
        functools.partial(_pool_kernel, n=n, group=group),
        grid=(bsz,),
        in_specs=[pl.BlockSpec((n, width), lambda b: (row_block0 + b, 0)),
                  pl.BlockSpec((None, 4, group, group), lambda b: (layer, 0, 0, 0)),
                  pl.BlockSpec((None, 1, width), lambda b: (layer, 0, 0))],
        out_specs=pl.BlockSpec((n, width), lambda b: (b, 0)),
        out_shape=jax.ShapeDtypeStruct((bsz * n, width), BF16),
        scratch_shapes=[pltpu.VMEM((n + 2 * POOL_PAD, width), F32)],
        compiler_params=_params("arbitrary"), name="pool",
    )(ub, w_pool, scale.reshape(scale.shape[0], 1, width))


def _na_tables(n):
    pos = np.arange(n)
    half = HEAD_DIM // 2
    inv_freq = ROPE_THETA ** (-np.arange(0, half, 2, dtype=np.float64) / half)
    lane = np.arange(HEAD_DIM)
    p = np.where(lane[None, :] < half, (pos // GRID_W)[:, None], (pos % GRID_W)[:, None]).astype(np.float64)
    ang = p * inv_freq[lane % (half // 2)][None, :]
    sign = np.where((lane % half) < half // 2, -1.0, 1.0)[None, :]
    return np.cos(ang).astype(np.float32), (np.sin(ang) * sign).astype(np.float32)


def _na_block_layout(rows):
    kr = NA_ROWS_MAX
    nblk = rows // NA_QROWS
    starts, patterns, types = [], [], []
    for j in range(nblk):
        u = int(np.clip(NA_QROWS * j - kr // 2, 0, rows - NA_KROWS))
        r = NA_QROWS * j + np.arange(NA_QROWS)
        start_r = np.clip(r - kr // 2, 0, rows - kr)
        kabs = u + np.arange(NA_KROWS)
        valid = (kabs[None, :] >= start_r[:, None]) & (kabs[None, :] < start_r[:, None] + kr)
        assert valid.sum(axis=1).min() == kr
        dr = np.clip(kabs[None, :] - r[:, None] + NA_ROWS_MAX - 1, 0, 2 * NA_ROWS_MAX - 2)
        key = (valid.tobytes(), dr.tobytes())
        keys = [p[0] for p in patterns]
        if key not in keys:
            patterns.append((key, valid, dr))
        types.append([p[0] for p in patterns].index(key))
        starts.append(u)
    return starts, types, [(p[1], p[2]) for p in patterns]


NA_DR = 2 * NA_ROWS_MAX - 1


def _na_bias_table(rpb):
    qcol = np.arange(GRID_W)
    col_start = np.clip(qcol - NA_COLS // 2, 0, GRID_W - NA_COLS)
    kcol = np.arange(GRID_W)
    col_mask = (kcol[None, :] >= col_start[:, None]) & (kcol[None, :] < col_start[:, None] + NA_COLS)
    dc = np.clip(kcol[None, :] - qcol[:, None] + NA_COLS - 1, 0, 2 * NA_COLS - 2)
    col_hot = (dc[None] == np.arange(2 * NA_COLS - 1)[:, None, None]).astype(np.float32)
    by_col = jnp.einsum("lhab,bqk->lhaqk", rpb.astype(F32), col_hot, precision=lax.Precision.HIGHEST)
    by_col = jnp.where(col_mask, by_col, NEG)
    masked = jnp.full(by_col.shape[:2] + (1, GRID_W, GRID_W), NEG, F32)
    table = jnp.concatenate([by_col, masked], axis=2)
    return jnp.concatenate([table, table], axis=-1)


def _na_pieces(patterns):
    out = []
    for valid, dr in patterns:
        idx = np.where(valid, dr, NA_DR)
        out.append(tuple(tuple((int(idx[r, 2 * p]), int(idx[r, 2 * p + 1])) for p in range(NA_KROWS // 2))
                         for r in range(NA_QROWS)))
    return tuple(out)


def _softmax_pv(s_list, v_list):
    m = None
    for s in s_list:
        mm = jnp.max(s, axis=-1, keepdims=True)
        m = mm if m is None else jnp.maximum(m, mm)
    num, den = None, None
    for s, v in zip(s_list, v_list):
        p = jnp.exp(s - m)
        ssum = jnp.sum(p, axis=-1, keepdims=True)
        o = jnp.dot(p.astype(BF16), v, preferred_element_type=F32)
        num = o if num is None else num + o
        den = ssum if den is None else den + ssum
    return num / den


def _na_kernel(q_ref, k_ref, v_ref, cq_ref, ck_ref, cv_ref, tab_ref, cos_ref, sin_ref, y_ref, cy_ref,
               qs_ref, ks_ref, bias_ref, *, starts, types, pieces):
    lane = lax.broadcasted_iota(jnp.int32, (1, HEAD_DIM), 1)
    first = (lane % (HEAD_DIM // 2)) < HEAD_DIM // 4
    scale = HEAD_DIM ** -0.5

    @pl.when(pl.program_id(1) == 0)
    def _():
        even = lane < GRID_W
        for tp, by_row in enumerate(pieces):
            for r, by_pair in enumerate(by_row):
                for p, (ie, io) in enumerate(by_pair):
                    bias_ref[tp, r * GRID_W:(r + 1) * GRID_W, 2 * p * GRID_W:2 * (p + 1) * GRID_W] = (
                        jnp.where(even, tab_ref[ie], tab_ref[io]))

    def rope(t):
        partner = jnp.where(first, pltpu.roll(t, HEAD_DIM - HEAD_DIM // 4, axis=1),
                            pltpu.roll(t, HEAD_DIM // 4, axis=1))
        return t * cos_ref[...] + partner * sin_ref[...]

    qs_ref[...] = (rope(q_ref[...].astype(F32)) * scale).astype(BF16)
    ks_ref[...] = rope(k_ref[...].astype(F32)).astype(BF16)
    ck = ck_ref[...]
    cv = cv_ref[...]
    qrows = NA_QROWS * GRID_W
    krows = NA_KROWS * GRID_W
    for j, (u, tp) in enumerate(zip(starts, types)):
        qb = qs_ref[j * qrows:(j + 1) * qrows, :]
        kb = ks_ref[u * GRID_W:u * GRID_W + krows, :]
        vb = v_ref[u * GRID_W:u * GRID_W + krows, :]
        s_loc = lax.dot_general(qb, kb, _NT, preferred_element_type=F32) + bias_ref[tp]
        s_ctx = lax.dot_general(qb, ck, _NT, preferred_element_type=F32)
        y_ref[j * qrows:(j + 1) * qrows, :] = _softmax_pv([s_loc, s_ctx], [vb, cv]).astype(y_ref.dtype)
    s = lax.dot_general(cq_ref[...], ck, _NT, preferred_element_type=F32) * scale
    cy_ref[...] = _softmax_pv([s], [cv]).astype(cy_ref.dtype)


def _attention(ub, col0, n, lc, bsz, heads, table, layer, layout):
    starts, types, patterns = layout
    cos, sin = _na_tables(n)
    cb = col0 // HEAD_DIM
    ctx0 = bsz * n // lc

    def lat(k):
        return pl.BlockSpec((n, HEAD_DIM), lambda h, b: (b, cb + k * heads + h))

    def ctx(k):
        return pl.BlockSpec((lc, HEAD_DIM), lambda h, b: (ctx0 + b, cb + k * heads + h))

    tab = pl.BlockSpec((n, HEAD_DIM), lambda h, b: (0, 0))
    return pl.pallas_call(
        functools.partial(_na_kernel, starts=tuple(starts), types=tuple(types), pieces=_na_pieces(patterns)),
        grid=(heads, bsz),
        in_specs=[lat(0), lat(1), lat(2), ctx(0), ctx(1), ctx(2),
                  pl.BlockSpec((None, None) + table.shape[2:], lambda h, b: (layer, h, 0, 0, 0)), tab, tab],
        out_specs=[pl.BlockSpec((n, HEAD_DIM), lambda h, b: (b, h)),
                   pl.BlockSpec((lc, HEAD_DIM), lambda h, b: (b, h))],
        out_shape=[jax.ShapeDtypeStruct((bsz * n, heads * HEAD_DIM), BF16),
                   jax.ShapeDtypeStruct((bsz * lc, heads * HEAD_DIM), BF16)],
        scratch_shapes=[pltpu.VMEM((n, HEAD_DIM), BF16), pltpu.VMEM((n, HEAD_DIM), BF16),
                        pltpu.VMEM((len(patterns), NA_QROWS * GRID_W, NA_KROWS * GRID_W), F32)],
        compiler_params=_params("arbitrary", "arbitrary"), name="attention",
    )(ub, ub, ub, ub, ub, ub, table, jnp.asarray(cos), jnp.asarray(sin))


def _route_kernel(afft_ref, slot_ref, *, n, ne, cap):
    bits = lax.bitcast_convert_type(afft_ref[...], jnp.int32)

    def bisect(i, thr):
        cand = thr | jnp.left_shift(jnp.int32(1), 30 - i)
        cnt = jnp.sum((bits >= cand).astype(F32), axis=1, keepdims=True)
        return jnp.where(cnt >= cap, cand, thr)

    thr = lax.fori_loop(0, 31, bisect, jnp.zeros((ne, 1), jnp.int32))
    above = bits > thr
    tied = bits == thr
    need = cap - jnp.sum(above.astype(F32), axis=1, keepdims=True)

    row = lax.broadcasted_iota(jnp.int32, (LANES, LANES), 0)
    colm = lax.broadcasted_iota(jnp.int32, (LANES, LANES), 1)
    before = (row < colm).astype(BF16)
    ones = jnp.ones((LANES, LANES), BF16)

    def prefix(x):
        xb = x.astype(BF16)
        outs, carry = [], jnp.zeros((ne, LANES), F32)
        for blk in range(n // LANES):
            xs = xb[:, blk * LANES:(blk + 1) * LANES]
            outs.append(jnp.dot(xs, before, preferred_element_type=F32) + carry)
            carry = carry + jnp.dot(xs, ones, preferred_element_type=F32)
        return jnp.concatenate(outs, axis=1)

    chosen = above | (tied & (prefix(tied.astype(F32)) < need))
    slot = prefix(chosen.astype(F32)).astype(jnp.int32)
    slot_ref[...] = jnp.where(chosen, slot, n)


def _route(aff, row0, n, bsz, cap):
    ne = aff.shape[1]
    at = jnp.swapaxes(aff[row0:row0 + bsz * n].reshape(bsz, n, ne), 1, 2)
    spec = pl.BlockSpec((None, ne, n), lambda b: (b, 0, 0))
    slot = pl.pallas_call(
        functools.partial(_route_kernel, n=n, ne=ne, cap=cap),
        grid=(bsz,), in_specs=[spec], out_specs=spec,
        out_shape=jax.ShapeDtypeStruct((bsz, ne, n), jnp.int32),
        compiler_params=_params("arbitrary"), name="expert_route",
    )(at)
    return at, slot


def _gather_kernel(slot_ref, afft_ref, h_ref, xg_ref, g_ref, *, cap, eg):
    e0 = pl.program_id(1) * eg
    want = lax.broadcasted_iota(jnp.int32, (cap, 1), 0)
    sels = [slot_ref[pl.ds(e0 + k, 1), :] == want for k in range(eg)]
    sel = sels[0].astype(BF16) if eg == 1 else jnp.concatenate([s.astype(BF16) for s in sels], axis=0)
    xg = jnp.dot(sel, h_ref[...], preferred_element_type=F32).astype(xg_ref.dtype)
    xg_ref[...] = xg.reshape(xg_ref.shape)
    for k in range(eg):
        g_ref[k] = jnp.sum(jnp.where(sels[k], afft_ref[pl.ds(e0 + k, 1), :], 0.0), axis=1, keepdims=True)


def _gather(slot, afft, h2, row_block0, n, bsz, cap):
    ne = slot.shape[1]
    d = h2.shape[1]
    eg = max(1, min(ne, GATHER_ROWS // cap))
    while ne % eg:
        eg -= 1
    row_spec = pl.BlockSpec((None, ne, n), lambda b, e: (b, 0, 0))
    return pl.pallas_call(
        functools.partial(_gather_kernel, cap=cap, eg=eg),
        grid=(bsz, ne // eg),
        in_specs=[row_spec, row_spec, pl.BlockSpec((n, d), lambda b, e: (row_block0 + b, 0))],
        out_specs=[pl.BlockSpec((eg, cap, d), lambda b, e: (e, b, 0)),
                   pl.BlockSpec((eg, cap, 1), lambda b, e: (e, b, 0))],
        out_shape=[jax.ShapeDtypeStruct((ne, bsz * cap, d), BF16),
                   jax.ShapeDtypeStruct((ne, bsz * cap, 1), F32)],
        compiler_params=_params("arbitrary", "arbitrary"), name="expert_gather",
    )(slot, afft, h2)


def _ffn_kernel(*refs, ns, nf):
    x_refs = refs[0:2 * ns:2]
    g_refs = refs[1:2 * ns:2]
    wg_ref, wu_ref, wd_ref = refs[2 * ns:2 * ns + 3]
    o_refs = refs[2 * ns + 3:3 * ns + 3]
    wgb, wub, wdb = refs[3 * ns + 3:3 * ns + 6]
    acc_refs = refs[3 * ns + 6:]
    f = pl.program_id(1)
    wgb[...] = wg_ref[...].astype(BF16)
    wub[...] = wu_ref[...].astype(BF16)
    wdb[...] = wd_ref[...].astype(BF16)
    @pl.when(f == 0)
    def _():
        for acc in acc_refs:
            acc[...] = jnp.zeros(acc.shape, F32)

    for k in range(ns):
        rows = x_refs[k].shape[0]
        step = min(FFN_ROWS, rows)
        for r0 in range(0, rows, step):
            rs = slice(r0, r0 + step)
            x = x_refs[k][rs, :]
            hg = jnp.dot(x, wgb[...], preferred_element_type=F32)
            hu = jnp.dot(x, wub[...], preferred_element_type=F32)
            hid = (hg * _sigmoid(hg) * hu).astype(BF16)
            acc_refs[k][rs, :] += jnp.dot(hid, wdb[...], preferred_element_type=F32)

    @pl.when(f == nf - 1)
    def _():
        for k in range(ns):
            o_refs[k][...] = (acc_refs[k][...] * g_refs[k][...]).astype(o_refs[k].dtype)


def _expert_ffn(xgs, gates, w_gate, w_up, w_down, layer):
    ns = len(xgs)
    _, ne, d, ff = w_gate.shape
    tf = _lane_tile(ff, 256)
    nf = ff // tf
    in_specs, out_specs, out_shape, acc_scr, args = [], [], [], [], []
    for k in range(ns):
        rows = xgs[k].shape[1]
        in_specs.append(pl.BlockSpec((None, rows, d), lambda e, f: (e, 0, 0)))
        in_specs.append(pl.BlockSpec((None, rows, 1), lambda e, f: (e, 0, 0)))
        out_specs.append(pl.BlockSpec((None, rows, d), lambda e, f: (e, 0, 0)))
        out_shape.append(jax.ShapeDtypeStruct((ne, rows, d), BF16))
        acc_scr.append(pltpu.VMEM((rows, d), F32))
        args += [xgs[k], gates[k]]
    in_specs += [pl.BlockSpec((None, None, d, tf), lambda e, f: (layer, e, 0, f)),
                 pl.BlockSpec((None, None, d, tf), lambda e, f: (layer, e, 0, f)),
                 pl.BlockSpec((None, None, tf, d), lambda e, f: (layer, e, f, 0))]
    return pl.pallas_call(
        functools.partial(_ffn_kernel, ns=ns, nf=nf),
        grid=(ne, nf),
        in_specs=in_specs, out_specs=out_specs, out_shape=out_shape,
        scratch_shapes=[pltpu.VMEM((d, tf), BF16), pltpu.VMEM((d, tf), BF16), pltpu.VMEM((tf, d), BF16)] + acc_scr,
        compiler_params=_params("arbitrary", "arbitrary"), name="expert_ffn",
    )(*args, w_gate, w_up, w_down)


def _scatter_kernel(slot_ref, y_ref, x_ref, mod_ref, o_ref, *, ne, cap):
    sl = slot_ref[...]
    want = lax.broadcasted_iota(jnp.int32, (1, cap), 1)
    acc = None
    for e in range(ne):
        sel = (sl[:, e:e + 1] == want).astype(BF16)
        p = jnp.dot(sel, y_ref[e], preferred_element_type=F32)
        acc = p if acc is None else acc + p
    o_ref[...] = x_ref[...] + mod_ref[5:6, :] * acc


def _scatter(slot, y, x, mod, layer, mod_row0, row0, n, bsz, cap, tn):
    ne = slot.shape[1]
    d = x.shape[1]
    tm = min(512, n)
    slot_t = jnp.swapaxes(slot, 1, 2)
    xb0 = row0 // tm
    per_seq = n // tm
    return pl.pallas_call(
        functools.partial(_scatter_kernel, ne=ne, cap=cap),
        grid=(bsz, d // tn, per_seq),
        in_specs=[pl.BlockSpec((None, tm, ne), lambda b, j, i: (b, i, 0)),
                  pl.BlockSpec((ne, cap, tn), lambda b, j, i: (0, b, j)),
                  pl.BlockSpec((tm, tn), lambda b, j, i: (xb0 + b * per_seq + i, j)),
                  pl.BlockSpec((None, None, 6, tn), lambda b, j, i: (layer, mod_row0(b), 0, j))],
        out_specs=pl.BlockSpec((tm, tn), lambda b, j, i: (b * per_seq + i, j)),
        out_shape=jax.ShapeDtypeStruct((bsz * n, d), F32),
        compiler_params=_params("arbitrary", "arbitrary", "arbitrary"), name="expert_scatter",
    )(slot_t, y, x, mod)


def kernel(x, c, ctx, c_ctx, w_mod, b_mod, g_norm1, w_in, lb_param, g_hgrn, w_pool, pool_scale, rpb,
           w_branch, w_out, g_norm2, w_router, w_gate_e, w_up_e, w_down_e, g_final):
    bsz, n, d = x.shape
    lc = ctx.shape[1]
    depth = w_mod.shape[0]
    width = lb_param.shape[2]
    heads = width // HEAD_DIM
    ne = w_router.shape[2]
    n_lat = bsz * n
    n_ctx = bsz * lc
    total = n_lat + n_ctx
    assert bsz + 1 <= 8 and n % ROW_TILE == 0 and lc % ROW_TILE == 0 and n_lat % lc == 0
    assert w_pool.shape[-1] * 4 == width and w_branch.shape[2] == width
    tm = _row_tile(n, n_ctx)
    tn = min(512, d)

    lb_all = jnp.cumsum(jax.nn.softmax(lb_param.astype(F32), axis=1), axis=1)
    lb_all = lb_all - lb_all[:, :1]
    lbp_all = jnp.stack([jnp.log(lb_all), jnp.log1p(-lb_all)], axis=2) * LOG2E
    lbp_all = lbp_all.reshape(2, depth, 2, heads, LANES).transpose(0, 1, 3, 2, 4)

    c8 = jnp.concatenate([c, c_ctx[None], jnp.zeros((8 - bsz - 1, d), F32)], axis=0)
    mod = _modulation(c8, w_mod, b_mod).reshape(depth, 8, 6, d)

    cm3, pair_masks, upper = _hgrn_constants()
    hgrn_consts = (jnp.asarray(cm3, BF16), jnp.asarray(pair_masks, F32), jnp.asarray(upper, F32))
    na_layout = _na_block_layout(n // GRID_W)
    na_bias = _na_bias_table(rpb)
    lanes_e = -(-ne // LANES) * LANES
    w_router_p = jnp.pad(w_router, ((0, 0), (0, 0), (0, lanes_e - ne)))

    x_lat = x.reshape(n_lat, d)
    x_ctx = ctx.reshape(n_ctx, d)
    zero_state = jnp.zeros((bsz, heads, LANES, LANES), F32)
    cap = EC_CAPACITY * n // ne
    cap_c = EC_CAPACITY * lc // ne
    a_cols = 5 * width
    b_cols = w_in.shape[2] - a_cols
    tn_in = _lane_tile(math.gcd(a_cols, b_cols), 1024)
    gate_col0 = 4 * width
    rt_all = _Rows(tm, n, bsz, n_lat, total)
    rt_lat = _Rows(tm, n, bsz, n_lat, n_lat)
    tm_norm = min(NORM_ROWS, tm)
    rn_all = _Rows(tm_norm, n, bsz, n_lat, total)
    rn_lat = _Rows(tm_norm, n, bsz, n_lat, n_lat)
    tm_out, tn_out = max(tm // 2, ROW_TILE), _lane_tile(d, 1024)
    ro_all = _Rows(tm_out, n, bsz, n_lat, total)
    ro_lat = _Rows(tm_out, n, bsz, n_lat, n_lat)

    for l in range(depth):
        last = l == depth - 1
        rt, rn, ro = (rt_lat, rn_lat, ro_lat) if last else (rt_all, rn_all, ro_all)
        gain_h = g_hgrn[l].reshape(heads, 1, LANES)

        h = _norm1(x_lat, x_ctx, g_norm1[l], mod, l, rn_all)
        ua = _matmul(h, w_in, l, 0, a_cols, F32, tm, tn_in)
        ub = _matmul(h, w_in, l, a_cols, b_cols, BF16, tm, tn_in)

        cy_a, s_f, s_b = _hgrn(ua, n_lat // lc, lc, bsz, heads, lbp_all[:, l], gain_h, zero_state, zero_state,
                               hgrn_consts)
        y_a, _, _ = _hgrn(ua, 0, n, bsz, heads, lbp_all[:, l], gain_h, s_f, s_b, hgrn_consts)
        y_b = _pool(ub, 0, n, bsz, w_pool, pool_scale, l)
        y_c, cy_c = _attention(ub, width, n, lc, bsz, heads, na_bias, l, na_layout)
        ys_lat = (y_a, y_b, y_c)
        ys_ctx = ys_lat if last else (cy_a, _pool(ub, n_lat // lc, lc, bsz, w_pool, pool_scale, l), cy_c)

        merged = _merge(ys_lat, ys_ctx, ub, gate_col0, w_branch, l, rt, tn)
        x_mid = _out_proj(merged, w_out, l, x_lat, x_ctx, mod, ro, tn_out)

        h2, aff = _norm2(x_mid, g_norm2[l], mod, l, rn, w_router_p, ne)
        afft, slot = _route(aff, 0, n, bsz, cap)
        xg, gate = _gather(slot, afft, h2, 0, n, bsz, cap)
        xgs, gates = [xg], [gate]
        if not last:
            afft_c, slot_c = _route(aff, n_lat, lc, bsz, cap_c)
            xg_c, gate_c = _gather(slot_c, afft_c, h2, n_lat // lc, lc, bsz, cap_c)
            xgs, gates = xgs + [xg_c], gates + [gate_c]
        ys = _expert_ffn(xgs, gates, w_gate_e, w_up_e, w_down_e, l)
        x_lat = _scatter(slot, ys[0], x_mid, mod, l, lambda b: b, 0, n, bsz, cap, tn)
        if not last:
            x_ctx = _scatter(slot_c, ys[1], x_mid, mod, l, lambda b: bsz, n_lat, lc, bsz, cap_c, tn)

    return _final_norm(x_lat, g_final, tm_norm).reshape(bsz, n, d)
```

```python
import functools
import math

import numpy as np
import jax
import jax.numpy as jnp
from jax import lax
from jax.experimental import pallas as pl
from jax.experimental.pallas import tpu as pltpu

F32 = jnp.float32
BF16 = jnp.bfloat16
EPS = 1e-6
LANES = 128
HEAD_DIM = 128
CHUNK = 64
HGRN_GROUP = 4
GRID_W = 64
POOL_WINDOWS = (2, 4, 8, 16)
POOL_PAD = 16
NA_ROWS_MAX = 8
NA_COLS = 16
NA_QROWS = 4
NA_KROWS = 12
ROPE_THETA = 10000.0
EC_CAPACITY = 2
GATHER_ROWS = 512
FFN_ROWS = 512
NEG = -1e30
ROW_TILE = 256
NORM_ROWS = 512
LOG2E = 1.4426950408889634
VMEM_LIMIT = 56 * 2 ** 20


def _params(*sem):
    return pltpu.CompilerParams(dimension_semantics=sem, vmem_limit_bytes=VMEM_LIMIT)


def _sigmoid(x):
    return 1.0 / (1.0 + jnp.exp(-x))


def _neg_abs(x):
    bits = lax.bitcast_convert_type(x, jnp.uint32) | jnp.uint32(0x80000000)
    return lax.bitcast_convert_type(bits, F32)


def _lane_tile(n, limit):
    t = min(limit, n) // LANES * LANES
    while n % t:
        t -= LANES
    return t


def _row_tile(n, ctx_rows):
    tm = 1024
    while n % tm or ctx_rows % tm:
        tm //= 2
    return tm


class _Rows:
    def __init__(self, tm, n, bsz, n_lat, rows):
        self.tm, self.n, self.bsz = tm, n, bsz
        self.lat_tiles = n_lat // tm
        self.tiles = rows // tm
        self.ctx_tiles = max(self.tiles - self.lat_tiles, 1)

    def lat(self, i):
        return jnp.minimum(i, self.lat_tiles - 1)

    def ctx(self, i):
        return jnp.clip(i - self.lat_tiles, 0, self.ctx_tiles - 1)

    def mod_row(self, i):
        return jnp.minimum(i * self.tm // self.n, self.bsz)


def _pick(is_lat, lat_ref, ctx_ref):
    return jnp.where(is_lat, lat_ref[...], ctx_ref[...])


def _mod_kernel(c_ref, w_ref, b_ref, o_ref):
    c = c_ref[...]
    sc = (c * _sigmoid(c)).astype(BF16)
    o_ref[...] = jnp.dot(sc, w_ref[...].astype(BF16), preferred_element_type=F32) + b_ref[...]


def _modulation(c8, w_mod, b_mod):
    depth, d, n6 = w_mod.shape
    tn = _lane_tile(n6, 1024)
    return pl.pallas_call(
        _mod_kernel,
        grid=(depth, n6 // tn),
        in_specs=[pl.BlockSpec((8, d), lambda l, j: (0, 0)),
                  pl.BlockSpec((None, d, tn), lambda l, j: (l, 0, j)),
                  pl.BlockSpec((None, 1, tn), lambda l, j: (l, 0, j))],
        out_specs=pl.BlockSpec((None, 8, tn), lambda l, j: (l, 0, j)),
        out_shape=jax.ShapeDtypeStruct((depth, 8, n6), F32),
        compiler_params=_params("arbitrary", "arbitrary"),
        name="modulation",
    )(c8, w_mod, b_mod.reshape(depth, 1, n6))


def _rmsnorm(x, gain):
    return x * lax.rsqrt(jnp.mean(x * x, axis=-1, keepdims=True) + EPS) * gain


def _modulate(y, mod_ref, shift_idx):
    return y * (1.0 + mod_ref[shift_idx + 1:shift_idx + 2, :]) + mod_ref[shift_idx:shift_idx + 1, :]


def _norm1_kernel(xl_ref, xc_ref, g_ref, mod_ref, o_ref, *, lat_tiles):
    x = _pick(pl.program_id(0) < lat_tiles, xl_ref, xc_ref)
    o_ref[...] = _modulate(_rmsnorm(x, g_ref[...]), mod_ref, 0).astype(o_ref.dtype)


def _norm2_kernel(x_ref, g_ref, mod_ref, wr_ref, o_ref, aff_ref):
    h = _modulate(_rmsnorm(x_ref[...], g_ref[...]), mod_ref, 3)
    o_ref[...] = h.astype(o_ref.dtype)
    logits = jnp.dot(h, wr_ref[...], precision=lax.Precision.HIGHEST, preferred_element_type=F32)
    logits = logits[:, :aff_ref.shape[1]]
    e = jnp.exp(logits - jnp.max(logits, axis=-1, keepdims=True))
    aff_ref[...] = e / jnp.sum(e, axis=-1, keepdims=True)


def _final_norm_kernel(x_ref, g_ref, o_ref):
    o_ref[...] = _rmsnorm(x_ref[...], g_ref[...])


def _norm1(x_lat, x_ctx, gain, mod, layer, rt):
    d = x_lat.shape[1]
    return pl.pallas_call(
        functools.partial(_norm1_kernel, lat_tiles=rt.lat_tiles),
        grid=(rt.tiles,),
        in_specs=[pl.BlockSpec((rt.tm, d), lambda i: (rt.lat(i), 0)),
                  pl.BlockSpec((rt.tm, d), lambda i: (rt.ctx(i), 0)),
                  pl.BlockSpec((1, d), lambda i: (0, 0)),
                  pl.BlockSpec((None, None, 6, d), lambda i: (layer, rt.mod_row(i), 0, 0))],
        out_specs=pl.BlockSpec((rt.tm, d), lambda i: (i, 0)),
        out_shape=jax.ShapeDtypeStruct((rt.tiles * rt.tm, d), BF16),
        compiler_params=_params("arbitrary"), name="norm1",
    )(x_lat, x_ctx, gain.reshape(1, d), mod)


def _norm2(x, gain, mod, layer, rt, w_router_padded, ne):
    d = x.shape[1]
    rows = rt.tiles * rt.tm
    return pl.pallas_call(
        _norm2_kernel,
        grid=(rt.tiles,),
        in_specs=[pl.BlockSpec((rt.tm, d), lambda i: (i, 0)),
                  pl.BlockSpec((1, d), lambda i: (0, 0)),
                  pl.BlockSpec((None, None, 6, d), lambda i: (layer, rt.mod_row(i), 0, 0)),
                  pl.BlockSpec((None, d, w_router_padded.shape[2]), lambda i: (layer, 0, 0))],
        out_specs=[pl.BlockSpec((rt.tm, d), lambda i: (i, 0)), pl.BlockSpec((rt.tm, ne), lambda i: (i, 0))],
        out_shape=[jax.ShapeDtypeStruct((rows, d), BF16), jax.ShapeDtypeStruct((rows, ne), F32)],
        compiler_params=_params("arbitrary"), name="norm2_router",
    )(x, gain.reshape(1, d), mod, w_router_padded)


def _final_norm(x, gain, tm):
    rows, d = x.shape
    return pl.pallas_call(
        _final_norm_kernel,
        grid=(rows // tm,),
        in_specs=[pl.BlockSpec((tm, d), lambda i: (i, 0)), pl.BlockSpec((1, d), lambda i: (0, 0))],
        out_specs=pl.BlockSpec((tm, d), lambda i: (i, 0)),
        out_shape=jax.ShapeDtypeStruct((rows, d), F32),
        compiler_params=_params("arbitrary"), name="final_norm",
    )(x, gain.reshape(1, d))


def _mm_kernel(a_ref, w_ref, o_ref, wbf_ref):
    @pl.when(pl.program_id(1) == 0)
    def _():
        wbf_ref[...] = w_ref[...].astype(BF16)
    o_ref[...] = jnp.dot(a_ref[...], wbf_ref[...], preferred_element_type=F32).astype(o_ref.dtype)


def _matmul(a, w_all, layer, col0, ncols, out_dtype, tm, tn):
    m, k = a.shape
    off = col0 // tn
    return pl.pallas_call(
        _mm_kernel,
        grid=(ncols // tn, m // tm),
        in_specs=[pl.BlockSpec((tm, k), lambda j, i: (i, 0)),
                  pl.BlockSpec((None, k, tn), lambda j, i: (layer, 0, j + off))],
        out_specs=pl.BlockSpec((tm, tn), lambda j, i: (i, j)),
        out_shape=jax.ShapeDtypeStruct((m, ncols), out_dtype),
        scratch_shapes=[pltpu.VMEM((k, tn), BF16)],
        compiler_params=_params("arbitrary", "arbitrary"), name="matmul",
    )(a, w_all)


def _out_proj_kernel(a_ref, w_ref, xl_ref, xc_ref, mod_ref, o_ref, wbf_ref, *, lat_tiles):
    @pl.when(pl.program_id(1) == 0)
    def _():
        wbf_ref[...] = w_ref[...].astype(BF16)
    acc = jnp.dot(a_ref[...], wbf_ref[...], preferred_element_type=F32)
    x = _pick(pl.program_id(1) < lat_tiles, xl_ref, xc_ref)
    o_ref[...] = x + mod_ref[2:3, :] * acc


def _out_proj(a, w_all, layer, x_lat, x_ctx, mod, rt, tn):
    k = a.shape[1]
    d = w_all.shape[2]
    tm = rt.tm
    return pl.pallas_call(
        functools.partial(_out_proj_kernel, lat_tiles=rt.lat_tiles),
        grid=(d // tn, rt.tiles),
        in_specs=[pl.BlockSpec((tm, k), lambda j, i: (i, 0)),
                  pl.BlockSpec((None, k, tn), lambda j, i: (layer, 0, j)),
                  pl.BlockSpec((tm, tn), lambda j, i: (rt.lat(i), j)),
                  pl.BlockSpec((tm, tn), lambda j, i: (rt.ctx(i), j)),
                  pl.BlockSpec((None, None, 6, tn), lambda j, i: (layer, rt.mod_row(i), 0, j))],
        out_specs=pl.BlockSpec((tm, tn), lambda j, i: (i, j)),
        out_shape=jax.ShapeDtypeStruct((rt.tiles * tm, d), F32),
        scratch_shapes=[pltpu.VMEM((k, tn), BF16)],
        compiler_params=_params("arbitrary", "arbitrary"), name="out_proj",
    )(a, w_all, x_lat, x_ctx, mod)


def _merge_kernel(ya_ref, yb_ref, yc_ref, cya_ref, cyb_ref, cyc_ref, g0_ref, g1_ref, g2_ref, w_ref, o_ref, wbf_ref,
                  *, lat_tiles):
    @pl.when(pl.program_id(1) == 0)
    def _():
        wbf_ref[...] = w_ref[...].astype(BF16)
    is_lat = pl.program_id(1) < lat_tiles
    acc = None
    for j, (y_ref, cy_ref, g_ref) in enumerate(((ya_ref, cya_ref, g0_ref), (yb_ref, cyb_ref, g1_ref),
                                                (yc_ref, cyc_ref, g2_ref))):
        p = jnp.dot(_pick(is_lat, y_ref, cy_ref), wbf_ref[j], preferred_element_type=F32)
        t = _sigmoid(g_ref[...].astype(F32)) * p
        acc = t if acc is None else acc + t
    o_ref[...] = acc.astype(o_ref.dtype)


def _merge(ys_lat, ys_ctx, ub, gate_col0, w_branch, layer, rt, tn):
    bw = ys_lat[0].shape[1]
    d = w_branch.shape[3]
    tm = rt.tm
    lat_spec = pl.BlockSpec((tm, bw), lambda j, i: (rt.lat(i), 0))
    ctx_spec = pl.BlockSpec((tm, bw), lambda j, i: (rt.ctx(i), 0))

    def gate_spec(k):
        off = (gate_col0 + k * d) // tn
        return pl.BlockSpec((tm, tn), lambda j, i: (i, off + j))

    return pl.pallas_call(
        functools.partial(_merge_kernel, lat_tiles=rt.lat_tiles),
        grid=(d // tn, rt.tiles),
        in_specs=[lat_spec] * 3 + [ctx_spec] * 3 + [gate_spec(0), gate_spec(1), gate_spec(2),
                  pl.BlockSpec((None, 3, bw, tn), lambda j, i: (layer, 0, 0, j))],
        out_specs=pl.BlockSpec((tm, tn), lambda j, i: (i, j)),
        out_shape=jax.ShapeDtypeStruct((rt.tiles * tm, d), BF16),
        scratch_shapes=[pltpu.VMEM((3, bw, tn), BF16)],
        compiler_params=_params("arbitrary", "arbitrary"), name="merge",
    )(*ys_lat, *ys_ctx, ub, ub, ub, w_branch)


_HGRN_BLOCKS = (32, 16, 8, 4, 2, 1)


def _hgrn_constants():
    c = CHUNK
    t = np.arange(c)
    tri_f = (t[:, None] >= t[None, :]).astype(np.float32)
    tri_b = np.ascontiguousarray(tri_f[::-1, ::-1])
    sel_f, sel_b, masks, upper = [tri_f], [tri_b], [], []
    for m in _HGRN_BLOCKS:
        blk = (t // (2 * m)) * (2 * m)
        up = (t % (2 * m)) >= m
        sel_f.append(tri_f - tri_f[blk + m - 1])
        sel_b.append(tri_b - tri_b[blk + m])
        same = (t[:, None] // (2 * m)) == (t[None, :] // (2 * m))
        masks.append(same & (up[:, None] != up[None, :]))
        upper.append(np.broadcast_to(up[:, None], (c, LANES)))
    masks.append(np.eye(c, dtype=bool))
    sel = np.stack([np.concatenate(sel_f, axis=0), np.concatenate(sel_b, axis=0)])
    return (np.concatenate([sel, sel, sel], axis=2), np.stack(masks).astype(np.float32),
            np.stack(upper).astype(np.float32))


def _hgrn_log2_gate(z, lbp):
    zl = z * LOG2E
    log_sig = jnp.minimum(zl, 0.0) - jnp.log2(1.0 + jnp.exp2(_neg_abs(zl)))
    a = lbp[0:1, :]
    cc = lbp[1:2, :] + log_sig
    return jnp.maximum(a, cc) + jnp.log2(1.0 + jnp.exp2(_neg_abs(a - cc)))


def _split3(x):
    g1 = x.astype(BF16)
    r1 = x - g1.astype(F32)
    g2 = r1.astype(BF16)
    g3 = (r1 - g2.astype(F32)).astype(BF16)
    return jnp.concatenate([g1, g2, g3], axis=0)


_NT = (((1,), (1,)), ((), ()))
_TN = (((0,), (0,)), ((), ()))


def _hgrn_kernel(q_ref, ff_ref, fb_ref, i_ref, g_ref, lbf_ref, lbb_ref, gain_ref, cm_ref, mask_ref, up_ref,
                 sf0_ref, sb0_ref, y_ref, sf_ref, sb_ref, o_scr, qt_scr, upd_scr, dec_scr, st_scr, sums_scr, k_scr,
                 *, nc):
    z_refs = (ff_ref, fb_ref)
    lb_refs = (lbf_ref, lbb_ref)
    nl = len(_HGRN_BLOCKS)
    last_row = (CHUNK - 1, 0)

    groups = nc // HGRN_GROUP
    srows = (1 + nl) * CHUNK

    def gate_pass(it, carry):
        rows = [pl.ds(pl.multiple_of((it * HGRN_GROUP + s) * CHUNK, CHUNK), CHUNK) for s in range(HGRN_GROUP)]
        for d in range(2):
            logf = [_hgrn_log2_gate(z_refs[d][r, :], lb_refs[d][...]) for r in rows]
            sums_scr[d, pl.ds(pl.multiple_of(it * srows, srows), srows), :] = jnp.dot(
                cm_ref[d], jnp.concatenate([_split3(lf) for lf in logf], axis=1), preferred_element_type=F32)
            for r, lf in zip(rows, logf):
                k_scr[d, r, :] = 1.0 - jnp.exp2(lf)
        return carry

    lax.fori_loop(0, groups, gate_pass, 0, unroll=min(4, groups))

    def local_pass(it, carry):
        sums = [sums_scr[d, pl.ds(pl.multiple_of(it * srows, srows), srows), :] for d in range(2)]
        rows = [pl.ds(pl.multiple_of((it * HGRN_GROUP + s) * CHUNK, CHUNK), CHUNK) for s in range(HGRN_GROUP)]
        up = up_ref[...] > 0.5
        for s, r in enumerate(rows):
            lanes = slice(s * LANES, (s + 1) * LANES)
            qr = q_ref[r, :]
            q = qr * _sigmoid(qr)
            vb = i_ref[r, :].astype(BF16)
            b = [sums[d][0:CHUNK, lanes] for d in range(2)]
            lvl = [jnp.exp2(_neg_abs(sums[d][CHUNK:, lanes].reshape(nl, CHUNK, LANES))) for d in range(2)]
            k = [k_scr[d, r, :] for d in range(2)]
            q_dec = jnp.where(up, lvl[0], lvl[1])
            k_dec = jnp.where(up, k[1][None] * lvl[1], k[0][None] * lvl[0])
            qs = jnp.concatenate([q[None] * q_dec, q[None]], axis=0).astype(BF16)
            ks = jnp.concatenate([k_dec, (k[0] + k[1])[None]], axis=0).astype(BF16)
            sc = jnp.einsum("ltk,lsk->lts", qs, ks, preferred_element_type=F32)
            amat = jnp.sum(sc * mask_ref[...], axis=0)
            o_scr[r, :] = jnp.dot(amat.astype(BF16), vb, preferred_element_type=F32)
            tot = [b[d][last_row[d]:last_row[d] + 1, :] for d in range(2)]
            kd = jnp.concatenate([(k[d] * jnp.exp2(tot[d] - b[d])).astype(BF16) for d in range(2)], axis=1)
            upd = lax.dot_general(vb, kd, _TN, preferred_element_type=F32)
            j = it * HGRN_GROUP + s
            for d in range(2):
                qt_scr[d, r, :] = (q * jnp.exp2(b[d])).astype(BF16)
                upd_scr[d, j] = upd[:, d * LANES:(d + 1) * LANES]
                dec_scr[d, pl.ds(j, 1), :] = jnp.exp2(tot[d])
        return carry

    lax.fori_loop(0, groups, local_pass, 0, unroll=min(2, groups))

    st_scr[0] = sf0_ref[...]
    st_scr[1] = sb0_ref[...]

    def state_pass(it, carry):
        for d in range(2):
            j = it if d == 0 else nc - 1 - it
            rows = pl.ds(pl.multiple_of(j * CHUNK, CHUNK), CHUNK)
            st = st_scr[d]
            o_scr[rows, :] += lax.dot_general(qt_scr[d, rows, :], st.astype(BF16), _NT, preferred_element_type=F32)
            st_scr[d] = st * dec_scr[d, pl.ds(j, 1), :] + upd_scr[d, j]
        return carry

    lax.fori_loop(0, nc, state_pass, 0, unroll=min(8, nc))
    sf_ref[...] = st_scr[0]
    sb_ref[...] = st_scr[1]
    o = o_scr[...]
    o = o * lax.rsqrt(jnp.mean(o * o, axis=-1, keepdims=True) + EPS) * gain_ref[...]
    g = g_ref[...]
    y_ref[...] = (o * (g * _sigmoid(g))).astype(y_ref.dtype)


def _hgrn(ua, row_block0, n, bsz, heads, lbp, gain, sf0, sb0, consts):
    nc = n // CHUNK
    assert nc % HGRN_GROUP == 0 and nc % 2 == 0

    def col(k):
        return pl.BlockSpec((n, LANES), lambda b, h: (row_block0 + b, k * heads + h))

    lb_spec = lambda d: pl.BlockSpec((None, None, 2, LANES), lambda b, h: (d, h, 0, 0))
    st_spec = pl.BlockSpec((None, None, LANES, LANES), lambda b, h: (b, h, 0, 0))
    st_shape = jax.ShapeDtypeStruct((bsz, heads, LANES, LANES), F32)
    const_specs = [pl.BlockSpec(a.shape, lambda b, h, nd=a.ndim: (0,) * nd) for a in consts]
    return pl.pallas_call(
        functools.partial(_hgrn_kernel, nc=nc),
        grid=(bsz, heads),
        in_specs=[col(0), col(1), col(2), col(3), col(4), lb_spec(0), lb_spec(1),
                  pl.BlockSpec((None, 1, LANES), lambda b, h: (h, 0, 0))] + const_specs + [st_spec, st_spec],
        out_specs=[pl.BlockSpec((n, LANES), lambda b, h: (b, h)), st_spec, st_spec],
        out_shape=[jax.ShapeDtypeStruct((bsz * n, heads * LANES), BF16), st_shape, st_shape],
        scratch_shapes=[pltpu.VMEM((n, LANES), F32), pltpu.VMEM((2, n, LANES), BF16),
                        pltpu.VMEM((2, nc, LANES, LANES), F32), pltpu.VMEM((2, nc, LANES), F32),
                        pltpu.VMEM((2, LANES, LANES), F32),
                        pltpu.VMEM((2, nc * (1 + len(_HGRN_BLOCKS)) * CHUNK // HGRN_GROUP, HGRN_GROUP * LANES), F32),
                        pltpu.VMEM((2, n, LANES), F32)],
        compiler_params=_params("arbitrary", "arbitrary"), name="hgrn2",
    )(ua, ua, ua, ua, ua, lbp, lbp, gain, *consts, sf0, sb0)


def _pool_kernel(u_ref, w_ref, s_ref, y_ref, pad_ref, *, n, group):
    pos = lax.broadcasted_iota(jnp.int32, (n, 1), 0)
    zeros = jnp.zeros((POOL_PAD, pad_ref.shape[1]), F32)
    pad_ref[0:POOL_PAD, :] = zeros
    pad_ref[POOL_PAD + n:2 * POOL_PAD + n, :] = zeros
    pad_ref[POOL_PAD:POOL_PAD + n, :] = u_ref[...].astype(F32)
    for gi, w in enumerate(POOL_WINDOWS):
        cols = slice(gi * group, (gi + 1) * group)
        acc = None
        for dlt in range(-(w // 2), w // 2):
            t = pad_ref[POOL_PAD + dlt:POOL_PAD + dlt + n, cols]
            acc = t if acc is None else acc + t
        lo = jnp.maximum(pos - w // 2, 0)
        hi = jnp.minimum(pos + w // 2 - 1, n - 1)
        cnt = (hi - lo + 1).astype(F32)
        dd = acc / cnt - pad_ref[POOL_PAD:POOL_PAD + n, cols]
        y = jnp.dot(dd.astype(BF16), w_ref[gi].astype(BF16), preferred_element_type=F32)
        y_ref[:, cols] = (y * s_ref[:, cols]).astype(y_ref.dtype)


def _pool(ub, row_block0, n, bsz, w_pool, scale, layer):
    group = w_pool.shape[-1]
    width = 4 * group
    return pl.pallas_call(
        functools.partial(_pool_kernel, n=n, group=group),
        grid=(bsz,),
        in_specs=[pl.BlockSpec((n, width), lambda b: (row_block0 + b, 0)),
                  pl.BlockSpec((None, 4, group, group), lambda b: (layer, 0, 0, 0)),
                  pl.BlockSpec((None, 1, width), lambda b: (layer, 0, 0))],
        out_specs=pl.BlockSpec((n, width), lambda b: (b, 0)),
        out_shape=jax.ShapeDtypeStruct((bsz * n, width), BF16),
        scratch_shapes=[pltpu.VMEM((n + 2 * POOL_PAD, width), F32)],
        compiler_params=_params("arbitrary"), name="pool",
    )(ub, w_pool, scale.reshape(scale.shape[0], 1, width))


def _na_tables(n):
    pos = np.arange(n)
    half = HEAD_DIM // 2
    inv_freq = ROPE_THETA ** (-np.arange(0, half, 2, dtype=np.float64) / half)
    lane = np.arange(HEAD_DIM)
    p = np.where(lane[None, :] < half, (pos // GRID_W)[:, None], (pos % GRID_W)[:, None]).astype(np.float64)
    ang = p * inv_freq[lane % (half // 2)][None, :]
    sign = np.where((lane % half) < half // 2, -1.0, 1.0)[None, :]
    return np.cos(ang).astype(np.float32), (np.sin(ang) * sign).astype(np.float32)


def _na_block_layout(rows):
    kr = NA_ROWS_MAX
    nblk = rows // NA_QROWS
    starts, patterns, types = [], [], []
    for j in range(nblk):
        u = int(np.clip(NA_QROWS * j - kr // 2, 0, rows - NA_KROWS))
        r = NA_QROWS * j + np.arange(NA_QROWS)
        start_r = np.clip(r - kr // 2, 0, rows - kr)
        kabs = u + np.arange(NA_KROWS)
        valid = (kabs[None, :] >= start_r[:, None]) & (kabs[None, :] < start_r[:, None] + kr)
        assert valid.sum(axis=1).min() == kr
        dr = np.clip(kabs[None, :] - r[:, None] + NA_ROWS_MAX - 1, 0, 2 * NA_ROWS_MAX - 2)
        key = (valid.tobytes(), dr.tobytes())
        keys = [p[0] for p in patterns]
        if key not in keys:
            patterns.append((key, valid, dr))
        types.append([p[0] for p in patterns].index(key))
        starts.append(u)
    return starts, types, [(p[1], p[2]) for p in patterns]


NA_DR = 2 * NA_ROWS_MAX - 1


def _na_bias_table(rpb):
    qcol = np.arange(GRID_W)
    col_start = np.clip(qcol - NA_COLS // 2, 0, GRID_W - NA_COLS)
    kcol = np.arange(GRID_W)
    col_mask = (kcol[None, :] >= col_start[:, None]) & (kcol[None, :] < col_start[:, None] + NA_COLS)
    dc = np.clip(kcol[None, :] - qcol[:, None] + NA_COLS - 1, 0, 2 * NA_COLS - 2)
    col_hot = (dc[None] == np.arange(2 * NA_COLS - 1)[:, None, None]).astype(np.float32)
    by_col = jnp.einsum("lhab,bqk->lhaqk", rpb.astype(F32), col_hot, precision=lax.Precision.HIGHEST)
    by_col = jnp.where(col_mask, by_col * LOG2E, NEG)
    masked = jnp.full(by_col.shape[:2] + (1, GRID_W, GRID_W), NEG, F32)
    table = jnp.concatenate([by_col, masked], axis=2)
    return jnp.concatenate([table, table], axis=-1)


def _na_pieces(patterns):
    out = []
    for valid, dr in patterns:
        idx = np.where(valid, dr, NA_DR)
        out.append(tuple(tuple((int(idx[r, 2 * p]), int(idx[r, 2 * p + 1])) for p in range(NA_KROWS // 2))
                         for r in range(NA_QROWS)))
    return tuple(out)


def _softmax_pv(s_list, v_list):
    m = None
    for s in s_list:
        mm = jnp.max(s, axis=-1, keepdims=True)
        m = mm if m is None else jnp.maximum(m, mm)
    num, den = None, None
    for s, v in zip(s_list, v_list):
        p = jnp.exp2(s - m)
        ssum = jnp.sum(p, axis=-1, keepdims=True)
        o = jnp.dot(p.astype(BF16), v, preferred_element_type=F32)
        num = o if num is None else num + o
        den = ssum if den is None else den + ssum
    return num / den


def _na_kernel(q_ref, k_ref, v_ref, cq_ref, ck_ref, cv_ref, tab_ref, cos_ref, sin_ref, y_ref, cy_ref,
               qs_ref, ks_ref, bias_ref, *, starts, types, pieces):
    lane = lax.broadcasted_iota(jnp.int32, (1, HEAD_DIM), 1)
    first = (lane % (HEAD_DIM // 2)) < HEAD_DIM // 4
    scale = HEAD_DIM ** -0.5 * LOG2E

    @pl.when(pl.program_id(1) == 0)
    def _():
        even = lane < GRID_W
        for tp, by_row in enumerate(pieces):
            for r, by_pair in enumerate(by_row):
                for p, (ie, io) in enumerate(by_pair):
                    bias_ref[tp, r * GRID_W:(r + 1) * GRID_W, 2 * p * GRID_W:2 * (p + 1) * GRID_W] = (
                        jnp.where(even, tab_ref[ie], tab_ref[io]))

    def rope(t):
        partner = jnp.where(first, pltpu.roll(t, HEAD_DIM - HEAD_DIM // 4, axis=1),
                            pltpu.roll(t, HEAD_DIM // 4, axis=1))
        return t * cos_ref[...] + partner * sin_ref[...]

    qs_ref[...] = (rope(q_ref[...].astype(F32)) * scale).astype(BF16)
    ks_ref[...] = rope(k_ref[...].astype(F32)).astype(BF16)
    ck = ck_ref[...]
    cv = cv_ref[...]
    qrows = NA_QROWS * GRID_W
    krows = NA_KROWS * GRID_W
    for j, (u, tp) in enumerate(zip(starts, types)):
        qb = qs_ref[j * qrows:(j + 1) * qrows, :]
        kb = ks_ref[u * GRID_W:u * GRID_W + krows, :]
        vb = v_ref[u * GRID_W:u * GRID_W + krows, :]
        s_loc = lax.dot_general(qb, kb, _NT, preferred_element_type=F32) + bias_ref[tp]
        s_ctx = lax.dot_general(qb, ck, _NT, preferred_element_type=F32)
        y_ref[j * qrows:(j + 1) * qrows, :] = _softmax_pv([s_loc, s_ctx], [vb, cv]).astype(y_ref.dtype)
    s = lax.dot_general(cq_ref[...], ck, _NT, preferred_element_type=F32) * scale
    cy_ref[...] = _softmax_pv([s], [cv]).astype(cy_ref.dtype)


def _attention(ub, col0, n, lc, bsz, heads, table, layer, layout):
    starts, types, patterns = layout
    cos, sin = _na_tables(n)
    cb = col0 // HEAD_DIM
    ctx0 = bsz * n // lc

    def lat(k):
        return pl.BlockSpec((n, HEAD_DIM), lambda h, b: (b, cb + k * heads + h))

    def ctx(k):
        return pl.BlockSpec((lc, HEAD_DIM), lambda h, b: (ctx0 + b, cb + k * heads + h))

    tab = pl.BlockSpec((n, HEAD_DIM), lambda h, b: (0, 0))
    return pl.pallas_call(
        functools.partial(_na_kernel, starts=tuple(starts), types=tuple(types), pieces=_na_pieces(patterns)),
        grid=(heads, bsz),
        in_specs=[lat(0), lat(1), lat(2), ctx(0), ctx(1), ctx(2),
                  pl.BlockSpec((None, None) + table.shape[2:], lambda h, b: (layer, h, 0, 0, 0)), tab, tab],
        out_specs=[pl.BlockSpec((n, HEAD_DIM), lambda h, b: (b, h)),
                   pl.BlockSpec((lc, HEAD_DIM), lambda h, b: (b, h))],
        out_shape=[jax.ShapeDtypeStruct((bsz * n, heads * HEAD_DIM), BF16),
                   jax.ShapeDtypeStruct((bsz * lc, heads * HEAD_DIM), BF16)],
        scratch_shapes=[pltpu.VMEM((n, HEAD_DIM), BF16), pltpu.VMEM((n, HEAD_DIM), BF16),
                        pltpu.VMEM((len(patterns), NA_QROWS * GRID_W, NA_KROWS * GRID_W), F32)],
        compiler_params=_params("arbitrary", "arbitrary"), name="attention",
    )(ub, ub, ub, ub, ub, ub, table, jnp.asarray(cos), jnp.asarray(sin))


def _route_kernel(afft_ref, slot_ref, *, n, ne, cap):
    bits = lax.bitcast_convert_type(afft_ref[...], jnp.int32)

    def bisect(i, thr):
        cand = thr | jnp.left_shift(jnp.int32(1), 30 - i)
        cnt = jnp.sum((bits >= cand).astype(F32), axis=1, keepdims=True)
        return jnp.where(cnt >= cap, cand, thr)

    thr = lax.fori_loop(0, 31, bisect, jnp.zeros((ne, 1), jnp.int32))
    above = bits > thr
    tied = bits == thr
    need = cap - jnp.sum(above.astype(F32), axis=1, keepdims=True)

    row = lax.broadcasted_iota(jnp.int32, (LANES, LANES), 0)
    colm = lax.broadcasted_iota(jnp.int32, (LANES, LANES), 1)
    before = (row < colm).astype(BF16)
    ones = jnp.ones((LANES, LANES), BF16)

    def prefix(x):
        xb = x.astype(BF16)
        outs, carry = [], jnp.zeros((ne, LANES), F32)
        for blk in range(n // LANES):
            xs = xb[:, blk * LANES:(blk + 1) * LANES]
            outs.append(jnp.dot(xs, before, preferred_element_type=F32) + carry)
            carry = carry + jnp.dot(xs, ones, preferred_element_type=F32)
        return jnp.concatenate(outs, axis=1)

    chosen = above | (tied & (prefix(tied.astype(F32)) < need))
    slot = prefix(chosen.astype(F32)).astype(jnp.int32)
    slot_ref[...] = jnp.where(chosen, slot, n)


def _route(aff, row0, n, bsz, cap):
    ne = aff.shape[1]
    at = jnp.swapaxes(aff[row0:row0 + bsz * n].reshape(bsz, n, ne), 1, 2)
    spec = pl.BlockSpec((None, ne, n), lambda b: (b, 0, 0))
    slot = pl.pallas_call(
        functools.partial(_route_kernel, n=n, ne=ne, cap=cap),
        grid=(bsz,), in_specs=[spec], out_specs=spec,
        out_shape=jax.ShapeDtypeStruct((bsz, ne, n), jnp.int32),
        compiler_params=_params("arbitrary"), name="expert_route",
    )(at)
    return at, slot


def _gather_kernel(slot_ref, afft_ref, h_ref, xg_ref, g_ref, *, cap, eg):
    e0 = pl.program_id(1) * eg
    want = lax.broadcasted_iota(jnp.int32, (cap, 1), 0)
    sels = [slot_ref[pl.ds(e0 + k, 1), :] == want for k in range(eg)]
    sel = sels[0].astype(BF16) if eg == 1 else jnp.concatenate([s.astype(BF16) for s in sels], axis=0)
    xg = jnp.dot(sel, h_ref[...], preferred_element_type=F32).astype(xg_ref.dtype)
    xg_ref[...] = xg.reshape(xg_ref.shape)
    for k in range(eg):
        g_ref[k] = jnp.sum(jnp.where(sels[k], afft_ref[pl.ds(e0 + k, 1), :], 0.0), axis=1, keepdims=True)


def _gather(slot, afft, h2, row_block0, n, bsz, cap):
    ne = slot.shape[1]
    d = h2.shape[1]
    eg = max(1, min(ne, GATHER_ROWS // cap))
    while ne % eg:
        eg -= 1
    row_spec = pl.BlockSpec((None, ne, n), lambda b, e: (b, 0, 0))
    return pl.pallas_call(
        functools.partial(_gather_kernel, cap=cap, eg=eg),
        grid=(bsz, ne // eg),
        in_specs=[row_spec, row_spec, pl.BlockSpec((n, d), lambda b, e: (row_block0 + b, 0))],
        out_specs=[pl.BlockSpec((eg, cap, d), lambda b, e: (e, b, 0)),
                   pl.BlockSpec((eg, cap, 1), lambda b, e: (e, b, 0))],
        out_shape=[jax.ShapeDtypeStruct((ne, bsz * cap, d), BF16),
                   jax.ShapeDtypeStruct((ne, bsz * cap, 1), F32)],
        compiler_params=_params("arbitrary", "arbitrary"), name="expert_gather",
    )(slot, afft, h2)


def _ffn_kernel(*refs, ns, nf):
    x_refs = refs[0:2 * ns:2]
    g_refs = refs[1:2 * ns:2]
    wg_ref, wu_ref, wd_ref = refs[2 * ns:2 * ns + 3]
    o_refs = refs[2 * ns + 3:3 * ns + 3]
    wgb, wub, wdb = refs[3 * ns + 3:3 * ns + 6]
    acc_refs = refs[3 * ns + 6:]
    f = pl.program_id(1)
    wgb[...] = wg_ref[...].astype(BF16)
    wub[...] = wu_ref[...].astype(BF16)
    wdb[...] = wd_ref[...].astype(BF16)
    @pl.when(f == 0)
    def _():
        for acc in acc_refs:
            acc[...] = jnp.zeros(acc.shape, F32)

    for k in range(ns):
        rows = x_refs[k].shape[0]
        step = min(FFN_ROWS, rows)
        for r0 in range(0, rows, step):
            rs = slice(r0, r0 + step)
            x = x_refs[k][rs, :]
            hg = jnp.dot(x, wgb[...], preferred_element_type=F32)
            hu = jnp.dot(x, wub[...], preferred_element_type=F32)
            hid = (hg * _sigmoid(hg) * hu).astype(BF16)
            acc_refs[k][rs, :] += jnp.dot(hid, wdb[...], preferred_element_type=F32)

    @pl.when(f == nf - 1)
    def _():
        for k in range(ns):
            o_refs[k][...] = (acc_refs[k][...] * g_refs[k][...]).astype(o_refs[k].dtype)


def _expert_ffn(xgs, gates, w_gate, w_up, w_down, layer):
    ns = len(xgs)
    _, ne, d, ff = w_gate.shape
    tf = _lane_tile(ff, 256)
    nf = ff // tf
    in_specs, out_specs, out_shape, acc_scr, args = [], [], [], [], []
    for k in range(ns):
        rows = xgs[k].shape[1]
        in_specs.append(pl.BlockSpec((None, rows, d), lambda e, f: (e, 0, 0)))
        in_specs.append(pl.BlockSpec((None, rows, 1), lambda e, f: (e, 0, 0)))
        out_specs.append(pl.BlockSpec((None, rows, d), lambda e, f: (e, 0, 0)))
        out_shape.append(jax.ShapeDtypeStruct((ne, rows, d), BF16))
        acc_scr.append(pltpu.VMEM((rows, d), F32))
        args += [xgs[k], gates[k]]
    in_specs += [pl.BlockSpec((None, None, d, tf), lambda e, f: (layer, e, 0, f)),
                 pl.BlockSpec((None, None, d, tf), lambda e, f: (layer, e, 0, f)),
                 pl.BlockSpec((None, None, tf, d), lambda e, f: (layer, e, f, 0))]
    return pl.pallas_call(
        functools.partial(_ffn_kernel, ns=ns, nf=nf),
        grid=(ne, nf),
        in_specs=in_specs, out_specs=out_specs, out_shape=out_shape,
        scratch_shapes=[pltpu.VMEM((d, tf), BF16), pltpu.VMEM((d, tf), BF16), pltpu.VMEM((tf, d), BF16)] + acc_scr,
        compiler_params=_params("arbitrary", "arbitrary"), name="expert_ffn",
    )(*args, w_gate, w_up, w_down)


def _scatter_kernel(slot_ref, y_ref, x_ref, mod_ref, o_ref, *, ne, cap):
    sl = slot_ref[...]
    want = lax.broadcasted_iota(jnp.int32, (1, cap), 1)
    acc = None
    for e in range(ne):
        sel = (sl[:, e:e + 1] == want).astype(BF16)
        p = jnp.dot(sel, y_ref[e], preferred_element_type=F32)
        acc = p if acc is None else acc + p
    o_ref[...] = x_ref[...] + mod_ref[5:6, :] * acc


def _scatter(slot, y, x, mod, layer, mod_row0, row0, n, bsz, cap, tn):
    ne = slot.shape[1]
    d = x.shape[1]
    tm = min(512, n)
    slot_t = jnp.swapaxes(slot, 1, 2)
    xb0 = row0 // tm
    per_seq = n // tm
    return pl.pallas_call(
        functools.partial(_scatter_kernel, ne=ne, cap=cap),
        grid=(bsz, d // tn, per_seq),
        in_specs=[pl.BlockSpec((None, tm, ne), lambda b, j, i: (b, i, 0)),
                  pl.BlockSpec((ne, cap, tn), lambda b, j, i: (0, b, j)),
                  pl.BlockSpec((tm, tn), lambda b, j, i: (xb0 + b * per_seq + i, j)),
                  pl.BlockSpec((None, None, 6, tn), lambda b, j, i: (layer, mod_row0(b), 0, j))],
        out_specs=pl.BlockSpec((tm, tn), lambda b, j, i: (b * per_seq + i, j)),
        out_shape=jax.ShapeDtypeStruct((bsz * n, d), F32),
        compiler_params=_params("arbitrary", "arbitrary", "arbitrary"), name="expert_scatter",
    )(slot_t, y, x, mod)


def kernel(x, c, ctx, c_ctx, w_mod, b_mod, g_norm1, w_in, lb_param, g_hgrn, w_pool, pool_scale, rpb,
           w_branch, w_out, g_norm2, w_router, w_gate_e, w_up_e, w_down_e, g_final):
    bsz, n, d = x.shape
    lc = ctx.shape[1]
    depth = w_mod.shape[0]
    width = lb_param.shape[2]
    heads = width // HEAD_DIM
    ne = w_router.shape[2]
    n_lat = bsz * n
    n_ctx = bsz * lc
    total = n_lat + n_ctx
    assert bsz + 1 <= 8 and n % ROW_TILE == 0 and lc % ROW_TILE == 0 and n_lat % lc == 0
    assert w_pool.shape[-1] * 4 == width and w_branch.shape[2] == width
    tm = _row_tile(n, n_ctx)
    tn = min(512, d)

    lb_all = jnp.cumsum(jax.nn.softmax(lb_param.astype(F32), axis=1), axis=1)
    lb_all = lb_all - lb_all[:, :1]
    lbp_all = jnp.stack([jnp.log(lb_all), jnp.log1p(-lb_all)], axis=2) * LOG2E
    lbp_all = lbp_all.reshape(2, depth, 2, heads, LANES).transpose(0, 1, 3, 2, 4)

    c8 = jnp.concatenate([c, c_ctx[None], jnp.zeros((8 - bsz - 1, d), F32)], axis=0)
    mod = _modulation(c8, w_mod, b_mod).reshape(depth, 8, 6, d)

    cm3, pair_masks, upper = _hgrn_constants()
    hgrn_consts = (jnp.asarray(cm3, BF16), jnp.asarray(pair_masks, F32), jnp.asarray(upper, F32))
    na_layout = _na_block_layout(n // GRID_W)
    na_bias = _na_bias_table(rpb)
    lanes_e = -(-ne // LANES) * LANES
    w_router_p = jnp.pad(w_router, ((0, 0), (0, 0), (0, lanes_e - ne)))

    x_lat = x.reshape(n_lat, d)
    x_ctx = ctx.reshape(n_ctx, d)
    zero_state = jnp.zeros((bsz, heads, LANES, LANES), F32)
    cap = EC_CAPACITY * n // ne
    cap_c = EC_CAPACITY * lc // ne
    a_cols = 5 * width
    b_cols = w_in.shape[2] - a_cols
    tn_in = _lane_tile(math.gcd(a_cols, b_cols), 1024)
    gate_col0 = 4 * width
    rt_all = _Rows(tm, n, bsz, n_lat, total)
    rt_lat = _Rows(tm, n, bsz, n_lat, n_lat)
    tm_norm = min(NORM_ROWS, tm)
    rn_all = _Rows(tm_norm, n, bsz, n_lat, total)
    rn_lat = _Rows(tm_norm, n, bsz, n_lat, n_lat)
    tm_out, tn_out = max(tm // 2, ROW_TILE), _lane_tile(d, 1024)
    ro_all = _Rows(tm_out, n, bsz, n_lat, total)
    ro_lat = _Rows(tm_out, n, bsz, n_lat, n_lat)

    for l in range(depth):
        last = l == depth - 1
        rt, rn, ro = (rt_lat, rn_lat, ro_lat) if last else (rt_all, rn_all, ro_all)
        gain_h = g_hgrn[l].reshape(heads, 1, LANES)

        h = _norm1(x_lat, x_ctx, g_norm1[l], mod, l, rn_all)
        ua = _matmul(h, w_in, l, 0, a_cols, F32, tm, tn_in)
        ub = _matmul(h, w_in, l, a_cols, b_cols, BF16, tm, tn_in)

        cy_a, s_f, s_b = _hgrn(ua, n_lat // lc, lc, bsz, heads, lbp_all[:, l], gain_h, zero_state, zero_state,
                               hgrn_consts)
        y_a, _, _ = _hgrn(ua, 0, n, bsz, heads, lbp_all[:, l], gain_h, s_f, s_b, hgrn_consts)
        y_b = _pool(ub, 0, n, bsz, w_pool, pool_scale, l)
        y_c, cy_c = _attention(ub, width, n, lc, bsz, heads, na_bias, l, na_layout)
        ys_lat = (y_a, y_b, y_c)
        ys_ctx = ys_lat if last else (cy_a, _pool(ub, n_lat // lc, lc, bsz, w_pool, pool_scale, l), cy_c)

        merged = _merge(ys_lat, ys_ctx, ub, gate_col0, w_branch, l, rt, tn)
        x_mid = _out_proj(merged, w_out, l, x_lat, x_ctx, mod, ro, tn_out)

        h2, aff = _norm2(x_mid, g_norm2[l], mod, l, rn, w_router_p, ne)
        afft, slot = _route(aff, 0, n, bsz, cap)
        xg, gate = _gather(slot, afft, h2, 0, n, bsz, cap)
        xgs, gates = [xg], [gate]
        if not last:
            afft_c, slot_c = _route(aff, n_lat, lc, bsz, cap_c)
            xg_c, gate_c = _gather(slot_c, afft_c, h2, n_lat // lc, lc, bsz, cap_c)
            xgs, gates = xgs + [xg_c], gates + [gate_c]
        ys = _expert_ffn(xgs, gates, w_gate_e, w_up_e, w_down_e, l)
        x_lat = _scatter(slot, ys[0], x_mid, mod, l, lambda b: b, 0, n, bsz, cap, tn)
        if not last:
            x_ctx = _scatter(slot_c, ys[1], x_mid, mod, l, lambda b: bsz, n_lat, lc, bsz, cap_c, tn)

    return _final_norm(x_lat, g_final, tm_norm).reshape(bsz, n, d)
```

```python
import functools
import math

import numpy as np
import jax
import jax.numpy as jnp
from jax import lax
from jax.experimental import pallas as pl
from jax.experimental.pallas import tpu as pltpu

F32 = jnp.float32
BF16 = jnp.bfloat16
EPS = 1e-6
LANES = 128
HEAD_DIM = 128
CHUNK = 64
HGRN_GROUP = 4
GRID_W = 64
POOL_WINDOWS = (2, 4, 8, 16)
POOL_PAD = 16
NA_ROWS_MAX = 8
NA_COLS = 16
NA_QROWS = 4
NA_KROWS = 12
ROPE_THETA = 10000.0
EC_CAPACITY = 2
GATHER_ROWS = 512
FFN_ROWS = 512
NEG = -1e30
ROW_TILE = 256
NORM_ROWS = 512
LOG2E = 1.4426950408889634
VMEM_LIMIT = 56 * 2 ** 20


def _params(*sem):
    return pltpu.CompilerParams(dimension_semantics=sem, vmem_limit_bytes=VMEM_LIMIT)


def _sigmoid(x):
    return 1.0 / (1.0 + jnp.exp(-x))


def _neg_abs(x):
    bits = lax.bitcast_convert_type(x, jnp.uint32) | jnp.uint32(0x80000000)
    return lax.bitcast_convert_type(bits, F32)


def _lane_tile(n, limit):
    t = min(limit, n) // LANES * LANES
    while n % t:
        t -= LANES
    return t


def _row_tile(n, ctx_rows):
    tm = 1024
    while n % tm or ctx_rows % tm:
        tm //= 2
    return tm


class _Rows:
    def __init__(self, tm, n, bsz, n_lat, rows):
        self.tm, self.n, self.bsz = tm, n, bsz
        self.lat_tiles = n_lat // tm
        self.tiles = rows // tm
        self.ctx_tiles = max(self.tiles - self.lat_tiles, 1)

    def lat(self, i):
        return jnp.minimum(i, self.lat_tiles - 1)

    def ctx(self, i):
        return jnp.clip(i - self.lat_tiles, 0, self.ctx_tiles - 1)

    def mod_row(self, i):
        return jnp.minimum(i * self.tm // self.n, self.bsz)


def _pick(is_lat, lat_ref, ctx_ref):
    return jnp.where(is_lat, lat_ref[...], ctx_ref[...])


def _mod_kernel(c_ref, w_ref, b_ref, o_ref):
    c = c_ref[...]
    sc = (c * _sigmoid(c)).astype(BF16)
    o_ref[...] = jnp.dot(sc, w_ref[...].astype(BF16), preferred_element_type=F32) + b_ref[...]


def _modulation(c8, w_mod, b_mod):
    depth, d, n6 = w_mod.shape
    tn = _lane_tile(n6, 1024)
    return pl.pallas_call(
        _mod_kernel,
        grid=(depth, n6 // tn),
        in_specs=[pl.BlockSpec((8, d), lambda l, j: (0, 0)),
                  pl.BlockSpec((None, d, tn), lambda l, j: (l, 0, j)),
                  pl.BlockSpec((None, 1, tn), lambda l, j: (l, 0, j))],
        out_specs=pl.BlockSpec((None, 8, tn), lambda l, j: (l, 0, j)),
        out_shape=jax.ShapeDtypeStruct((depth, 8, n6), F32),
        compiler_params=_params("arbitrary", "arbitrary"),
        name="modulation",
    )(c8, w_mod, b_mod.reshape(depth, 1, n6))


def _rmsnorm(x, gain):
    return x * lax.rsqrt(jnp.mean(x * x, axis=-1, keepdims=True) + EPS) * gain


def _modulate(y, mod_ref, shift_idx):
    return y * (1.0 + mod_ref[shift_idx + 1:shift_idx + 2, :]) + mod_ref[shift_idx:shift_idx + 1, :]


def _norm1_kernel(xl_ref, xc_ref, g_ref, mod_ref, o_ref, *, lat_tiles):
    x = _pick(pl.program_id(0) < lat_tiles, xl_ref, xc_ref)
    o_ref[...] = _modulate(_rmsnorm(x, g_ref[...]), mod_ref, 0).astype(o_ref.dtype)


def _norm2_kernel(x_ref, g_ref, mod_ref, wr_ref, o_ref, aff_ref):
    h = _modulate(_rmsnorm(x_ref[...], g_ref[...]), mod_ref, 3)
    o_ref[...] = h.astype(o_ref.dtype)
    logits = jnp.dot(h, wr_ref[...], precision=lax.Precision.HIGHEST, preferred_element_type=F32)
    logits = logits[:, :aff_ref.shape[1]]
    e = jnp.exp(logits - jnp.max(logits, axis=-1, keepdims=True))
    aff_ref[...] = e / jnp.sum(e, axis=-1, keepdims=True)


def _final_norm_kernel(x_ref, g_ref, o_ref):
    o_ref[...] = _rmsnorm(x_ref[...], g_ref[...])


def _norm1(x_lat, x_ctx, gain, mod, layer, rt):
    d = x_lat.shape[1]
    return pl.pallas_call(
        functools.partial(_norm1_kernel, lat_tiles=rt.lat_tiles),
        grid=(rt.tiles,),
        in_specs=[pl.BlockSpec((rt.tm, d), lambda i: (rt.lat(i), 0)),
                  pl.BlockSpec((rt.tm, d), lambda i: (rt.ctx(i), 0)),
                  pl.BlockSpec((1, d), lambda i: (0, 0)),
                  pl.BlockSpec((None, None, 6, d), lambda i: (layer, rt.mod_row(i), 0, 0))],
        out_specs=pl.BlockSpec((rt.tm, d), lambda i: (i, 0)),
        out_shape=jax.ShapeDtypeStruct((rt.tiles * rt.tm, d), BF16),
        compiler_params=_params("arbitrary"), name="norm1",
    )(x_lat, x_ctx, gain.reshape(1, d), mod)


def _norm2(x, gain, mod, layer, rt, w_router_padded, ne):
    d = x.shape[1]
    rows = rt.tiles * rt.tm
    return pl.pallas_call(
        _norm2_kernel,
        grid=(rt.tiles,),
        in_specs=[pl.BlockSpec((rt.tm, d), lambda i: (i, 0)),
                  pl.BlockSpec((1, d), lambda i: (0, 0)),
                  pl.BlockSpec((None, None, 6, d), lambda i: (layer, rt.mod_row(i), 0, 0)),
                  pl.BlockSpec((None, d, w_router_padded.shape[2]), lambda i: (layer, 0, 0))],
        out_specs=[pl.BlockSpec((rt.tm, d), lambda i: (i, 0)), pl.BlockSpec((rt.tm, ne), lambda i: (i, 0))],
        out_shape=[jax.ShapeDtypeStruct((rows, d), BF16), jax.ShapeDtypeStruct((rows, ne), F32)],
        compiler_params=_params("arbitrary"), name="norm2_router",
    )(x, gain.reshape(1, d), mod, w_router_padded)


def _final_norm(x, gain, tm):
    rows, d = x.shape
    return pl.pallas_call(
        _final_norm_kernel,
        grid=(rows // tm,),
        in_specs=[pl.BlockSpec((tm, d), lambda i: (i, 0)), pl.BlockSpec((1, d), lambda i: (0, 0))],
        out_specs=pl.BlockSpec((tm, d), lambda i: (i, 0)),
        out_shape=jax.ShapeDtypeStruct((rows, d), F32),
        compiler_params=_params("arbitrary"), name="final_norm",
    )(x, gain.reshape(1, d))


def _mm_kernel(a_ref, w_ref, o_ref, wbf_ref):
    @pl.when(pl.program_id(1) == 0)
    def _():
        wbf_ref[...] = w_ref[...].astype(BF16)
    o_ref[...] = jnp.dot(a_ref[...], wbf_ref[...], preferred_element_type=F32).astype(o_ref.dtype)


def _matmul(a, w_all, layer, col0, ncols, out_dtype, tm, tn):
    m, k = a.shape
    off = col0 // tn
    return pl.pallas_call(
        _mm_kernel,
        grid=(ncols // tn, m // tm),
        in_specs=[pl.BlockSpec((tm, k), lambda j, i: (i, 0)),
                  pl.BlockSpec((None, k, tn), lambda j, i: (layer, 0, j + off))],
        out_specs=pl.BlockSpec((tm, tn), lambda j, i: (i, j)),
        out_shape=jax.ShapeDtypeStruct((m, ncols), out_dtype),
        scratch_shapes=[pltpu.VMEM((k, tn), BF16)],
        compiler_params=_params("arbitrary", "arbitrary"), name="matmul",
    )(a, w_all)


def _out_proj_kernel(a_ref, w_ref, xl_ref, xc_ref, mod_ref, o_ref, wbf_ref, *, lat_tiles):
    @pl.when(pl.program_id(1) == 0)
    def _():
        wbf_ref[...] = w_ref[...].astype(BF16)
    acc = jnp.dot(a_ref[...], wbf_ref[...], preferred_element_type=F32)
    x = _pick(pl.program_id(1) < lat_tiles, xl_ref, xc_ref)
    o_ref[...] = x + mod_ref[2:3, :] * acc


def _out_proj(a, w_all, layer, x_lat, x_ctx, mod, rt, tn):
    k = a.shape[1]
    d = w_all.shape[2]
    tm = rt.tm
    return pl.pallas_call(
        functools.partial(_out_proj_kernel, lat_tiles=rt.lat_tiles),
        grid=(d // tn, rt.tiles),
        in_specs=[pl.BlockSpec((tm, k), lambda j, i: (i, 0)),
                  pl.BlockSpec((None, k, tn), lambda j, i: (layer, 0, j)),
                  pl.BlockSpec((tm, tn), lambda j, i: (rt.lat(i), j)),
                  pl.BlockSpec((tm, tn), lambda j, i: (rt.ctx(i), j)),
                  pl.BlockSpec((None, None, 6, tn), lambda j, i: (layer, rt.mod_row(i), 0, j))],
        out_specs=pl.BlockSpec((tm, tn), lambda j, i: (i, j)),
        out_shape=jax.ShapeDtypeStruct((rt.tiles * tm, d), F32),
        scratch_shapes=[pltpu.VMEM((k, tn), BF16)],
        compiler_params=_params("arbitrary", "arbitrary"), name="out_proj",
    )(a, w_all, x_lat, x_ctx, mod)


def _merge_kernel(ya_ref, yb_ref, yc_ref, cya_ref, cyb_ref, cyc_ref, g0_ref, g1_ref, g2_ref, w_ref, o_ref, wbf_ref,
                  *, lat_tiles):
    @pl.when(pl.program_id(1) == 0)
    def _():
        wbf_ref[...] = w_ref[...].astype(BF16)
    is_lat = pl.program_id(1) < lat_tiles
    acc = None
    for j, (y_ref, cy_ref, g_ref) in enumerate(((ya_ref, cya_ref, g0_ref), (yb_ref, cyb_ref, g1_ref),
                                                (yc_ref, cyc_ref, g2_ref))):
        p = jnp.dot(_pick(is_lat, y_ref, cy_ref), wbf_ref[j], preferred_element_type=F32)
        t = _sigmoid(g_ref[...].astype(F32)) * p
        acc = t if acc is None else acc + t
    o_ref[...] = acc.astype(o_ref.dtype)


def _merge(ys_lat, ys_ctx, ub, gate_col0, w_branch, layer, rt, tn):
    bw = ys_lat[0].shape[1]
    d = w_branch.shape[3]
    tm = rt.tm
    lat_spec = pl.BlockSpec((tm, bw), lambda j, i: (rt.lat(i), 0))
    ctx_spec = pl.BlockSpec((tm, bw), lambda j, i: (rt.ctx(i), 0))

    def gate_spec(k):
        off = (gate_col0 + k * d) // tn
        return pl.BlockSpec((tm, tn), lambda j, i: (i, off + j))

    return pl.pallas_call(
        functools.partial(_merge_kernel, lat_tiles=rt.lat_tiles),
        grid=(d // tn, rt.tiles),
        in_specs=[lat_spec] * 3 + [ctx_spec] * 3 + [gate_spec(0), gate_spec(1), gate_spec(2),
                  pl.BlockSpec((None, 3, bw, tn), lambda j, i: (layer, 0, 0, j))],
        out_specs=pl.BlockSpec((tm, tn), lambda j, i: (i, j)),
        out_shape=jax.ShapeDtypeStruct((rt.tiles * tm, d), BF16),
        scratch_shapes=[pltpu.VMEM((3, bw, tn), BF16)],
        compiler_params=_params("arbitrary", "arbitrary"), name="merge",
    )(*ys_lat, *ys_ctx, ub, ub, ub, w_branch)


_HGRN_BLOCKS = (32, 16, 8, 4, 2, 1)


def _hgrn_constants():
    c = CHUNK
    t = np.arange(c)
    tri_f = (t[:, None] >= t[None, :]).astype(np.float32)
    tri_b = np.ascontiguousarray(tri_f[::-1, ::-1])
    sel_f, sel_b, masks, upper = [tri_f], [tri_b], [], []
    for m in _HGRN_BLOCKS:
        blk = (t // (2 * m)) * (2 * m)
        up = (t % (2 * m)) >= m
        sel_f.append(tri_f - tri_f[blk + m - 1])
        sel_b.append(tri_b - tri_b[blk + m])
        same = (t[:, None] // (2 * m)) == (t[None, :] // (2 * m))
        masks.append(same & (up[:, None] != up[None, :]))
        upper.append(np.broadcast_to(up[:, None], (c, LANES)))
    masks.append(np.eye(c, dtype=bool))
    sel = np.stack([np.concatenate(sel_f, axis=0), np.concatenate(sel_b, axis=0)])
    return (np.concatenate([sel, sel, sel], axis=2), np.stack(masks).astype(np.float32),
            np.stack(upper).astype(np.float32))


def _hgrn_log2_gate(z, lbp):
    zl = z * LOG2E
    log_sig = jnp.minimum(zl, 0.0) - jnp.log2(1.0 + jnp.exp2(_neg_abs(zl)))
    a = lbp[0:1, :]
    cc = lbp[1:2, :] + log_sig
    return jnp.maximum(a, cc) + jnp.log2(1.0 + jnp.exp2(_neg_abs(a - cc)))


def _split3(x):
    g1 = x.astype(BF16)
    r1 = x - g1.astype(F32)
    g2 = r1.astype(BF16)
    g3 = (r1 - g2.astype(F32)).astype(BF16)
    return jnp.concatenate([g1, g2, g3], axis=0)


_NT = (((1,), (1,)), ((), ()))
_TN = (((0,), (0,)), ((), ()))


def _hgrn_kernel(q_ref, ff_ref, fb_ref, i_ref, g_ref, lbf_ref, lbb_ref, gain_ref, cm_ref, mask_ref, up_ref,
                 sf0_ref, sb0_ref, y_ref, sf_ref, sb_ref, o_scr, qt_scr, upd_scr, dec_scr, st_scr, sums_scr, k_scr,
                 *, nc):
    z_refs = (ff_ref, fb_ref)
    lb_refs = (lbf_ref, lbb_ref)
    nl = len(_HGRN_BLOCKS)
    last_row = (CHUNK - 1, 0)

    groups = nc // HGRN_GROUP
    srows = (1 + nl) * CHUNK

    def gate_pass(it, carry):
        rows = [pl.ds(pl.multiple_of((it * HGRN_GROUP + s) * CHUNK, CHUNK), CHUNK) for s in range(HGRN_GROUP)]
        for d in range(2):
            logf = [_hgrn_log2_gate(z_refs[d][r, :], lb_refs[d][...]) for r in rows]
            sums_scr[d, pl.ds(pl.multiple_of(it * srows, srows), srows), :] = jnp.dot(
                cm_ref[d], jnp.concatenate([_split3(lf) for lf in logf], axis=1), preferred_element_type=F32)
            for r, lf in zip(rows, logf):
                k_scr[d, r, :] = 1.0 - jnp.exp2(lf)
        return carry

    lax.fori_loop(0, groups, gate_pass, 0, unroll=min(4, groups))

    def local_pass(it, carry):
        sums = [sums_scr[d, pl.ds(pl.multiple_of(it * srows, srows), srows), :] for d in range(2)]
        rows = [pl.ds(pl.multiple_of((it * HGRN_GROUP + s) * CHUNK, CHUNK), CHUNK) for s in range(HGRN_GROUP)]
        up = up_ref[...] > 0.5
        for s, r in enumerate(rows):
            lanes = slice(s * LANES, (s + 1) * LANES)
            qr = q_ref[r, :]
            q = qr * _sigmoid(qr)
            vb = i_ref[r, :].astype(BF16)
            b = [sums[d][0:CHUNK, lanes] for d in range(2)]
            lvl = [jnp.exp2(_neg_abs(sums[d][CHUNK:, lanes].reshape(nl, CHUNK, LANES))) for d in range(2)]
            k = [k_scr[d, r, :] for d in range(2)]
            q_dec = jnp.where(up, lvl[0], lvl[1])
            k_dec = jnp.where(up, k[1][None] * lvl[1], k[0][None] * lvl[0])
            qs = jnp.concatenate([q[None] * q_dec, q[None]], axis=0).astype(BF16)
            ks = jnp.concatenate([k_dec, (k[0] + k[1])[None]], axis=0).astype(BF16)
            sc = jnp.einsum("ltk,lsk->lts", qs, ks, preferred_element_type=F32)
            amat = jnp.sum(sc * mask_ref[...], axis=0)
            o_scr[r, :] = jnp.dot(amat.astype(BF16), vb, preferred_element_type=F32)
            tot = [b[d][last_row[d]:last_row[d] + 1, :] for d in range(2)]
            kd = jnp.concatenate([(k[d] * jnp.exp2(tot[d] - b[d])).astype(BF16) for d in range(2)], axis=1)
            upd = lax.dot_general(vb, kd, _TN, preferred_element_type=F32)
            j = it * HGRN_GROUP + s
            for d in range(2):
                qt_scr[d, r, :] = (q * jnp.exp2(b[d])).astype(BF16)
                upd_scr[d, j] = upd[:, d * LANES:(d + 1) * LANES]
                dec_scr[d, pl.ds(j, 1), :] = jnp.exp2(tot[d])
        return carry

    lax.fori_loop(0, groups, local_pass, 0, unroll=min(4, groups))

    st_scr[0] = sf0_ref[...]
    st_scr[1] = sb0_ref[...]

    def state_pass(it, carry):
        for d in range(2):
            j = it if d == 0 else nc - 1 - it
            rows = pl.ds(pl.multiple_of(j * CHUNK, CHUNK), CHUNK)
            st = st_scr[d]
            o_scr[rows, :] += lax.dot_general(qt_scr[d, rows, :], st.astype(BF16), _NT, preferred_element_type=F32)
            st_scr[d] = st * dec_scr[d, pl.ds(j, 1), :] + upd_scr[d, j]
        return carry

    lax.fori_loop(0, nc, state_pass, 0, unroll=min(8, nc))
    sf_ref[...] = st_scr[0]
    sb_ref[...] = st_scr[1]
    o = o_scr[...]
    o = o * lax.rsqrt(jnp.mean(o * o, axis=-1, keepdims=True) + EPS) * gain_ref[...]
    g = g_ref[...]
    y_ref[...] = (o * (g * _sigmoid(g))).astype(y_ref.dtype)


def _hgrn(ua, row_block0, n, bsz, heads, lbp, gain, sf0, sb0, consts):
    nc = n // CHUNK
    assert nc % HGRN_GROUP == 0 and nc % 2 == 0

    def col(k):
        return pl.BlockSpec((n, LANES), lambda b, h: (row_block0 + b, k * heads + h))

    lb_spec = lambda d: pl.BlockSpec((None, None, 2, LANES), lambda b, h: (d, h, 0, 0))
    st_spec = pl.BlockSpec((None, None, LANES, LANES), lambda b, h: (b, h, 0, 0))
    st_shape = jax.ShapeDtypeStruct((bsz, heads, LANES, LANES), F32)
    const_specs = [pl.BlockSpec(a.shape, lambda b, h, nd=a.ndim: (0,) * nd) for a in consts]
    return pl.pallas_call(
        functools.partial(_hgrn_kernel, nc=nc),
        grid=(bsz, heads),
        in_specs=[col(0), col(1), col(2), col(3), col(4), lb_spec(0), lb_spec(1),
                  pl.BlockSpec((None, 1, LANES), lambda b, h: (h, 0, 0))] + const_specs + [st_spec, st_spec],
        out_specs=[pl.BlockSpec((n, LANES), lambda b, h: (b, h)), st_spec, st_spec],
        out_shape=[jax.ShapeDtypeStruct((bsz * n, heads * LANES), BF16), st_shape, st_shape],
        scratch_shapes=[pltpu.VMEM((n, LANES), F32), pltpu.VMEM((2, n, LANES), BF16),
                        pltpu.VMEM((2, nc, LANES, LANES), F32), pltpu.VMEM((2, nc, LANES), F32),
                        pltpu.VMEM((2, LANES, LANES), F32),
                        pltpu.VMEM((2, nc * (1 + len(_HGRN_BLOCKS)) * CHUNK // HGRN_GROUP, HGRN_GROUP * LANES), F32),
                        pltpu.VMEM((2, n, LANES), F32)],
        compiler_params=_params("arbitrary", "arbitrary"), name="hgrn2",
    )(ua, ua, ua, ua, ua, lbp, lbp, gain, *consts, sf0, sb0)


def _pool_kernel(u_ref, w_ref, s_ref, y_ref, pad_ref, *, n, group):
    pos = lax.broadcasted_iota(jnp.int32, (n, 1), 0)
    zeros = jnp.zeros((POOL_PAD, pad_ref.shape[1]), F32)
    pad_ref[0:POOL_PAD, :] = zeros
    pad_ref[POOL_PAD + n:2 * POOL_PAD + n, :] = zeros
    pad_ref[POOL_PAD:POOL_PAD + n, :] = u_ref[...].astype(F32)
    for gi, w in enumerate(POOL_WINDOWS):
        cols = slice(gi * group, (gi + 1) * group)
        acc = None
        for dlt in range(-(w // 2), w // 2):
            t = pad_ref[POOL_PAD + dlt:POOL_PAD + dlt + n, cols]
            acc = t if acc is None else acc + t
        lo = jnp.maximum(pos - w // 2, 0)
        hi = jnp.minimum(pos + w // 2 - 1, n - 1)
        cnt = (hi - lo + 1).astype(F32)
        dd = acc / cnt - pad_ref[POOL_PAD:POOL_PAD + n, cols]
        y = jnp.dot(dd.astype(BF16), w_ref[gi].astype(BF16), preferred_element_type=F32)
        y_ref[:, cols] = (y * s_ref[:, cols]).astype(y_ref.dtype)


def _pool(ub, row_block0, n, bsz, w_pool, scale, layer):
    group = w_pool.shape[-1]
    width = 4 * group
    return pl.pallas_call(
        functools.partial(_pool_kernel, n=n, group=group),
        grid=(bsz,),
        in_specs=[pl.BlockSpec((n, width), lambda b: (row_block0 + b, 0)),
                  pl.BlockSpec((None, 4, group, group), lambda b: (layer, 0, 0, 0)),
                  pl.BlockSpec((None, 1, width), lambda b: (layer, 0, 0))],
        out_specs=pl.BlockSpec((n, width), lambda b: (b, 0)),
        out_shape=jax.ShapeDtypeStruct((bsz * n, width), BF16),
        scratch_shapes=[pltpu.VMEM((n + 2 * POOL_PAD, width), F32)],
        compiler_params=_params("arbitrary"), name="pool",
    )(ub, w_pool, scale.reshape(scale.shape[0], 1, width))


def _na_tables(n):
    pos = np.arange(n)
    half = HEAD_DIM // 2
    inv_freq = ROPE_THETA ** (-np.arange(0, half, 2, dtype=np.float64) / half)
    lane = np.arange(HEAD_DIM)
    p = np.where(lane[None, :] < half, (pos // GRID_W)[:, None], (pos % GRID_W)[:, None]).astype(np.float64)
    ang = p * inv_freq[lane % (half // 2)][None, :]
    sign = np.where((lane % half) < half // 2, -1.0, 1.0)[None, :]
    return np.cos(ang).astype(np.float32), (np.sin(ang) * sign).astype(np.float32)


def _na_block_layout(rows):
    kr = NA_ROWS_MAX
    nblk = rows // NA_QROWS
    starts, patterns, types = [], [], []
    for j in range(nblk):
        u = int(np.clip(NA_QROWS * j - kr // 2, 0, rows - NA_KROWS))
        r = NA_QROWS * j + np.arange(NA_QROWS)
        start_r = np.clip(r - kr // 2, 0, rows - kr)
        kabs = u + np.arange(NA_KROWS)
        valid = (kabs[None, :] >= start_r[:, None]) & (kabs[None, :] < start_r[:, None] + kr)
        assert valid.sum(axis=1).min() == kr
        dr = np.clip(kabs[None, :] - r[:, None] + NA_ROWS_MAX - 1, 0, 2 * NA_ROWS_MAX - 2)
        key = (valid.tobytes(), dr.tobytes())
        keys = [p[0] for p in patterns]
        if key not in keys:
            patterns.append((key, valid, dr))
        types.append([p[0] for p in patterns].index(key))
        starts.append(u)
    return starts, types, [(p[1], p[2]) for p in patterns]


NA_DR = 2 * NA_ROWS_MAX - 1


def _na_bias_table(rpb):
    qcol = np.arange(GRID_W)
    col_start = np.clip(qcol - NA_COLS // 2, 0, GRID_W - NA_COLS)
    kcol = np.arange(GRID_W)
    col_mask = (kcol[None, :] >= col_start[:, None]) & (kcol[None, :] < col_start[:, None] + NA_COLS)
    dc = np.clip(kcol[None, :] - qcol[:, None] + NA_COLS - 1, 0, 2 * NA_COLS - 2)
    col_hot = (dc[None] == np.arange(2 * NA_COLS - 1)[:, None, None]).astype(np.float32)
    by_col = jnp.einsum("lhab,bqk->lhaqk", rpb.astype(F32), col_hot, precision=lax.Precision.HIGHEST)
    by_col = jnp.where(col_mask, by_col * LOG2E, NEG)
    masked = jnp.full(by_col.shape[:2] + (1, GRID_W, GRID_W), NEG, F32)
    table = jnp.concatenate([by_col, masked], axis=2)
    return jnp.concatenate([table, table], axis=-1)


def _na_pieces(patterns):
    out = []
    for valid, dr in patterns:
        idx = np.where(valid, dr, NA_DR)
        out.append(tuple(tuple((int(idx[r, 2 * p]), int(idx[r, 2 * p + 1])) for p in range(NA_KROWS // 2))
                         for r in range(NA_QROWS)))
    return tuple(out)


def _softmax_pv(s_list, v_list):
    m = None
    for s in s_list:
        mm = jnp.max(s, axis=-1, keepdims=True)
        m = mm if m is None else jnp.maximum(m, mm)
    num, den = None, None
    for s, v in zip(s_list, v_list):
        p = jnp.exp2(s - m)
        ssum = jnp.sum(p, axis=-1, keepdims=True)
        o = jnp.dot(p.astype(BF16), v, preferred_element_type=F32)
        num = o if num is None else num + o
        den = ssum if den is None else den + ssum
    return num / den


def _na_kernel(q_ref, k_ref, v_ref, cq_ref, ck_ref, cv_ref, tab_ref, cos_ref, sin_ref, y_ref, cy_ref,
               qs_ref, ks_ref, bias_ref, *, starts, types, pieces):
    lane = lax.broadcasted_iota(jnp.int32, (1, HEAD_DIM), 1)
    first = (lane % (HEAD_DIM // 2)) < HEAD_DIM // 4
    scale = HEAD_DIM ** -0.5 * LOG2E

    @pl.when(pl.program_id(1) == 0)
    def _():
        even = lane < GRID_W
        for tp, by_row in enumerate(pieces):
            for r, by_pair in enumerate(by_row):
                for p, (ie, io) in enumerate(by_pair):
                    bias_ref[tp, r * GRID_W:(r + 1) * GRID_W, 2 * p * GRID_W:2 * (p + 1) * GRID_W] = (
                        jnp.where(even, tab_ref[ie], tab_ref[io]))

    def rope(t):
        partner = jnp.where(first, pltpu.roll(t, HEAD_DIM - HEAD_DIM // 4, axis=1),
                            pltpu.roll(t, HEAD_DIM // 4, axis=1))
        return t * cos_ref[...] + partner * sin_ref[...]

    qs_ref[...] = (rope(q_ref[...].astype(F32)) * scale).astype(BF16)
    ks_ref[...] = rope(k_ref[...].astype(F32)).astype(BF16)
    ck = ck_ref[...]
    cv = cv_ref[...]
    qrows = NA_QROWS * GRID_W
    krows = NA_KROWS * GRID_W
    for j, (u, tp) in enumerate(zip(starts, types)):
        qb = qs_ref[j * qrows:(j + 1) * qrows, :]
        kb = ks_ref[u * GRID_W:u * GRID_W + krows, :]
        vb = v_ref[u * GRID_W:u * GRID_W + krows, :]
        s_loc = lax.dot_general(qb, kb, _NT, preferred_element_type=F32) + bias_ref[tp]
        s_ctx = lax.dot_general(qb, ck, _NT, preferred_element_type=F32)
        y_ref[j * qrows:(j + 1) * qrows, :] = _softmax_pv([s_loc, s_ctx], [vb, cv]).astype(y_ref.dtype)
    s = lax.dot_general(cq_ref[...], ck, _NT, preferred_element_type=F32) * scale
    cy_ref[...] = _softmax_pv([s], [cv]).astype(cy_ref.dtype)


def _attention(ub, col0, n, lc, bsz, heads, table, layer, layout):
    starts, types, patterns = layout
    cos, sin = _na_tables(n)
    cb = col0 // HEAD_DIM
    ctx0 = bsz * n // lc

    def lat(k):
        return pl.BlockSpec((n, HEAD_DIM), lambda h, b: (b, cb + k * heads + h))

    def ctx(k):
        return pl.BlockSpec((lc, HEAD_DIM), lambda h, b: (ctx0 + b, cb + k * heads + h))

    tab = pl.BlockSpec((n, HEAD_DIM), lambda h, b: (0, 0))
    return pl.pallas_call(
        functools.partial(_na_kernel, starts=tuple(starts), types=tuple(types), pieces=_na_pieces(patterns)),
        grid=(heads, bsz),
        in_specs=[lat(0), lat(1), lat(2), ctx(0), ctx(1), ctx(2),
                  pl.BlockSpec((None, None) + table.shape[2:], lambda h, b: (layer, h, 0, 0, 0)), tab, tab],
        out_specs=[pl.BlockSpec((n, HEAD_DIM), lambda h, b: (b, h)),
                   pl.BlockSpec((lc, HEAD_DIM), lambda h, b: (b, h))],
        out_shape=[jax.ShapeDtypeStruct((bsz * n, heads * HEAD_DIM), BF16),
                   jax.ShapeDtypeStruct((bsz * lc, heads * HEAD_DIM), BF16)],
        scratch_shapes=[pltpu.VMEM((n, HEAD_DIM), BF16), pltpu.VMEM((n, HEAD_DIM), BF16),
                        pltpu.VMEM((len(patterns), NA_QROWS * GRID_W, NA_KROWS * GRID_W), F32)],
        compiler_params=_params("arbitrary", "arbitrary"), name="attention",
    )(ub, ub, ub, ub, ub, ub, table, jnp.asarray(cos), jnp.asarray(sin))


def _route_kernel(afft_ref, slot_ref, *, n, ne, cap):
    bits = lax.bitcast_convert_type(afft_ref[...], jnp.int32)

    def bisect(i, thr):
        cand = thr | jnp.left_shift(jnp.int32(1), 30 - i)
        cnt = jnp.sum((bits >= cand).astype(F32), axis=1, keepdims=True)
        return jnp.where(cnt >= cap, cand, thr)

    thr = lax.fori_loop(0, 31, bisect, jnp.zeros((ne, 1), jnp.int32))
    above = bits > thr
    tied = bits == thr
    need = cap - jnp.sum(above.astype(F32), axis=1, keepdims=True)

    row = lax.broadcasted_iota(jnp.int32, (LANES, LANES), 0)
    colm = lax.broadcasted_iota(jnp.int32, (LANES, LANES), 1)
    before = (row < colm).astype(BF16)
    ones = jnp.ones((LANES, LANES), BF16)

    def prefix(x):
        xb = x.astype(BF16)
        outs, carry = [], jnp.zeros((ne, LANES), F32)
        for blk in range(n // LANES):
            xs = xb[:, blk * LANES:(blk + 1) * LANES]
            outs.append(jnp.dot(xs, before, preferred_element_type=F32) + carry)
            carry = carry + jnp.dot(xs, ones, preferred_element_type=F32)
        return jnp.concatenate(outs, axis=1)

    chosen = above | (tied & (prefix(tied.astype(F32)) < need))
    slot = prefix(chosen.astype(F32)).astype(jnp.int32)
    slot_ref[...] = jnp.where(chosen, slot, n)


def _route(aff, row0, n, bsz, cap):
    ne = aff.shape[1]
    at = jnp.swapaxes(aff[row0:row0 + bsz * n].reshape(bsz, n, ne), 1, 2)
    spec = pl.BlockSpec((None, ne, n), lambda b: (b, 0, 0))
    slot = pl.pallas_call(
        functools.partial(_route_kernel, n=n, ne=ne, cap=cap),
        grid=(bsz,), in_specs=[spec], out_specs=spec,
        out_shape=jax.ShapeDtypeStruct((bsz, ne, n), jnp.int32),
        compiler_params=_params("arbitrary"), name="expert_route",
    )(at)
    return at, slot


def _gather_kernel(slot_ref, afft_ref, h_ref, xg_ref, g_ref, *, cap, eg):
    e0 = pl.program_id(1) * eg
    want = lax.broadcasted_iota(jnp.int32, (cap, 1), 0)
    sels = [slot_ref[pl.ds(e0 + k, 1), :] == want for k in range(eg)]
    sel = sels[0].astype(BF16) if eg == 1 else jnp.concatenate([s.astype(BF16) for s in sels], axis=0)
    xg = jnp.dot(sel, h_ref[...], preferred_element_type=F32).astype(xg_ref.dtype)
    xg_ref[...] = xg.reshape(xg_ref.shape)
    for k in range(eg):
        g_ref[k] = jnp.sum(jnp.where(sels[k], afft_ref[pl.ds(e0 + k, 1), :], 0.0), axis=1, keepdims=True)


def _gather(slot, afft, h2, row_block0, n, bsz, cap):
    ne = slot.shape[1]
    d = h2.shape[1]
    eg = max(1, min(ne, GATHER_ROWS // cap))
    while ne % eg:
        eg -= 1
    row_spec = pl.BlockSpec((None, ne, n), lambda b, e: (b, 0, 0))
    return pl.pallas_call(
        functools.partial(_gather_kernel, cap=cap, eg=eg),
        grid=(bsz, ne // eg),
        in_specs=[row_spec, row_spec, pl.BlockSpec((n, d), lambda b, e: (row_block0 + b, 0))],
        out_specs=[pl.BlockSpec((eg, cap, d), lambda b, e: (e, b, 0)),
                   pl.BlockSpec((eg, cap, 1), lambda b, e: (e, b, 0))],
        out_shape=[jax.ShapeDtypeStruct((ne, bsz * cap, d), BF16),
                   jax.ShapeDtypeStruct((ne, bsz * cap, 1), F32)],
        compiler_params=_params("arbitrary", "arbitrary"), name="expert_gather",
    )(slot, afft, h2)


def _ffn_kernel(*refs, ns, nf):
    x_refs = refs[0:2 * ns:2]
    g_refs = refs[1:2 * ns:2]
    wg_ref, wu_ref, wd_ref = refs[2 * ns:2 * ns + 3]
    o_refs = refs[2 * ns + 3:3 * ns + 3]
    wgb, wub, wdb = refs[3 * ns + 3:3 * ns + 6]
    acc_refs = refs[3 * ns + 6:]
    f = pl.program_id(1)
    wgb[...] = wg_ref[...].astype(BF16)
    wub[...] = wu_ref[...].astype(BF16)
    wdb[...] = wd_ref[...].astype(BF16)

    def ff_tile(first, last):
        for k in range(ns):
            rows = x_refs[k].shape[0]
            step = min(FFN_ROWS, rows)
            for r0 in range(0, rows, step):
                rs = slice(r0, r0 + step)
                x = x_refs[k][rs, :]
                hg = jnp.dot(x, wgb[...], preferred_element_type=F32)
                hu = jnp.dot(x, wub[...], preferred_element_type=F32)
                hid = (hg * _sigmoid(hg) * hu).astype(BF16)
                part = jnp.dot(hid, wdb[...], preferred_element_type=F32)
                if not first:
                    part = acc_refs[k][rs, :] + part
                if last:
                    o_refs[k][rs, :] = (part * g_refs[k][rs, :]).astype(o_refs[k].dtype)
                else:
                    acc_refs[k][rs, :] = part

    if nf == 1:
        ff_tile(True, True)
    else:
        pl.when(f == 0)(functools.partial(ff_tile, True, False))
        pl.when(f == nf - 1)(functools.partial(ff_tile, False, True))
        if nf > 2:
            pl.when((f > 0) & (f < nf - 1))(functools.partial(ff_tile, False, False))


def _expert_ffn(xgs, gates, w_gate, w_up, w_down, layer):
    ns = len(xgs)
    _, ne, d, ff = w_gate.shape
    tf = _lane_tile(ff, 256)
    nf = ff // tf
    in_specs, out_specs, out_shape, acc_scr, args = [], [], [], [], []
    for k in range(ns):
        rows = xgs[k].shape[1]
        in_specs.append(pl.BlockSpec((None, rows, d), lambda e, f: (e, 0, 0)))
        in_specs.append(pl.BlockSpec((None, rows, 1), lambda e, f: (e, 0, 0)))
        out_specs.append(pl.BlockSpec((None, rows, d), lambda e, f: (e, 0, 0)))
        out_shape.append(jax.ShapeDtypeStruct((ne, rows, d), BF16))
        acc_scr.append(pltpu.VMEM((rows, d), F32))
        args += [xgs[k], gates[k]]
    in_specs += [pl.BlockSpec((None, None, d, tf), lambda e, f: (layer, e, 0, f)),
                 pl.BlockSpec((None, None, d, tf), lambda e, f: (layer, e, 0, f)),
                 pl.BlockSpec((None, None, tf, d), lambda e, f: (layer, e, f, 0))]
    return pl.pallas_call(
        functools.partial(_ffn_kernel, ns=ns, nf=nf),
        grid=(ne, nf),
        in_specs=in_specs, out_specs=out_specs, out_shape=out_shape,
        scratch_shapes=[pltpu.VMEM((d, tf), BF16), pltpu.VMEM((d, tf), BF16), pltpu.VMEM((tf, d), BF16)] + acc_scr,
        compiler_params=_params("arbitrary", "arbitrary"), name="expert_ffn",
    )(*args, w_gate, w_up, w_down)


def _scatter_kernel(slot_ref, y_ref, x_ref, mod_ref, o_ref, *, ne, cap):
    sl = slot_ref[...]
    want = lax.broadcasted_iota(jnp.int32, (1, cap), 1)
    acc = None
    for e in range(ne):
        sel = (sl[:, e:e + 1] == want).astype(BF16)
        p = jnp.dot(sel, y_ref[e], preferred_element_type=F32)
        acc = p if acc is None else acc + p
    o_ref[...] = x_ref[...] + mod_ref[5:6, :] * acc


def _scatter(slot, y, x, mod, layer, mod_row0, row0, n, bsz, cap, tn):
    ne = slot.shape[1]
    d = x.shape[1]
    tm = min(512, n)
    slot_t = jnp.swapaxes(slot, 1, 2)
    xb0 = row0 // tm
    per_seq = n // tm
    return pl.pallas_call(
        functools.partial(_scatter_kernel, ne=ne, cap=cap),
        grid=(bsz, d // tn, per_seq),
        in_specs=[pl.BlockSpec((None, tm, ne), lambda b, j, i: (b, i, 0)),
                  pl.BlockSpec((ne, cap, tn), lambda b, j, i: (0, b, j)),
                  pl.BlockSpec((tm, tn), lambda b, j, i: (xb0 + b * per_seq + i, j)),
                  pl.BlockSpec((None, None, 6, tn), lambda b, j, i: (layer, mod_row0(b), 0, j))],
        out_specs=pl.BlockSpec((tm, tn), lambda b, j, i: (b * per_seq + i, j)),
        out_shape=jax.ShapeDtypeStruct((bsz * n, d), F32),
        compiler_params=_params("arbitrary", "arbitrary", "arbitrary"), name="expert_scatter",
    )(slot_t, y, x, mod)


def kernel(x, c, ctx, c_ctx, w_mod, b_mod, g_norm1, w_in, lb_param, g_hgrn, w_pool, pool_scale, rpb,
           w_branch, w_out, g_norm2, w_router, w_gate_e, w_up_e, w_down_e, g_final):
    bsz, n, d = x.shape
    lc = ctx.shape[1]
    depth = w_mod.shape[0]
    width = lb_param.shape[2]
    heads = width // HEAD_DIM
    ne = w_router.shape[2]
    n_lat = bsz * n
    n_ctx = bsz * lc
    total = n_lat + n_ctx
    assert bsz + 1 <= 8 and n % ROW_TILE == 0 and lc % ROW_TILE == 0 and n_lat % lc == 0
    assert w_pool.shape[-1] * 4 == width and w_branch.shape[2] == width
    tm = _row_tile(n, n_ctx)
    tn = min(512, d)

    lb_all = jnp.cumsum(jax.nn.softmax(lb_param.astype(F32), axis=1), axis=1)
    lb_all = lb_all - lb_all[:, :1]
    lbp_all = jnp.stack([jnp.log(lb_all), jnp.log1p(-lb_all)], axis=2) * LOG2E
    lbp_all = lbp_all.reshape(2, depth, 2, heads, LANES).transpose(0, 1, 3, 2, 4)

    c8 = jnp.concatenate([c, c_ctx[None], jnp.zeros((8 - bsz - 1, d), F32)], axis=0)
    mod = _modulation(c8, w_mod, b_mod).reshape(depth, 8, 6, d)

    cm3, pair_masks, upper = _hgrn_constants()
    hgrn_consts = (jnp.asarray(cm3, BF16), jnp.asarray(pair_masks, F32), jnp.asarray(upper, F32))
    na_layout = _na_block_layout(n // GRID_W)
    na_bias = _na_bias_table(rpb)
    lanes_e = -(-ne // LANES) * LANES
    w_router_p = jnp.pad(w_router, ((0, 0), (0, 0), (0, lanes_e - ne)))

    x_lat = x.reshape(n_lat, d)
    x_ctx = ctx.reshape(n_ctx, d)
    zero_state = jnp.zeros((bsz, heads, LANES, LANES), F32)
    cap = EC_CAPACITY * n // ne
    cap_c = EC_CAPACITY * lc // ne
    a_cols = 5 * width
    b_cols = w_in.shape[2] - a_cols
    tn_in = _lane_tile(math.gcd(a_cols, b_cols), 1024)
    gate_col0 = 4 * width
    rt_all = _Rows(tm, n, bsz, n_lat, total)
    rt_lat = _Rows(tm, n, bsz, n_lat, n_lat)
    tm_norm = min(NORM_ROWS, tm)
    rn_all = _Rows(tm_norm, n, bsz, n_lat, total)
    rn_lat = _Rows(tm_norm, n, bsz, n_lat, n_lat)
    tm_out, tn_out = max(tm // 2, ROW_TILE), _lane_tile(d, 1024)
    ro_all = _Rows(tm_out, n, bsz, n_lat, total)
    ro_lat = _Rows(tm_out, n, bsz, n_lat, n_lat)

    for l in range(depth):
        last = l == depth - 1
        rt, rn, ro = (rt_lat, rn_lat, ro_lat) if last else (rt_all, rn_all, ro_all)
        gain_h = g_hgrn[l].reshape(heads, 1, LANES)

        h = _norm1(x_lat, x_ctx, g_norm1[l], mod, l, rn_all)
        ua = _matmul(h, w_in, l, 0, a_cols, F32, tm, tn_in)
        ub = _matmul(h, w_in, l, a_cols, b_cols, BF16, tm, tn_in)

        cy_a, s_f, s_b = _hgrn(ua, n_lat // lc, lc, bsz, heads, lbp_all[:, l], gain_h, zero_state, zero_state,
                               hgrn_consts)
        y_a, _, _ = _hgrn(ua, 0, n, bsz, heads, lbp_all[:, l], gain_h, s_f, s_b, hgrn_consts)
        y_b = _pool(ub, 0, n, bsz, w_pool, pool_scale, l)
        y_c, cy_c = _attention(ub, width, n, lc, bsz, heads, na_bias, l, na_layout)
        ys_lat = (y_a, y_b, y_c)
        ys_ctx = ys_lat if last else (cy_a, _pool(ub, n_lat // lc, lc, bsz, w_pool, pool_scale, l), cy_c)

        merged = _merge(ys_lat, ys_ctx, ub, gate_col0, w_branch, l, rt, tn)
        x_mid = _out_proj(merged, w_out, l, x_lat, x_ctx, mod, ro, tn_out)

        h2, aff = _norm2(x_mid, g_norm2[l], mod, l, rn, w_router_p, ne)
        afft, slot = _route(aff, 0, n, bsz, cap)
        xg, gate = _gather(slot, afft, h2, 0, n, bsz, cap)
        xgs, gates = [xg], [gate]
        if not last:
            afft_c, slot_c = _route(aff, n_lat, lc, bsz, cap_c)
            xg_c, gate_c = _gather(slot_c, afft_c, h2, n_lat // lc, lc, bsz, cap_c)
            xgs, gates = xgs + [xg_c], gates + [gate_c]
        ys = _expert_ffn(xgs, gates, w_gate_e, w_up_e, w_down_e, l)
        x_lat = _scatter(slot, ys[0], x_mid, mod, l, lambda b: b, 0, n, bsz, cap, tn)
        if not last:
            x_ctx = _scatter(slot_c, ys[1], x_mid, mod, l, lambda b: bsz, n_lat, lc, bsz, cap_c, tn)

    return _final_norm(x_lat, g_final, tm_norm).reshape(bsz, n, d)
```

```python
import functools
import math

import numpy as np
import jax
import jax.numpy as jnp
from jax import lax
from jax.experimental import pallas as pl
from jax.experimental.pallas import tpu as pltpu

F32 = jnp.float32
BF16 = jnp.bfloat16
EPS = 1e-6
LANES = 128
HEAD_DIM = 128
CHUNK = 64
HGRN_GROUP = 4
GRID_W = 64
POOL_WINDOWS = (2, 4, 8, 16)
POOL_PAD = 16
NA_ROWS_MAX = 8
NA_COLS = 16
NA_QROWS = 4
NA_KROWS = 12
ROPE_THETA = 10000.0
EC_CAPACITY = 2
GATHER_ROWS = 512
FFN_ROWS = 512
NEG = -1e30
ROW_TILE = 256
NORM_ROWS = 512
MAX_ROW_TILE = 1024
WIDE_COL_TILE = 1024
MERGE_COL_TILE = 512
FF_TILE = 256
SCATTER_ROWS = 512
F32_SIGN_BIT = 0x80000000
F32_VALUE_BITS = 31
LOG2E = 1.4426950408889634
VMEM_LIMIT = 56 * 2 ** 20


def _params(*sem):
    return pltpu.CompilerParams(dimension_semantics=sem, vmem_limit_bytes=VMEM_LIMIT)


def _sigmoid(x):
    return 1.0 / (1.0 + jnp.exp(-x))


def _neg_abs(x):
    bits = lax.bitcast_convert_type(x, jnp.uint32) | jnp.uint32(F32_SIGN_BIT)
    return lax.bitcast_convert_type(bits, F32)


def _lane_tile(n, limit):
    t = min(limit, n) // LANES * LANES
    while n % t:
        t -= LANES
    return t


def _row_tile(n, ctx_rows):
    tm = MAX_ROW_TILE
    while n % tm or ctx_rows % tm:
        tm //= 2
    return tm


class _Rows:
    def __init__(self, tm, n, bsz, n_lat, rows):
        self.tm, self.n, self.bsz = tm, n, bsz
        self.lat_tiles = n_lat // tm
        self.tiles = rows // tm
        self.ctx_tiles = max(self.tiles - self.lat_tiles, 1)

    def lat(self, i):
        return jnp.minimum(i, self.lat_tiles - 1)

    def ctx(self, i):
        return jnp.clip(i - self.lat_tiles, 0, self.ctx_tiles - 1)

    def mod_row(self, i):
        return jnp.minimum(i * self.tm // self.n, self.bsz)


def _pick(is_lat, lat_ref, ctx_ref):
    return jnp.where(is_lat, lat_ref[...], ctx_ref[...])


def _mod_kernel(c_ref, w_ref, b_ref, o_ref):
    c = c_ref[...]
    sc = (c * _sigmoid(c)).astype(BF16)
    o_ref[...] = jnp.dot(sc, w_ref[...].astype(BF16), preferred_element_type=F32) + b_ref[...]


def _modulation(c8, w_mod, b_mod):
    depth, d, n6 = w_mod.shape
    tn = _lane_tile(n6, WIDE_COL_TILE)
    return pl.pallas_call(
        _mod_kernel,
        grid=(depth, n6 // tn),
        in_specs=[pl.BlockSpec((8, d), lambda l, j: (0, 0)),
                  pl.BlockSpec((None, d, tn), lambda l, j: (l, 0, j)),
                  pl.BlockSpec((None, 1, tn), lambda l, j: (l, 0, j))],
        out_specs=pl.BlockSpec((None, 8, tn), lambda l, j: (l, 0, j)),
        out_shape=jax.ShapeDtypeStruct((depth, 8, n6), F32),
        compiler_params=_params("arbitrary", "arbitrary"),
        name="modulation",
    )(c8, w_mod, b_mod.reshape(depth, 1, n6))


def _rmsnorm(x, gain):
    return x * lax.rsqrt(jnp.mean(x * x, axis=-1, keepdims=True) + EPS) * gain


def _modulate(y, mod_ref, shift_idx):
    return y * (1.0 + mod_ref[shift_idx + 1:shift_idx + 2, :]) + mod_ref[shift_idx:shift_idx + 1, :]


def _norm1_kernel(xl_ref, xc_ref, g_ref, mod_ref, o_ref, *, lat_tiles):
    x = _pick(pl.program_id(0) < lat_tiles, xl_ref, xc_ref)
    o_ref[...] = _modulate(_rmsnorm(x, g_ref[...]), mod_ref, 0).astype(o_ref.dtype)


def _norm2_kernel(x_ref, g_ref, mod_ref, wr_ref, o_ref, aff_ref):
    h = _modulate(_rmsnorm(x_ref[...], g_ref[...]), mod_ref, 3)
    o_ref[...] = h.astype(o_ref.dtype)
    logits = jnp.dot(h, wr_ref[...], precision=lax.Precision.HIGHEST, preferred_element_type=F32)
    logits = logits[:, :aff_ref.shape[1]]
    e = jnp.exp(logits - jnp.max(logits, axis=-1, keepdims=True))
    aff_ref[...] = e / jnp.sum(e, axis=-1, keepdims=True)


def _final_norm_kernel(x_ref, g_ref, o_ref):
    o_ref[...] = _rmsnorm(x_ref[...], g_ref[...])


def _norm1(x_lat, x_ctx, gain, mod, layer, rt):
    d = x_lat.shape[1]
    return pl.pallas_call(
        functools.partial(_norm1_kernel, lat_tiles=rt.lat_tiles),
        grid=(rt.tiles,),
        in_specs=[pl.BlockSpec((rt.tm, d), lambda i: (rt.lat(i), 0)),
                  pl.BlockSpec((rt.tm, d), lambda i: (rt.ctx(i), 0)),
                  pl.BlockSpec((1, d), lambda i: (0, 0)),
                  pl.BlockSpec((None, None, 6, d), lambda i: (layer, rt.mod_row(i), 0, 0))],
        out_specs=pl.BlockSpec((rt.tm, d), lambda i: (i, 0)),
        out_shape=jax.ShapeDtypeStruct((rt.tiles * rt.tm, d), BF16),
        compiler_params=_params("arbitrary"), name="norm1",
    )(x_lat, x_ctx, gain.reshape(1, d), mod)


def _norm2(x, gain, mod, layer, rt, w_router_padded, ne):
    d = x.shape[1]
    rows = rt.tiles * rt.tm
    return pl.pallas_call(
        _norm2_kernel,
        grid=(rt.tiles,),
        in_specs=[pl.BlockSpec((rt.tm, d), lambda i: (i, 0)),
                  pl.BlockSpec((1, d), lambda i: (0, 0)),
                  pl.BlockSpec((None, None, 6, d), lambda i: (layer, rt.mod_row(i), 0, 0)),
                  pl.BlockSpec((None, d, w_router_padded.shape[2]), lambda i: (layer, 0, 0))],
        out_specs=[pl.BlockSpec((rt.tm, d), lambda i: (i, 0)), pl.BlockSpec((rt.tm, ne), lambda i: (i, 0))],
        out_shape=[jax.ShapeDtypeStruct((rows, d), BF16), jax.ShapeDtypeStruct((rows, ne), F32)],
        compiler_params=_params("arbitrary"), name="norm2_router",
    )(x, gain.reshape(1, d), mod, w_router_padded)


def _final_norm(x, gain, tm):
    rows, d = x.shape
    return pl.pallas_call(
        _final_norm_kernel,
        grid=(rows // tm,),
        in_specs=[pl.BlockSpec((tm, d), lambda i: (i, 0)), pl.BlockSpec((1, d), lambda i: (0, 0))],
        out_specs=pl.BlockSpec((tm, d), lambda i: (i, 0)),
        out_shape=jax.ShapeDtypeStruct((rows, d), F32),
        compiler_params=_params("arbitrary"), name="final_norm",
    )(x, gain.reshape(1, d))


def _mm_kernel(a_ref, w_ref, o_ref, wbf_ref):
    @pl.when(pl.program_id(1) == 0)
    def _():
        wbf_ref[...] = w_ref[...].astype(BF16)
    o_ref[...] = jnp.dot(a_ref[...], wbf_ref[...], preferred_element_type=F32).astype(o_ref.dtype)


def _matmul(a, w_all, layer, col0, ncols, out_dtype, tm, tn):
    m, k = a.shape
    off = col0 // tn
    return pl.pallas_call(
        _mm_kernel,
        grid=(ncols // tn, m // tm),
        in_specs=[pl.BlockSpec((tm, k), lambda j, i: (i, 0)),
                  pl.BlockSpec((None, k, tn), lambda j, i: (layer, 0, j + off))],
        out_specs=pl.BlockSpec((tm, tn), lambda j, i: (i, j)),
        out_shape=jax.ShapeDtypeStruct((m, ncols), out_dtype),
        scratch_shapes=[pltpu.VMEM((k, tn), BF16)],
        compiler_params=_params("arbitrary", "arbitrary"), name="matmul",
    )(a, w_all)


def _out_proj_kernel(a_ref, w_ref, xl_ref, xc_ref, mod_ref, o_ref, wbf_ref, *, lat_tiles):
    @pl.when(pl.program_id(1) == 0)
    def _():
        wbf_ref[...] = w_ref[...].astype(BF16)
    acc = jnp.dot(a_ref[...], wbf_ref[...], preferred_element_type=F32)
    x = _pick(pl.program_id(1) < lat_tiles, xl_ref, xc_ref)
    o_ref[...] = x + mod_ref[2:3, :] * acc


def _out_proj(a, w_all, layer, x_lat, x_ctx, mod, rt, tn):
    k = a.shape[1]
    d = w_all.shape[2]
    tm = rt.tm
    return pl.pallas_call(
        functools.partial(_out_proj_kernel, lat_tiles=rt.lat_tiles),
        grid=(d // tn, rt.tiles),
        in_specs=[pl.BlockSpec((tm, k), lambda j, i: (i, 0)),
                  pl.BlockSpec((None, k, tn), lambda j, i: (layer, 0, j)),
                  pl.BlockSpec((tm, tn), lambda j, i: (rt.lat(i), j)),
                  pl.BlockSpec((tm, tn), lambda j, i: (rt.ctx(i), j)),
                  pl.BlockSpec((None, None, 6, tn), lambda j, i: (layer, rt.mod_row(i), 0, j))],
        out_specs=pl.BlockSpec((tm, tn), lambda j, i: (i, j)),
        out_shape=jax.ShapeDtypeStruct((rt.tiles * tm, d), F32),
        scratch_shapes=[pltpu.VMEM((k, tn), BF16)],
        compiler_params=_params("arbitrary", "arbitrary"), name="out_proj",
    )(a, w_all, x_lat, x_ctx, mod)


def _merge_kernel(ya_ref, yb_ref, yc_ref, cya_ref, cyb_ref, cyc_ref, g0_ref, g1_ref, g2_ref, w_ref, o_ref, wbf_ref,
                  *, lat_tiles):
    @pl.when(pl.program_id(1) == 0)
    def _():
        wbf_ref[...] = w_ref[...].astype(BF16)
    is_lat = pl.program_id(1) < lat_tiles
    acc = None
    for j, (y_ref, cy_ref, g_ref) in enumerate(((ya_ref, cya_ref, g0_ref), (yb_ref, cyb_ref, g1_ref),
                                                (yc_ref, cyc_ref, g2_ref))):
        p = jnp.dot(_pick(is_lat, y_ref, cy_ref), wbf_ref[j], preferred_element_type=F32)
        t = _sigmoid(g_ref[...].astype(F32)) * p
        acc = t if acc is None else acc + t
    o_ref[...] = acc.astype(o_ref.dtype)


def _merge(ys_lat, ys_ctx, ub, gate_col0, w_branch, layer, rt, tn):
    bw = ys_lat[0].shape[1]
    d = w_branch.shape[3]
    tm = rt.tm
    lat_spec = pl.BlockSpec((tm, bw), lambda j, i: (rt.lat(i), 0))
    ctx_spec = pl.BlockSpec((tm, bw), lambda j, i: (rt.ctx(i), 0))

    def gate_spec(k):
        off = (gate_col0 + k * d) // tn
        return pl.BlockSpec((tm, tn), lambda j, i: (i, off + j))

    return pl.pallas_call(
        functools.partial(_merge_kernel, lat_tiles=rt.lat_tiles),
        grid=(d // tn, rt.tiles),
        in_specs=[lat_spec] * 3 + [ctx_spec] * 3 + [gate_spec(0), gate_spec(1), gate_spec(2),
                  pl.BlockSpec((None, 3, bw, tn), lambda j, i: (layer, 0, 0, j))],
        out_specs=pl.BlockSpec((tm, tn), lambda j, i: (i, j)),
        out_shape=jax.ShapeDtypeStruct((rt.tiles * tm, d), BF16),
        scratch_shapes=[pltpu.VMEM((3, bw, tn), BF16)],
        compiler_params=_params("arbitrary", "arbitrary"), name="merge",
    )(*ys_lat, *ys_ctx, ub, ub, ub, w_branch)


_HGRN_BLOCKS = (32, 16, 8, 4, 2, 1)


def _hgrn_constants():
    c = CHUNK
    t = np.arange(c)
    tri_f = (t[:, None] >= t[None, :]).astype(np.float32)
    tri_b = np.ascontiguousarray(tri_f[::-1, ::-1])
    sel_f, sel_b, masks, upper = [tri_f], [tri_b], [], []
    for m in _HGRN_BLOCKS:
        blk = (t // (2 * m)) * (2 * m)
        up = (t % (2 * m)) >= m
        sel_f.append(tri_f - tri_f[blk + m - 1])
        sel_b.append(tri_b - tri_b[blk + m])
        same = (t[:, None] // (2 * m)) == (t[None, :] // (2 * m))
        masks.append(same & (up[:, None] != up[None, :]))
        upper.append(np.broadcast_to(up[:, None], (c, LANES)))
    masks.append(np.eye(c, dtype=bool))
    sel = np.stack([np.concatenate(sel_f, axis=0), np.concatenate(sel_b, axis=0)])
    return (np.concatenate([sel, sel, sel], axis=2), np.stack(masks).astype(np.float32),
            np.stack(upper).astype(np.float32))


def _hgrn_log2_gate(z, lbp):
    zl = z * LOG2E
    log_sig = jnp.minimum(zl, 0.0) - jnp.log2(1.0 + jnp.exp2(_neg_abs(zl)))
    a = lbp[0:1, :]
    cc = lbp[1:2, :] + log_sig
    return jnp.maximum(a, cc) + jnp.log2(1.0 + jnp.exp2(_neg_abs(a - cc)))


def _split3(x):
    g1 = x.astype(BF16)
    r1 = x - g1.astype(F32)
    g2 = r1.astype(BF16)
    g3 = (r1 - g2.astype(F32)).astype(BF16)
    return jnp.concatenate([g1, g2, g3], axis=0)


_NT = (((1,), (1,)), ((), ()))
_TN = (((0,), (0,)), ((), ()))


def _hgrn_kernel(q_ref, ff_ref, fb_ref, i_ref, g_ref, lbf_ref, lbb_ref, gain_ref, cm_ref, mask_ref, up_ref,
                 sf0_ref, sb0_ref, y_ref, sf_ref, sb_ref, o_scr, qt_scr, upd_scr, dec_scr, st_scr, sums_scr, k_scr,
                 *, nc):
    z_refs = (ff_ref, fb_ref)
    lb_refs = (lbf_ref, lbb_ref)
    nl = len(_HGRN_BLOCKS)
    last_row = (CHUNK - 1, 0)

    groups = nc // HGRN_GROUP
    srows = (1 + nl) * CHUNK

    def gate_pass(it, carry):
        rows = [pl.ds(pl.multiple_of((it * HGRN_GROUP + s) * CHUNK, CHUNK), CHUNK) for s in range(HGRN_GROUP)]
        for d in range(2):
            logf = [_hgrn_log2_gate(z_refs[d][r, :], lb_refs[d][...]) for r in rows]
            sums_scr[d, pl.ds(pl.multiple_of(it * srows, srows), srows), :] = jnp.dot(
                cm_ref[d], jnp.concatenate([_split3(lf) for lf in logf], axis=1), preferred_element_type=F32)
            for r, lf in zip(rows, logf):
                k_scr[d, r, :] = 1.0 - jnp.exp2(lf)
        return carry

    lax.fori_loop(0, groups, gate_pass, 0, unroll=min(4, groups))

    def local_pass(it, carry):
        sums = [sums_scr[d, pl.ds(pl.multiple_of(it * srows, srows), srows), :] for d in range(2)]
        rows = [pl.ds(pl.multiple_of((it * HGRN_GROUP + s) * CHUNK, CHUNK), CHUNK) for s in range(HGRN_GROUP)]
        up = up_ref[...] > 0.5
        for s, r in enumerate(rows):
            lanes = slice(s * LANES, (s + 1) * LANES)
            qr = q_ref[r, :]
            q = qr * _sigmoid(qr)
            vb = i_ref[r, :].astype(BF16)
            b = [sums[d][0:CHUNK, lanes] for d in range(2)]
            lvl = [jnp.exp2(_neg_abs(sums[d][CHUNK:, lanes].reshape(nl, CHUNK, LANES))) for d in range(2)]
            k = [k_scr[d, r, :] for d in range(2)]
            q_dec = jnp.where(up, lvl[0], lvl[1])
            k_dec = jnp.where(up, k[1][None] * lvl[1], k[0][None] * lvl[0])
            qs = jnp.concatenate([q[None] * q_dec, q[None]], axis=0).astype(BF16)
            ks = jnp.concatenate([k_dec, (k[0] + k[1])[None]], axis=0).astype(BF16)
            sc = jnp.einsum("ltk,lsk->lts", qs, ks, preferred_element_type=F32)
            amat = jnp.sum(sc * mask_ref[...], axis=0)
            o_scr[r, :] = jnp.dot(amat.astype(BF16), vb, preferred_element_type=F32)
            tot = [b[d][last_row[d]:last_row[d] + 1, :] for d in range(2)]
            kd = jnp.concatenate([(k[d] * jnp.exp2(tot[d] - b[d])).astype(BF16) for d in range(2)], axis=1)
            upd = lax.dot_general(vb, kd, _TN, preferred_element_type=F32)
            j = it * HGRN_GROUP + s
            for d in range(2):
                qt_scr[d, r, :] = (q * jnp.exp2(b[d])).astype(BF16)
                upd_scr[d, j] = upd[:, d * LANES:(d + 1) * LANES]
                dec_scr[d, pl.ds(j, 1), :] = jnp.exp2(tot[d])
        return carry

    lax.fori_loop(0, groups, local_pass, 0, unroll=min(4, groups))

    st_scr[0] = sf0_ref[...]
    st_scr[1] = sb0_ref[...]

    def state_pass(it, carry):
        for d in range(2):
            j = it if d == 0 else nc - 1 - it
            rows = pl.ds(pl.multiple_of(j * CHUNK, CHUNK), CHUNK)
            st = st_scr[d]
            o_scr[rows, :] += lax.dot_general(qt_scr[d, rows, :], st.astype(BF16), _NT, preferred_element_type=F32)
            st_scr[d] = st * dec_scr[d, pl.ds(j, 1), :] + upd_scr[d, j]
        return carry

    lax.fori_loop(0, nc, state_pass, 0, unroll=min(8, nc))
    sf_ref[...] = st_scr[0]
    sb_ref[...] = st_scr[1]
    o = o_scr[...]
    o = o * lax.rsqrt(jnp.mean(o * o, axis=-1, keepdims=True) + EPS) * gain_ref[...]
    g = g_ref[...]
    y_ref[...] = (o * (g * _sigmoid(g))).astype(y_ref.dtype)


def _hgrn(ua, row_block0, n, bsz, heads, lbp, gain, sf0, sb0, consts):
    nc = n // CHUNK
    assert nc % HGRN_GROUP == 0 and nc % 2 == 0

    def col(k):
        return pl.BlockSpec((n, LANES), lambda b, h: (row_block0 + b, k * heads + h))

    lb_spec = lambda d: pl.BlockSpec((None, None, 2, LANES), lambda b, h: (d, h, 0, 0))
    st_spec = pl.BlockSpec((None, None, LANES, LANES), lambda b, h: (b, h, 0, 0))
    st_shape = jax.ShapeDtypeStruct((bsz, heads, LANES, LANES), F32)
    const_specs = [pl.BlockSpec(a.shape, lambda b, h, nd=a.ndim: (0,) * nd) for a in consts]
    return pl.pallas_call(
        functools.partial(_hgrn_kernel, nc=nc),
        grid=(bsz, heads),
        in_specs=[col(0), col(1), col(2), col(3), col(4), lb_spec(0), lb_spec(1),
                  pl.BlockSpec((None, 1, LANES), lambda b, h: (h, 0, 0))] + const_specs + [st_spec, st_spec],
        out_specs=[pl.BlockSpec((n, LANES), lambda b, h: (b, h)), st_spec, st_spec],
        out_shape=[jax.ShapeDtypeStruct((bsz * n, heads * LANES), BF16), st_shape, st_shape],
        scratch_shapes=[pltpu.VMEM((n, LANES), F32), pltpu.VMEM((2, n, LANES), BF16),
                        pltpu.VMEM((2, nc, LANES, LANES), F32), pltpu.VMEM((2, nc, LANES), F32),
                        pltpu.VMEM((2, LANES, LANES), F32),
                        pltpu.VMEM((2, nc * (1 + len(_HGRN_BLOCKS)) * CHUNK // HGRN_GROUP, HGRN_GROUP * LANES), F32),
                        pltpu.VMEM((2, n, LANES), F32)],
        compiler_params=_params("arbitrary", "arbitrary"), name="hgrn2",
    )(ua, ua, ua, ua, ua, lbp, lbp, gain, *consts, sf0, sb0)


def _pool_kernel(u_ref, w_ref, s_ref, y_ref, pad_ref, *, n, group):
    pos = lax.broadcasted_iota(jnp.int32, (n, 1), 0)
    zeros = jnp.zeros((POOL_PAD, pad_ref.shape[1]), F32)
    pad_ref[0:POOL_PAD, :] = zeros
    pad_ref[POOL_PAD + n:2 * POOL_PAD + n, :] = zeros
    pad_ref[POOL_PAD:POOL_PAD + n, :] = u_ref[...].astype(F32)
    for gi, w in enumerate(POOL_WINDOWS):
        cols = slice(gi * group, (gi + 1) * group)
        acc = None
        for dlt in range(-(w // 2), w // 2):
            t = pad_ref[POOL_PAD + dlt:POOL_PAD + dlt + n, cols]
            acc = t if acc is None else acc + t
        lo = jnp.maximum(pos - w // 2, 0)
        hi = jnp.minimum(pos + w // 2 - 1, n - 1)
        cnt = (hi - lo + 1).astype(F32)
        dd = acc / cnt - pad_ref[POOL_PAD:POOL_PAD + n, cols]
        y = jnp.dot(dd.astype(BF16), w_ref[gi].astype(BF16), preferred_element_type=F32)
        y_ref[:, cols] = (y * s_ref[:, cols]).astype(y_ref.dtype)


def _pool(ub, row_block0, n, bsz, w_pool, scale, layer):
    group = w_pool.shape[-1]
    width = 4 * group
    return pl.pallas_call(
        functools.partial(_pool_kernel, n=n, group=group),
        grid=(bsz,),
        in_specs=[pl.BlockSpec((n, width), lambda b: (row_block0 + b, 0)),
                  pl.BlockSpec((None, 4, group, group), lambda b: (layer, 0, 0, 0)),
                  pl.BlockSpec((None, 1, width), lambda b: (layer, 0, 0))],
        out_specs=pl.BlockSpec((n, width), lambda b: (b, 0)),
        out_shape=jax.ShapeDtypeStruct((bsz * n, width), BF16),
        scratch_shapes=[pltpu.VMEM((n + 2 * POOL_PAD, width), F32)],
        compiler_params=_params("arbitrary"), name="pool",
    )(ub, w_pool, scale.reshape(scale.shape[0], 1, width))


def _na_tables(n):
    pos = np.arange(n)
    half = HEAD_DIM // 2
    inv_freq = ROPE_THETA ** (-np.arange(0, half, 2, dtype=np.float64) / half)
    lane = np.arange(HEAD_DIM)
    p = np.where(lane[None, :] < half, (pos // GRID_W)[:, None], (pos % GRID_W)[:, None]).astype(np.float64)
    ang = p * inv_freq[lane % (half // 2)][None, :]
    sign = np.where((lane % half) < half // 2, -1.0, 1.0)[None, :]
    return np.cos(ang).astype(np.float32), (np.sin(ang) * sign).astype(np.float32)


def _na_block_layout(rows):
    kr = NA_ROWS_MAX
    nblk = rows // NA_QROWS
    starts, patterns, types = [], [], []
    for j in range(nblk):
        u = int(np.clip(NA_QROWS * j - kr // 2, 0, rows - NA_KROWS))
        r = NA_QROWS * j + np.arange(NA_QROWS)
        start_r = np.clip(r - kr // 2, 0, rows - kr)
        kabs = u + np.arange(NA_KROWS)
        valid = (kabs[None, :] >= start_r[:, None]) & (kabs[None, :] < start_r[:, None] + kr)
        assert valid.sum(axis=1).min() == kr
        dr = np.clip(kabs[None, :] - r[:, None] + NA_ROWS_MAX - 1, 0, 2 * NA_ROWS_MAX - 2)
        key = (valid.tobytes(), dr.tobytes())
        keys = [p[0] for p in patterns]
        if key not in keys:
            patterns.append((key, valid, dr))
        types.append([p[0] for p in patterns].index(key))
        starts.append(u)
    return starts, types, [(p[1], p[2]) for p in patterns]


NA_DR = 2 * NA_ROWS_MAX - 1


def _na_bias_table(rpb):
    qcol = np.arange(GRID_W)
    col_start = np.clip(qcol - NA_COLS // 2, 0, GRID_W - NA_COLS)
    kcol = np.arange(GRID_W)
    col_mask = (kcol[None, :] >= col_start[:, None]) & (kcol[None, :] < col_start[:, None] + NA_COLS)
    dc = np.clip(kcol[None, :] - qcol[:, None] + NA_COLS - 1, 0, 2 * NA_COLS - 2)
    col_hot = (dc[None] == np.arange(2 * NA_COLS - 1)[:, None, None]).astype(np.float32)
    by_col = jnp.einsum("lhab,bqk->lhaqk", rpb.astype(F32), col_hot, precision=lax.Precision.HIGHEST)
    by_col = jnp.where(col_mask, by_col * LOG2E, NEG)
    masked = jnp.full(by_col.shape[:2] + (1, GRID_W, GRID_W), NEG, F32)
    table = jnp.concatenate([by_col, masked], axis=2)
    return jnp.concatenate([table, table], axis=-1)


def _na_pieces(patterns):
    out = []
    for valid, dr in patterns:
        idx = np.where(valid, dr, NA_DR)
        out.append(tuple(tuple((int(idx[r, 2 * p]), int(idx[r, 2 * p + 1])) for p in range(NA_KROWS // 2))
                         for r in range(NA_QROWS)))
    return tuple(out)


def _softmax_pv(s_list, v_list):
    m = None
    for s in s_list:
        mm = jnp.max(s, axis=-1, keepdims=True)
        m = mm if m is None else jnp.maximum(m, mm)
    num, den = None, None
    for s, v in zip(s_list, v_list):
        p = jnp.exp2(s - m)
        ssum = jnp.sum(p, axis=-1, keepdims=True)
        o = jnp.dot(p.astype(BF16), v, preferred_element_type=F32)
        num = o if num is None else num + o
        den = ssum if den is None else den + ssum
    return num / den


def _na_kernel(q_ref, k_ref, v_ref, cq_ref, ck_ref, cv_ref, tab_ref, cos_ref, sin_ref, y_ref, cy_ref,
               qs_ref, ks_ref, bias_ref, *, starts, types, pieces):
    lane = lax.broadcasted_iota(jnp.int32, (1, HEAD_DIM), 1)
    first = (lane % (HEAD_DIM // 2)) < HEAD_DIM // 4
    scale = HEAD_DIM ** -0.5 * LOG2E

    @pl.when(pl.program_id(1) == 0)
    def _():
        even = lane < GRID_W
        for tp, by_row in enumerate(pieces):
            for r, by_pair in enumerate(by_row):
                for p, (ie, io) in enumerate(by_pair):
                    bias_ref[tp, r * GRID_W:(r + 1) * GRID_W, 2 * p * GRID_W:2 * (p + 1) * GRID_W] = (
                        jnp.where(even, tab_ref[ie], tab_ref[io]))

    def rope(t):
        partner = jnp.where(first, pltpu.roll(t, HEAD_DIM - HEAD_DIM // 4, axis=1),
                            pltpu.roll(t, HEAD_DIM // 4, axis=1))
        return t * cos_ref[...] + partner * sin_ref[...]

    qs_ref[...] = (rope(q_ref[...].astype(F32)) * scale).astype(BF16)
    ks_ref[...] = rope(k_ref[...].astype(F32)).astype(BF16)
    ck = ck_ref[...]
    cv = cv_ref[...]
    qrows = NA_QROWS * GRID_W
    krows = NA_KROWS * GRID_W
    for j, (u, tp) in enumerate(zip(starts, types)):
        qb = qs_ref[j * qrows:(j + 1) * qrows, :]
        kb = ks_ref[u * GRID_W:u * GRID_W + krows, :]
        vb = v_ref[u * GRID_W:u * GRID_W + krows, :]
        s_loc = lax.dot_general(qb, kb, _NT, preferred_element_type=F32) + bias_ref[tp]
        s_ctx = lax.dot_general(qb, ck, _NT, preferred_element_type=F32)
        y_ref[j * qrows:(j + 1) * qrows, :] = _softmax_pv([s_loc, s_ctx], [vb, cv]).astype(y_ref.dtype)
    s = lax.dot_general(cq_ref[...], ck, _NT, preferred_element_type=F32) * scale
    cy_ref[...] = _softmax_pv([s], [cv]).astype(cy_ref.dtype)


def _attention(ub, col0, n, lc, bsz, heads, table, layer, layout):
    starts, types, patterns = layout
    cos, sin = _na_tables(n)
    cb = col0 // HEAD_DIM
    ctx0 = bsz * n // lc

    def lat(k):
        return pl.BlockSpec((n, HEAD_DIM), lambda h, b: (b, cb + k * heads + h))

    def ctx(k):
        return pl.BlockSpec((lc, HEAD_DIM), lambda h, b: (ctx0 + b, cb + k * heads + h))

    tab = pl.BlockSpec((n, HEAD_DIM), lambda h, b: (0, 0))
    return pl.pallas_call(
        functools.partial(_na_kernel, starts=tuple(starts), types=tuple(types), pieces=_na_pieces(patterns)),
        grid=(heads, bsz),
        in_specs=[lat(0), lat(1), lat(2), ctx(0), ctx(1), ctx(2),
                  pl.BlockSpec((None, None) + table.shape[2:], lambda h, b: (layer, h, 0, 0, 0)), tab, tab],
        out_specs=[pl.BlockSpec((n, HEAD_DIM), lambda h, b: (b, h)),
                   pl.BlockSpec((lc, HEAD_DIM), lambda h, b: (b, h))],
        out_shape=[jax.ShapeDtypeStruct((bsz * n, heads * HEAD_DIM), BF16),
                   jax.ShapeDtypeStruct((bsz * lc, heads * HEAD_DIM), BF16)],
        scratch_shapes=[pltpu.VMEM((n, HEAD_DIM), BF16), pltpu.VMEM((n, HEAD_DIM), BF16),
                        pltpu.VMEM((len(patterns), NA_QROWS * GRID_W, NA_KROWS * GRID_W), F32)],
        compiler_params=_params("arbitrary", "arbitrary"), name="attention",
    )(ub, ub, ub, ub, ub, ub, table, jnp.asarray(cos), jnp.asarray(sin))


def _route_kernel(afft_ref, slot_ref, *, n, ne, cap):
    bits = lax.bitcast_convert_type(afft_ref[...], jnp.int32)

    def bisect(i, thr):
        cand = thr | jnp.left_shift(jnp.int32(1), F32_VALUE_BITS - 1 - i)
        cnt = jnp.sum((bits >= cand).astype(F32), axis=1, keepdims=True)
        return jnp.where(cnt >= cap, cand, thr)

    thr = lax.fori_loop(0, F32_VALUE_BITS, bisect, jnp.zeros((ne, 1), jnp.int32))
    above = bits > thr
    tied = bits == thr
    need = cap - jnp.sum(above.astype(F32), axis=1, keepdims=True)

    row = lax.broadcasted_iota(jnp.int32, (LANES, LANES), 0)
    colm = lax.broadcasted_iota(jnp.int32, (LANES, LANES), 1)
    before = (row < colm).astype(BF16)
    ones = jnp.ones((LANES, LANES), BF16)

    def prefix(x):
        xb = x.astype(BF16)
        outs, carry = [], jnp.zeros((ne, LANES), F32)
        for blk in range(n // LANES):
            xs = xb[:, blk * LANES:(blk + 1) * LANES]
            outs.append(jnp.dot(xs, before, preferred_element_type=F32) + carry)
            carry = carry + jnp.dot(xs, ones, preferred_element_type=F32)
        return jnp.concatenate(outs, axis=1)

    chosen = above | (tied & (prefix(tied.astype(F32)) < need))
    slot = prefix(chosen.astype(F32)).astype(jnp.int32)
    slot_ref[...] = jnp.where(chosen, slot, n)


def _route(aff, row0, n, bsz, cap):
    ne = aff.shape[1]
    at = jnp.swapaxes(aff[row0:row0 + bsz * n].reshape(bsz, n, ne), 1, 2)
    spec = pl.BlockSpec((None, ne, n), lambda b: (b, 0, 0))
    slot = pl.pallas_call(
        functools.partial(_route_kernel, n=n, ne=ne, cap=cap),
        grid=(bsz,), in_specs=[spec], out_specs=spec,
        out_shape=jax.ShapeDtypeStruct((bsz, ne, n), jnp.int32),
        compiler_params=_params("arbitrary"), name="expert_route",
    )(at)
    return at, slot


def _gather_kernel(slot_ref, afft_ref, h_ref, xg_ref, g_ref, *, cap, eg):
    e0 = pl.program_id(1) * eg
    want = lax.broadcasted_iota(jnp.int32, (cap, 1), 0)
    sels = [slot_ref[pl.ds(e0 + k, 1), :] == want for k in range(eg)]
    sel = sels[0].astype(BF16) if eg == 1 else jnp.concatenate([s.astype(BF16) for s in sels], axis=0)
    xg = jnp.dot(sel, h_ref[...], preferred_element_type=F32).astype(xg_ref.dtype)
    xg_ref[...] = xg.reshape(xg_ref.shape)
    for k in range(eg):
        g_ref[k] = jnp.sum(jnp.where(sels[k], afft_ref[pl.ds(e0 + k, 1), :], 0.0), axis=1, keepdims=True)


def _gather(slot, afft, h2, row_block0, n, bsz, cap):
    ne = slot.shape[1]
    d = h2.shape[1]
    eg = max(1, min(ne, GATHER_ROWS // cap))
    while ne % eg:
        eg -= 1
    row_spec = pl.BlockSpec((None, ne, n), lambda b, e: (b, 0, 0))
    return pl.pallas_call(
        functools.partial(_gather_kernel, cap=cap, eg=eg),
        grid=(bsz, ne // eg),
        in_specs=[row_spec, row_spec, pl.BlockSpec((n, d), lambda b, e: (row_block0 + b, 0))],
        out_specs=[pl.BlockSpec((eg, cap, d), lambda b, e: (e, b, 0)),
                   pl.BlockSpec((eg, cap, 1), lambda b, e: (e, b, 0))],
        out_shape=[jax.ShapeDtypeStruct((ne, bsz * cap, d), BF16),
                   jax.ShapeDtypeStruct((ne, bsz * cap, 1), F32)],
        compiler_params=_params("arbitrary", "arbitrary"), name="expert_gather",
    )(slot, afft, h2)


def _ffn_kernel(*refs, ns, nf):
    x_refs = refs[0:2 * ns:2]
    g_refs = refs[1:2 * ns:2]
    wg_ref, wu_ref, wd_ref = refs[2 * ns:2 * ns + 3]
    o_refs = refs[2 * ns + 3:3 * ns + 3]
    wgb, wub, wdb = refs[3 * ns + 3:3 * ns + 6]
    acc_refs = refs[3 * ns + 6:]
    f = pl.program_id(1)
    wgb[...] = wg_ref[...].astype(BF16)
    wub[...] = wu_ref[...].astype(BF16)
    wdb[...] = wd_ref[...].astype(BF16)

    def ff_tile(first, last):
        for k in range(ns):
            rows = x_refs[k].shape[0]
            step = min(FFN_ROWS, rows)
            for r0 in range(0, rows, step):
                rs = slice(r0, r0 + step)
                x = x_refs[k][rs, :]
                hg = jnp.dot(x, wgb[...], preferred_element_type=F32)
                hu = jnp.dot(x, wub[...], preferred_element_type=F32)
                hid = (hg * _sigmoid(hg) * hu).astype(BF16)
                part = jnp.dot(hid, wdb[...], preferred_element_type=F32)
                if not first:
                    part = acc_refs[k][rs, :] + part
                if last:
                    o_refs[k][rs, :] = (part * g_refs[k][rs, :]).astype(o_refs[k].dtype)
                else:
                    acc_refs[k][rs, :] = part

    if nf == 1:
        ff_tile(True, True)
    else:
        pl.when(f == 0)(functools.partial(ff_tile, True, False))
        pl.when(f == nf - 1)(functools.partial(ff_tile, False, True))
        if nf > 2:
            pl.when((f > 0) & (f < nf - 1))(functools.partial(ff_tile, False, False))


def _expert_ffn(xgs, gates, w_gate, w_up, w_down, layer):
    ns = len(xgs)
    _, ne, d, ff = w_gate.shape
    tf = _lane_tile(ff, FF_TILE)
    nf = ff // tf
    in_specs, out_specs, out_shape, acc_scr, args = [], [], [], [], []
    for k in range(ns):
        rows = xgs[k].shape[1]
        in_specs.append(pl.BlockSpec((None, rows, d), lambda e, f: (e, 0, 0)))
        in_specs.append(pl.BlockSpec((None, rows, 1), lambda e, f: (e, 0, 0)))
        out_specs.append(pl.BlockSpec((None, rows, d), lambda e, f: (e, 0, 0)))
        out_shape.append(jax.ShapeDtypeStruct((ne, rows, d), BF16))
        acc_scr.append(pltpu.VMEM((rows, d), F32))
        args += [xgs[k], gates[k]]
    in_specs += [pl.BlockSpec((None, None, d, tf), lambda e, f: (layer, e, 0, f)),
                 pl.BlockSpec((None, None, d, tf), lambda e, f: (layer, e, 0, f)),
                 pl.BlockSpec((None, None, tf, d), lambda e, f: (layer, e, f, 0))]
    return pl.pallas_call(
        functools.partial(_ffn_kernel, ns=ns, nf=nf),
        grid=(ne, nf),
        in_specs=in_specs, out_specs=out_specs, out_shape=out_shape,
        scratch_shapes=[pltpu.VMEM((d, tf), BF16), pltpu.VMEM((d, tf), BF16), pltpu.VMEM((tf, d), BF16)] + acc_scr,
        compiler_params=_params("arbitrary", "arbitrary"), name="expert_ffn",
    )(*args, w_gate, w_up, w_down)


def _scatter_kernel(slot_ref, y_ref, x_ref, mod_ref, o_ref, *, ne, cap):
    sl = slot_ref[...]
    want = lax.broadcasted_iota(jnp.int32, (1, cap), 1)
    acc = None
    for e in range(ne):
        sel = (sl[:, e:e + 1] == want).astype(BF16)
        p = jnp.dot(sel, y_ref[e], preferred_element_type=F32)
        acc = p if acc is None else acc + p
    o_ref[...] = x_ref[...] + mod_ref[5:6, :] * acc


def _scatter(slot, y, x, mod, layer, mod_row0, row0, n, bsz, cap, tn):
    ne = slot.shape[1]
    d = x.shape[1]
    tm = min(SCATTER_ROWS, n)
    slot_t = jnp.swapaxes(slot, 1, 2)
    xb0 = row0 // tm
    per_seq = n // tm
    return pl.pallas_call(
        functools.partial(_scatter_kernel, ne=ne, cap=cap),
        grid=(bsz, d // tn, per_seq),
        in_specs=[pl.BlockSpec((None, tm, ne), lambda b, j, i: (b, i, 0)),
                  pl.BlockSpec((ne, cap, tn), lambda b, j, i: (0, b, j)),
                  pl.BlockSpec((tm, tn), lambda b, j, i: (xb0 + b * per_seq + i, j)),
                  pl.BlockSpec((None, None, 6, tn), lambda b, j, i: (layer, mod_row0(b), 0, j))],
        out_specs=pl.BlockSpec((tm, tn), lambda b, j, i: (b * per_seq + i, j)),
        out_shape=jax.ShapeDtypeStruct((bsz * n, d), F32),
        compiler_params=_params("arbitrary", "arbitrary", "arbitrary"), name="expert_scatter",
    )(slot_t, y, x, mod)


def kernel(x, c, ctx, c_ctx, w_mod, b_mod, g_norm1, w_in, lb_param, g_hgrn, w_pool, pool_scale, rpb,
           w_branch, w_out, g_norm2, w_router, w_gate_e, w_up_e, w_down_e, g_final):
    bsz, n, d = x.shape
    lc = ctx.shape[1]
    depth = w_mod.shape[0]
    width = lb_param.shape[2]
    heads = width // HEAD_DIM
    ne = w_router.shape[2]
    n_lat = bsz * n
    n_ctx = bsz * lc
    total = n_lat + n_ctx
    assert bsz + 1 <= 8 and n % ROW_TILE == 0 and lc % ROW_TILE == 0 and n_lat % lc == 0
    assert w_pool.shape[-1] * 4 == width and w_branch.shape[2] == width
    tm = _row_tile(n, n_ctx)
    tn = _lane_tile(d, MERGE_COL_TILE)
    tn_wide = _lane_tile(d, WIDE_COL_TILE)

    lb_all = jnp.cumsum(jax.nn.softmax(lb_param.astype(F32), axis=1), axis=1)
    lb_all = lb_all - lb_all[:, :1]
    lbp_all = jnp.stack([jnp.log(lb_all), jnp.log1p(-lb_all)], axis=2) * LOG2E
    lbp_all = lbp_all.reshape(2, depth, 2, heads, LANES).transpose(0, 1, 3, 2, 4)

    c8 = jnp.concatenate([c, c_ctx[None], jnp.zeros((8 - bsz - 1, d), F32)], axis=0)
    mod = _modulation(c8, w_mod, b_mod).reshape(depth, 8, 6, d)

    cm3, pair_masks, upper = _hgrn_constants()
    hgrn_consts = (jnp.asarray(cm3, BF16), jnp.asarray(pair_masks, F32), jnp.asarray(upper, F32))
    na_layout = _na_block_layout(n // GRID_W)
    na_bias = _na_bias_table(rpb)
    lanes_e = -(-ne // LANES) * LANES
    w_router_p = jnp.pad(w_router, ((0, 0), (0, 0), (0, lanes_e - ne)))

    x_lat = x.reshape(n_lat, d)
    x_ctx = ctx.reshape(n_ctx, d)
    zero_state = jnp.zeros((bsz, heads, LANES, LANES), F32)
    cap = EC_CAPACITY * n // ne
    cap_c = EC_CAPACITY * lc // ne
    a_cols = 5 * width
    b_cols = w_in.shape[2] - a_cols
    tn_in = _lane_tile(math.gcd(a_cols, b_cols), WIDE_COL_TILE)
    gate_col0 = 4 * width
    rt_all = _Rows(tm, n, bsz, n_lat, total)
    rt_lat = _Rows(tm, n, bsz, n_lat, n_lat)
    tm_norm = min(NORM_ROWS, tm)
    rn_all = _Rows(tm_norm, n, bsz, n_lat, total)
    rn_lat = _Rows(tm_norm, n, bsz, n_lat, n_lat)
    tm_out = max(tm // 2, ROW_TILE)
    ro_all = _Rows(tm_out, n, bsz, n_lat, total)
    ro_lat = _Rows(tm_out, n, bsz, n_lat, n_lat)

    for l in range(depth):
        last = l == depth - 1
        rt, rn, ro = (rt_lat, rn_lat, ro_lat) if last else (rt_all, rn_all, ro_all)
        gain_h = g_hgrn[l].reshape(heads, 1, LANES)

        h = _norm1(x_lat, x_ctx, g_norm1[l], mod, l, rn_all)
        ua = _matmul(h, w_in, l, 0, a_cols, F32, tm, tn_in)
        ub = _matmul(h, w_in, l, a_cols, b_cols, BF16, tm, tn_in)

        cy_a, s_f, s_b = _hgrn(ua, n_lat // lc, lc, bsz, heads, lbp_all[:, l], gain_h, zero_state, zero_state,
                               hgrn_consts)
        y_a, _, _ = _hgrn(ua, 0, n, bsz, heads, lbp_all[:, l], gain_h, s_f, s_b, hgrn_consts)
        y_b = _pool(ub, 0, n, bsz, w_pool, pool_scale, l)
        y_c, cy_c = _attention(ub, width, n, lc, bsz, heads, na_bias, l, na_layout)
        ys_lat = (y_a, y_b, y_c)
        ys_ctx = ys_lat if last else (cy_a, _pool(ub, n_lat // lc, lc, bsz, w_pool, pool_scale, l), cy_c)

        merged = _merge(ys_lat, ys_ctx, ub, gate_col0, w_branch, l, rt, tn)
        x_mid = _out_proj(merged, w_out, l, x_lat, x_ctx, mod, ro, tn_wide)

        h2, aff = _norm2(x_mid, g_norm2[l], mod, l, rn, w_router_p, ne)
        afft, slot = _route(aff, 0, n, bsz, cap)
        xg, gate = _gather(slot, afft, h2, 0, n, bsz, cap)
        xgs, gates = [xg], [gate]
        if not last:
            afft_c, slot_c = _route(aff, n_lat, lc, bsz, cap_c)
            xg_c, gate_c = _gather(slot_c, afft_c, h2, n_lat // lc, lc, bsz, cap_c)
            xgs, gates = xgs + [xg_c], gates + [gate_c]
        ys = _expert_ffn(xgs, gates, w_gate_e, w_up_e, w_down_e, l)
        x_lat = _scatter(slot, ys[0], x_mid, mod, l, lambda b: b, 0, n, bsz, cap, tn_wide)
        if not last:
            x_ctx = _scatter(slot_c, ys[1], x_mid, mod, l, lambda b: bsz, n_lat, lc, bsz, cap_c, tn_wide)

    return _final_norm(x_lat, g_final, tm_norm).reshape(bsz, n, d)
```

```python
import functools
import math

import numpy as np
import jax
import jax.numpy as jnp
from jax import lax
from jax.experimental import pallas as pl
from jax.experimental.pallas import tpu as pltpu

F32 = jnp.float32
BF16 = jnp.bfloat16
EPS = 1e-6
LANES = 128
HEAD_DIM = 128
CHUNK = 64
HGRN_GROUP = 4
GRID_W = 64
POOL_WINDOWS = (2, 4, 8, 16)
POOL_PAD = 16
NA_ROWS_MAX = 8
NA_COLS = 16
NA_QROWS = 4
NA_KROWS = 12
ROPE_THETA = 10000.0
EC_CAPACITY = 2
GATHER_ROWS = 512
FFN_ROWS = 512
NEG = -1e30
ROW_TILE = 256
NORM_ROWS = 512
MAX_ROW_TILE = 1024
WIDE_COL_TILE = 1024
MERGE_COL_TILE = 512
FF_TILE = 256
SCATTER_ROWS = 512
F32_SIGN_BIT = 0x80000000
F32_VALUE_BITS = 31
LOG2E = 1.4426950408889634
VMEM_LIMIT = 56 * 2 ** 20


def _params(*sem):
    return pltpu.CompilerParams(dimension_semantics=sem, vmem_limit_bytes=VMEM_LIMIT)


def _sigmoid(x):
    return 1.0 / (1.0 + jnp.exp(-x))


def _neg_abs(x):
    bits = lax.bitcast_convert_type(x, jnp.uint32) | jnp.uint32(F32_SIGN_BIT)
    return lax.bitcast_convert_type(bits, F32)


def _lane_tile(n, limit):
    t = min(limit, n) // LANES * LANES
    while n % t:
        t -= LANES
    return t


def _row_tile(n, ctx_rows):
    tm = MAX_ROW_TILE
    while n % tm or ctx_rows % tm:
        tm //= 2
    return tm


class _Rows:
    def __init__(self, tm, n, bsz, n_lat, rows):
        self.tm, self.n, self.bsz = tm, n, bsz
        self.lat_tiles = n_lat // tm
        self.tiles = rows // tm
        self.ctx_tiles = max(self.tiles - self.lat_tiles, 1)

    def lat(self, i):
        return jnp.minimum(i, self.lat_tiles - 1)

    def ctx(self, i):
        return jnp.clip(i - self.lat_tiles, 0, self.ctx_tiles - 1)

    def mod_row(self, i):
        return jnp.minimum(i * self.tm // self.n, self.bsz)


def _pick(is_lat, lat_ref, ctx_ref):
    return jnp.where(is_lat, lat_ref[...], ctx_ref[...])


def _mod_kernel(c_ref, w_ref, b_ref, o_ref):
    c = c_ref[...]
    sc = (c * _sigmoid(c)).astype(BF16)
    o_ref[...] = jnp.dot(sc, w_ref[...].astype(BF16), preferred_element_type=F32) + b_ref[...]


def _modulation(c8, w_mod, b_mod):
    depth, d, n6 = w_mod.shape
    tn = _lane_tile(n6, WIDE_COL_TILE)
    return pl.pallas_call(
        _mod_kernel,
        grid=(depth, n6 // tn),
        in_specs=[pl.BlockSpec((8, d), lambda l, j: (0, 0)),
                  pl.BlockSpec((None, d, tn), lambda l, j: (l, 0, j)),
                  pl.BlockSpec((None, 1, tn), lambda l, j: (l, 0, j))],
        out_specs=pl.BlockSpec((None, 8, tn), lambda l, j: (l, 0, j)),
        out_shape=jax.ShapeDtypeStruct((depth, 8, n6), F32),
        compiler_params=_params("arbitrary", "arbitrary"),
        name="modulation",
    )(c8, w_mod, b_mod.reshape(depth, 1, n6))


def _rmsnorm(x, gain):
    return x * lax.rsqrt(jnp.mean(x * x, axis=-1, keepdims=True) + EPS) * gain


def _modulate(y, mod_ref, shift_idx):
    return y * (1.0 + mod_ref[shift_idx + 1:shift_idx + 2, :]) + mod_ref[shift_idx:shift_idx + 1, :]


def _norm1_kernel(xl_ref, xc_ref, g_ref, mod_ref, o_ref, *, lat_tiles):
    x = _pick(pl.program_id(0) < lat_tiles, xl_ref, xc_ref)
    o_ref[...] = _modulate(_rmsnorm(x, g_ref[...]), mod_ref, 0).astype(o_ref.dtype)


def _norm2_kernel(x_ref, g_ref, mod_ref, wr_ref, o_ref, aff_ref):
    h = _modulate(_rmsnorm(x_ref[...], g_ref[...]), mod_ref, 3)
    hb = h.astype(BF16)
    o_ref[...] = hb
    h_lo = (h - hb.astype(F32)).astype(BF16)
    w = wr_ref[...]
    wb = w.astype(BF16)
    w_lo = (w - wb.astype(F32)).astype(BF16)
    logits = (jnp.dot(hb, wb, preferred_element_type=F32) + jnp.dot(h_lo, wb, preferred_element_type=F32)
              + jnp.dot(hb, w_lo, preferred_element_type=F32))
    logits = logits[:, :aff_ref.shape[1]]
    e = jnp.exp(logits - jnp.max(logits, axis=-1, keepdims=True))
    aff_ref[...] = e / jnp.sum(e, axis=-1, keepdims=True)


def _final_norm_kernel(x_ref, g_ref, o_ref):
    o_ref[...] = _rmsnorm(x_ref[...], g_ref[...])


def _norm1(x_lat, x_ctx, gain, mod, layer, rt):
    d = x_lat.shape[1]
    return pl.pallas_call(
        functools.partial(_norm1_kernel, lat_tiles=rt.lat_tiles),
        grid=(rt.tiles,),
        in_specs=[pl.BlockSpec((rt.tm, d), lambda i: (rt.lat(i), 0)),
                  pl.BlockSpec((rt.tm, d), lambda i: (rt.ctx(i), 0)),
                  pl.BlockSpec((1, d), lambda i: (0, 0)),
                  pl.BlockSpec((None, None, 6, d), lambda i: (layer, rt.mod_row(i), 0, 0))],
        out_specs=pl.BlockSpec((rt.tm, d), lambda i: (i, 0)),
        out_shape=jax.ShapeDtypeStruct((rt.tiles * rt.tm, d), BF16),
        compiler_params=_params("arbitrary"), name="norm1",
    )(x_lat, x_ctx, gain.reshape(1, d), mod)


def _norm2(x, gain, mod, layer, rt, w_router_padded, ne):
    d = x.shape[1]
    rows = rt.tiles * rt.tm
    return pl.pallas_call(
        _norm2_kernel,
        grid=(rt.tiles,),
        in_specs=[pl.BlockSpec((rt.tm, d), lambda i: (i, 0)),
                  pl.BlockSpec((1, d), lambda i: (0, 0)),
                  pl.BlockSpec((None, None, 6, d), lambda i: (layer, rt.mod_row(i), 0, 0)),
                  pl.BlockSpec((None, d, w_router_padded.shape[2]), lambda i: (layer, 0, 0))],
        out_specs=[pl.BlockSpec((rt.tm, d), lambda i: (i, 0)), pl.BlockSpec((rt.tm, ne), lambda i: (i, 0))],
        out_shape=[jax.ShapeDtypeStruct((rows, d), BF16), jax.ShapeDtypeStruct((rows, ne), F32)],
        compiler_params=_params("arbitrary"), name="norm2_router",
    )(x, gain.reshape(1, d), mod, w_router_padded)


def _final_norm(x, gain, tm):
    rows, d = x.shape
    return pl.pallas_call(
        _final_norm_kernel,
        grid=(rows // tm,),
        in_specs=[pl.BlockSpec((tm, d), lambda i: (i, 0)), pl.BlockSpec((1, d), lambda i: (0, 0))],
        out_specs=pl.BlockSpec((tm, d), lambda i: (i, 0)),
        out_shape=jax.ShapeDtypeStruct((rows, d), F32),
        compiler_params=_params("arbitrary"), name="final_norm",
    )(x, gain.reshape(1, d))


def _mm_kernel(a_ref, w_ref, o_ref, wbf_ref):
    @pl.when(pl.program_id(1) == 0)
    def _():
        wbf_ref[...] = w_ref[...].astype(BF16)
    o_ref[...] = jnp.dot(a_ref[...], wbf_ref[...], preferred_element_type=F32).astype(o_ref.dtype)


def _matmul(a, w_all, layer, col0, ncols, out_dtype, tm, tn):
    m, k = a.shape
    off = col0 // tn
    return pl.pallas_call(
        _mm_kernel,
        grid=(ncols // tn, m // tm),
        in_specs=[pl.BlockSpec((tm, k), lambda j, i: (i, 0)),
                  pl.BlockSpec((None, k, tn), lambda j, i: (layer, 0, j + off))],
        out_specs=pl.BlockSpec((tm, tn), lambda j, i: (i, j)),
        out_shape=jax.ShapeDtypeStruct((m, ncols), out_dtype),
        scratch_shapes=[pltpu.VMEM((k, tn), BF16)],
        compiler_params=_params("arbitrary", "arbitrary"), name="matmul",
    )(a, w_all)


def _out_proj_kernel(a_ref, w_ref, xl_ref, xc_ref, mod_ref, o_ref, wbf_ref, *, lat_tiles):
    @pl.when(pl.program_id(1) == 0)
    def _():
        wbf_ref[...] = w_ref[...].astype(BF16)
    acc = jnp.dot(a_ref[...], wbf_ref[...], preferred_element_type=F32)
    x = _pick(pl.program_id(1) < lat_tiles, xl_ref, xc_ref)
    o_ref[...] = x + mod_ref[2:3, :] * acc


def _out_proj(a, w_all, layer, x_lat, x_ctx, mod, rt, tn):
    k = a.shape[1]
    d = w_all.shape[2]
    tm = rt.tm
    return pl.pallas_call(
        functools.partial(_out_proj_kernel, lat_tiles=rt.lat_tiles),
        grid=(d // tn, rt.tiles),
        in_specs=[pl.BlockSpec((tm, k), lambda j, i: (i, 0)),
                  pl.BlockSpec((None, k, tn), lambda j, i: (layer, 0, j)),
                  pl.BlockSpec((tm, tn), lambda j, i: (rt.lat(i), j)),
                  pl.BlockSpec((tm, tn), lambda j, i: (rt.ctx(i), j)),
                  pl.BlockSpec((None, None, 6, tn), lambda j, i: (layer, rt.mod_row(i), 0, j))],
        out_specs=pl.BlockSpec((tm, tn), lambda j, i: (i, j)),
        out_shape=jax.ShapeDtypeStruct((rt.tiles * tm, d), F32),
        scratch_shapes=[pltpu.VMEM((k, tn), BF16)],
        compiler_params=_params("arbitrary", "arbitrary"), name="out_proj",
    )(a, w_all, x_lat, x_ctx, mod)


def _merge_kernel(ya_ref, yb_ref, yc_ref, cya_ref, cyb_ref, cyc_ref, g0_ref, g1_ref, g2_ref, w_ref, o_ref, wbf_ref,
                  *, lat_tiles):
    @pl.when(pl.program_id(1) == 0)
    def _():
        wbf_ref[...] = w_ref[...].astype(BF16)
    is_lat = pl.program_id(1) < lat_tiles
    acc = None
    for j, (y_ref, cy_ref, g_ref) in enumerate(((ya_ref, cya_ref, g0_ref), (yb_ref, cyb_ref, g1_ref),
                                                (yc_ref, cyc_ref, g2_ref))):
        p = jnp.dot(_pick(is_lat, y_ref, cy_ref), wbf_ref[j], preferred_element_type=F32)
        t = _sigmoid(g_ref[...].astype(F32)) * p
        acc = t if acc is None else acc + t
    o_ref[...] = acc.astype(o_ref.dtype)


def _merge(ys_lat, ys_ctx, ub, gate_col0, w_branch, layer, rt, tn):
    bw = ys_lat[0].shape[1]
    d = w_branch.shape[3]
    tm = rt.tm
    lat_spec = pl.BlockSpec((tm, bw), lambda j, i: (rt.lat(i), 0))
    ctx_spec = pl.BlockSpec((tm, bw), lambda j, i: (rt.ctx(i), 0))

    def gate_spec(k):
        off = (gate_col0 + k * d) // tn
        return pl.BlockSpec((tm, tn), lambda j, i: (i, off + j))

    return pl.pallas_call(
        functools.partial(_merge_kernel, lat_tiles=rt.lat_tiles),
        grid=(d // tn, rt.tiles),
        in_specs=[lat_spec] * 3 + [ctx_spec] * 3 + [gate_spec(0), gate_spec(1), gate_spec(2),
                  pl.BlockSpec((None, 3, bw, tn), lambda j, i: (layer, 0, 0, j))],
        out_specs=pl.BlockSpec((tm, tn), lambda j, i: (i, j)),
        out_shape=jax.ShapeDtypeStruct((rt.tiles * tm, d), BF16),
        scratch_shapes=[pltpu.VMEM((3, bw, tn), BF16)],
        compiler_params=_params("arbitrary", "arbitrary"), name="merge",
    )(*ys_lat, *ys_ctx, ub, ub, ub, w_branch)


_HGRN_BLOCKS = (32, 16, 8, 4, 2, 1)


def _hgrn_constants():
    c = CHUNK
    t = np.arange(c)
    tri_f = (t[:, None] >= t[None, :]).astype(np.float32)
    tri_b = np.ascontiguousarray(tri_f[::-1, ::-1])
    sel_f, sel_b, masks, upper = [tri_f], [tri_b], [], []
    for m in _HGRN_BLOCKS:
        blk = (t // (2 * m)) * (2 * m)
        up = (t % (2 * m)) >= m
        sel_f.append(tri_f - tri_f[blk + m - 1])
        sel_b.append(tri_b - tri_b[blk + m])
        same = (t[:, None] // (2 * m)) == (t[None, :] // (2 * m))
        masks.append(same & (up[:, None] != up[None, :]))
        upper.append(np.broadcast_to(up[:, None], (c, LANES)))
    masks.append(np.eye(c, dtype=bool))
    sel = np.stack([np.concatenate(sel_f, axis=0), np.concatenate(sel_b, axis=0)])
    return (np.concatenate([sel, sel, sel], axis=2), np.stack(masks).astype(np.float32),
            np.stack(upper).astype(np.float32))


def _hgrn_log2_gate(z, lbp):
    zl = z * LOG2E
    log_sig = jnp.minimum(zl, 0.0) - jnp.log2(1.0 + jnp.exp2(_neg_abs(zl)))
    a = lbp[0:1, :]
    cc = lbp[1:2, :] + log_sig
    return jnp.maximum(a, cc) + jnp.log2(1.0 + jnp.exp2(_neg_abs(a - cc)))


def _split3(x):
    g1 = x.astype(BF16)
    r1 = x - g1.astype(F32)
    g2 = r1.astype(BF16)
    g3 = (r1 - g2.astype(F32)).astype(BF16)
    return jnp.concatenate([g1, g2, g3], axis=0)


_NT = (((1,), (1,)), ((), ()))
_TN = (((0,), (0,)), ((), ()))


def _hgrn_kernel(q_ref, ff_ref, fb_ref, i_ref, g_ref, lbf_ref, lbb_ref, gain_ref, cm_ref, mask_ref, up_ref,
                 sf0_ref, sb0_ref, y_ref, sf_ref, sb_ref, o_scr, qt_scr, upd_scr, dec_scr, st_scr, sums_scr, k_scr,
                 *, nc):
    z_refs = (ff_ref, fb_ref)
    lb_refs = (lbf_ref, lbb_ref)
    nl = len(_HGRN_BLOCKS)
    last_row = (CHUNK - 1, 0)

    groups = nc // HGRN_GROUP
    srows = (1 + nl) * CHUNK

    def gate_pass(it, carry):
        rows = [pl.ds(pl.multiple_of((it * HGRN_GROUP + s) * CHUNK, CHUNK), CHUNK) for s in range(HGRN_GROUP)]
        for d in range(2):
            logf = [_hgrn_log2_gate(z_refs[d][r, :], lb_refs[d][...]) for r in rows]
            sums_scr[d, pl.ds(pl.multiple_of(it * srows, srows), srows), :] = jnp.dot(
                cm_ref[d], jnp.concatenate([_split3(lf) for lf in logf], axis=1), preferred_element_type=F32)
            for r, lf in zip(rows, logf):
                k_scr[d, r, :] = 1.0 - jnp.exp2(lf)
        return carry

    lax.fori_loop(0, groups, gate_pass, 0, unroll=min(4, groups))

    def local_pass(it, carry):
        sums = [sums_scr[d, pl.ds(pl.multiple_of(it * srows, srows), srows), :] for d in range(2)]
        rows = [pl.ds(pl.multiple_of((it * HGRN_GROUP + s) * CHUNK, CHUNK), CHUNK) for s in range(HGRN_GROUP)]
        up = up_ref[...] > 0.5
        for s, r in enumerate(rows):
            lanes = slice(s * LANES, (s + 1) * LANES)
            qr = q_ref[r, :]
            q = qr * _sigmoid(qr)
            vb = i_ref[r, :].astype(BF16)
            b = [sums[d][0:CHUNK, lanes] for d in range(2)]
            lvl = [jnp.exp2(_neg_abs(sums[d][CHUNK:, lanes].reshape(nl, CHUNK, LANES))) for d in range(2)]
            k = [k_scr[d, r, :] for d in range(2)]
            q_dec = jnp.where(up, lvl[0], lvl[1])
            k_dec = jnp.where(up, k[1][None] * lvl[1], k[0][None] * lvl[0])
            qs = jnp.concatenate([q[None] * q_dec, q[None]], axis=0).astype(BF16)
            ks = jnp.concatenate([k_dec, (k[0] + k[1])[None]], axis=0).astype(BF16)
            sc = jnp.einsum("ltk,lsk->lts", qs, ks, preferred_element_type=F32)
            amat = jnp.sum(sc * mask_ref[...], axis=0)
            o_scr[r, :] = jnp.dot(amat.astype(BF16), vb, preferred_element_type=F32)
            tot = [b[d][last_row[d]:last_row[d] + 1, :] for d in range(2)]
            kd = jnp.concatenate([(k[d] * jnp.exp2(tot[d] - b[d])).astype(BF16) for d in range(2)], axis=1)
            upd = lax.dot_general(vb, kd, _TN, preferred_element_type=F32)
            j = it * HGRN_GROUP + s
            for d in range(2):
                qt_scr[d, r, :] = (q * jnp.exp2(b[d])).astype(BF16)
                upd_scr[d, j] = upd[:, d * LANES:(d + 1) * LANES]
                dec_scr[d, pl.ds(j, 1), :] = jnp.exp2(tot[d])
        return carry

    lax.fori_loop(0, groups, local_pass, 0, unroll=min(4, groups))

    st_scr[0] = sf0_ref[...]
    st_scr[1] = sb0_ref[...]

    def state_pass(it, carry):
        for d in range(2):
            j = it if d == 0 else nc - 1 - it
            rows = pl.ds(pl.multiple_of(j * CHUNK, CHUNK), CHUNK)
            st = st_scr[d]
            o_scr[rows, :] += lax.dot_general(qt_scr[d, rows, :], st.astype(BF16), _NT, preferred_element_type=F32)
            st_scr[d] = st * dec_scr[d, pl.ds(j, 1), :] + upd_scr[d, j]
        return carry

    lax.fori_loop(0, nc, state_pass, 0, unroll=min(8, nc))
    sf_ref[...] = st_scr[0]
    sb_ref[...] = st_scr[1]
    o = o_scr[...]
    o = o * lax.rsqrt(jnp.mean(o * o, axis=-1, keepdims=True) + EPS) * gain_ref[...]
    g = g_ref[...]
    y_ref[...] = (o * (g * _sigmoid(g))).astype(y_ref.dtype)


def _hgrn(ua, row_block0, n, bsz, heads, lbp, gain, sf0, sb0, consts):
    nc = n // CHUNK
    assert nc % HGRN_GROUP == 0 and nc % 2 == 0

    def col(k):
        return pl.BlockSpec((n, LANES), lambda b, h: (row_block0 + b, k * heads + h))

    lb_spec = lambda d: pl.BlockSpec((None, None, 2, LANES), lambda b, h: (d, h, 0, 0))
    st_spec = pl.BlockSpec((None, None, LANES, LANES), lambda b, h: (b, h, 0, 0))
    st_shape = jax.ShapeDtypeStruct((bsz, heads, LANES, LANES), F32)
    const_specs = [pl.BlockSpec(a.shape, lambda b, h, nd=a.ndim: (0,) * nd) for a in consts]
    return pl.pallas_call(
        functools.partial(_hgrn_kernel, nc=nc),
        grid=(bsz, heads),
        in_specs=[col(0), col(1), col(2), col(3), col(4), lb_spec(0), lb_spec(1),
                  pl.BlockSpec((None, 1, LANES), lambda b, h: (h, 0, 0))] + const_specs + [st_spec, st_spec],
        out_specs=[pl.BlockSpec((n, LANES), lambda b, h: (b, h)), st_spec, st_spec],
        out_shape=[jax.ShapeDtypeStruct((bsz * n, heads * LANES), BF16), st_shape, st_shape],
        scratch_shapes=[pltpu.VMEM((n, LANES), F32), pltpu.VMEM((2, n, LANES), BF16),
                        pltpu.VMEM((2, nc, LANES, LANES), F32), pltpu.VMEM((2, nc, LANES), F32),
                        pltpu.VMEM((2, LANES, LANES), F32),
                        pltpu.VMEM((2, nc * (1 + len(_HGRN_BLOCKS)) * CHUNK // HGRN_GROUP, HGRN_GROUP * LANES), F32),
                        pltpu.VMEM((2, n, LANES), F32)],
        compiler_params=_params("arbitrary", "arbitrary"), name="hgrn2",
    )(ua, ua, ua, ua, ua, lbp, lbp, gain, *consts, sf0, sb0)


def _pool_kernel(u_ref, w_ref, s_ref, y_ref, pad_ref, *, n, group):
    pos = lax.broadcasted_iota(jnp.int32, (n, 1), 0)
    zeros = jnp.zeros((POOL_PAD, pad_ref.shape[1]), F32)
    pad_ref[0:POOL_PAD, :] = zeros
    pad_ref[POOL_PAD + n:2 * POOL_PAD + n, :] = zeros
    pad_ref[POOL_PAD:POOL_PAD + n, :] = u_ref[...].astype(F32)
    for gi, w in enumerate(POOL_WINDOWS):
        cols = slice(gi * group, (gi + 1) * group)
        acc = None
        for dlt in range(-(w // 2), w // 2):
            t = pad_ref[POOL_PAD + dlt:POOL_PAD + dlt + n, cols]
            acc = t if acc is None else acc + t
        lo = jnp.maximum(pos - w // 2, 0)
        hi = jnp.minimum(pos + w // 2 - 1, n - 1)
        cnt = (hi - lo + 1).astype(F32)
        dd = acc / cnt - pad_ref[POOL_PAD:POOL_PAD + n, cols]
        y = jnp.dot(dd.astype(BF16), w_ref[gi].astype(BF16), preferred_element_type=F32)
        y_ref[:, cols] = (y * s_ref[:, cols]).astype(y_ref.dtype)


def _pool(ub, row_block0, n, bsz, w_pool, scale, layer):
    group = w_pool.shape[-1]
    width = 4 * group
    return pl.pallas_call(
        functools.partial(_pool_kernel, n=n, group=group),
        grid=(bsz,),
        in_specs=[pl.BlockSpec((n, width), lambda b: (row_block0 + b, 0)),
                  pl.BlockSpec((None, 4, group, group), lambda b: (layer, 0, 0, 0)),
                  pl.BlockSpec((None, 1, width), lambda b: (layer, 0, 0))],
        out_specs=pl.BlockSpec((n, width), lambda b: (b, 0)),
        out_shape=jax.ShapeDtypeStruct((bsz * n, width), BF16),
        scratch_shapes=[pltpu.VMEM((n + 2 * POOL_PAD, width), F32)],
        compiler_params=_params("arbitrary"), name="pool",
    )(ub, w_pool, scale.reshape(scale.shape[0], 1, width))


def _na_tables(n):
    pos = np.arange(n)
    half = HEAD_DIM // 2
    inv_freq = ROPE_THETA ** (-np.arange(0, half, 2, dtype=np.float64) / half)
    lane = np.arange(HEAD_DIM)
    p = np.where(lane[None, :] < half, (pos // GRID_W)[:, None], (pos % GRID_W)[:, None]).astype(np.float64)
    ang = p * inv_freq[lane % (half // 2)][None, :]
    sign = np.where((lane % half) < half // 2, -1.0, 1.0)[None, :]
    return np.cos(ang).astype(np.float32), (np.sin(ang) * sign).astype(np.float32)


def _na_block_layout(rows):
    kr = NA_ROWS_MAX
    nblk = rows // NA_QROWS
    starts, patterns, types = [], [], []
    for j in range(nblk):
        u = int(np.clip(NA_QROWS * j - kr // 2, 0, rows - NA_KROWS))
        r = NA_QROWS * j + np.arange(NA_QROWS)
        start_r = np.clip(r - kr // 2, 0, rows - kr)
        kabs = u + np.arange(NA_KROWS)
        valid = (kabs[None, :] >= start_r[:, None]) & (kabs[None, :] < start_r[:, None] + kr)
        assert valid.sum(axis=1).min() == kr
        dr = np.clip(kabs[None, :] - r[:, None] + NA_ROWS_MAX - 1, 0, 2 * NA_ROWS_MAX - 2)
        key = (valid.tobytes(), dr.tobytes())
        keys = [p[0] for p in patterns]
        if key not in keys:
            patterns.append((key, valid, dr))
        types.append([p[0] for p in patterns].index(key))
        starts.append(u)
    return starts, types, [(p[1], p[2]) for p in patterns]


NA_DR = 2 * NA_ROWS_MAX - 1


def _na_bias_table(rpb):
    qcol = np.arange(GRID_W)
    col_start = np.clip(qcol - NA_COLS // 2, 0, GRID_W - NA_COLS)
    kcol = np.arange(GRID_W)
    col_mask = (kcol[None, :] >= col_start[:, None]) & (kcol[None, :] < col_start[:, None] + NA_COLS)
    dc = np.clip(kcol[None, :] - qcol[:, None] + NA_COLS - 1, 0, 2 * NA_COLS - 2)
    col_hot = (dc[None] == np.arange(2 * NA_COLS - 1)[:, None, None]).astype(np.float32)
    by_col = jnp.einsum("lhab,bqk->lhaqk", rpb.astype(F32), col_hot, precision=lax.Precision.HIGHEST)
    by_col = jnp.where(col_mask, by_col * LOG2E, NEG)
    masked = jnp.full(by_col.shape[:2] + (1, GRID_W, GRID_W), NEG, F32)
    table = jnp.concatenate([by_col, masked], axis=2)
    return jnp.concatenate([table, table], axis=-1)


def _na_pieces(patterns):
    out = []
    for valid, dr in patterns:
        idx = np.where(valid, dr, NA_DR)
        out.append(tuple(tuple((int(idx[r, 2 * p]), int(idx[r, 2 * p + 1])) for p in range(NA_KROWS // 2))
                         for r in range(NA_QROWS)))
    return tuple(out)


def _softmax_pv(s_list, v_list):
    m = None
    for s in s_list:
        mm = jnp.max(s, axis=-1, keepdims=True)
        m = mm if m is None else jnp.maximum(m, mm)
    num, den = None, None
    for s, v in zip(s_list, v_list):
        p = jnp.exp2(s - m)
        ssum = jnp.sum(p, axis=-1, keepdims=True)
        o = jnp.dot(p.astype(BF16), v, preferred_element_type=F32)
        num = o if num is None else num + o
        den = ssum if den is None else den + ssum
    return num / den


def _na_kernel(q_ref, k_ref, v_ref, cq_ref, ck_ref, cv_ref, tab_ref, cos_ref, sin_ref, y_ref, cy_ref,
               qs_ref, ks_ref, bias_ref, *, starts, types, pieces):
    lane = lax.broadcasted_iota(jnp.int32, (1, HEAD_DIM), 1)
    first = (lane % (HEAD_DIM // 2)) < HEAD_DIM // 4
    scale = HEAD_DIM ** -0.5 * LOG2E

    @pl.when(pl.program_id(1) == 0)
    def _():
        even = lane < GRID_W
        for tp, by_row in enumerate(pieces):
            for r, by_pair in enumerate(by_row):
                for p, (ie, io) in enumerate(by_pair):
                    bias_ref[tp, r * GRID_W:(r + 1) * GRID_W, 2 * p * GRID_W:2 * (p + 1) * GRID_W] = (
                        jnp.where(even, tab_ref[ie], tab_ref[io]))

    def rope(t):
        partner = jnp.where(first, pltpu.roll(t, HEAD_DIM - HEAD_DIM // 4, axis=1),
                            pltpu.roll(t, HEAD_DIM // 4, axis=1))
        return t * cos_ref[...] + partner * sin_ref[...]

    qs_ref[...] = (rope(q_ref[...].astype(F32)) * scale).astype(BF16)
    ks_ref[...] = rope(k_ref[...].astype(F32)).astype(BF16)
    ck = ck_ref[...]
    cv = cv_ref[...]
    qrows = NA_QROWS * GRID_W
    krows = NA_KROWS * GRID_W
    for j, (u, tp) in enumerate(zip(starts, types)):
        qb = qs_ref[j * qrows:(j + 1) * qrows, :]
        kb = ks_ref[u * GRID_W:u * GRID_W + krows, :]
        vb = v_ref[u * GRID_W:u * GRID_W + krows, :]
        s_loc = lax.dot_general(qb, kb, _NT, preferred_element_type=F32) + bias_ref[tp]
        s_ctx = lax.dot_general(qb, ck, _NT, preferred_element_type=F32)
        y_ref[j * qrows:(j + 1) * qrows, :] = _softmax_pv([s_loc, s_ctx], [vb, cv]).astype(y_ref.dtype)
    s = lax.dot_general(cq_ref[...], ck, _NT, preferred_element_type=F32) * scale
    cy_ref[...] = _softmax_pv([s], [cv]).astype(cy_ref.dtype)


def _attention(ub, col0, n, lc, bsz, heads, table, layer, layout):
    starts, types, patterns = layout
    cos, sin = _na_tables(n)
    cb = col0 // HEAD_DIM
    ctx0 = bsz * n // lc

    def lat(k):
        return pl.BlockSpec((n, HEAD_DIM), lambda h, b: (b, cb + k * heads + h))

    def ctx(k):
        return pl.BlockSpec((lc, HEAD_DIM), lambda h, b: (ctx0 + b, cb + k * heads + h))

    tab = pl.BlockSpec((n, HEAD_DIM), lambda h, b: (0, 0))
    return pl.pallas_call(
        functools.partial(_na_kernel, starts=tuple(starts), types=tuple(types), pieces=_na_pieces(patterns)),
        grid=(heads, bsz),
        in_specs=[lat(0), lat(1), lat(2), ctx(0), ctx(1), ctx(2),
                  pl.BlockSpec((None, None) + table.shape[2:], lambda h, b: (layer, h, 0, 0, 0)), tab, tab],
        out_specs=[pl.BlockSpec((n, HEAD_DIM), lambda h, b: (b, h)),
                   pl.BlockSpec((lc, HEAD_DIM), lambda h, b: (b, h))],
        out_shape=[jax.ShapeDtypeStruct((bsz * n, heads * HEAD_DIM), BF16),
                   jax.ShapeDtypeStruct((bsz * lc, heads * HEAD_DIM), BF16)],
        scratch_shapes=[pltpu.VMEM((n, HEAD_DIM), BF16), pltpu.VMEM((n, HEAD_DIM), BF16),
                        pltpu.VMEM((len(patterns), NA_QROWS * GRID_W, NA_KROWS * GRID_W), F32)],
        compiler_params=_params("arbitrary", "arbitrary"), name="attention",
    )(ub, ub, ub, ub, ub, ub, table, jnp.asarray(cos), jnp.asarray(sin))


def _route_kernel(afft_ref, slot_ref, *, n, ne, cap):
    bits = lax.bitcast_convert_type(afft_ref[...], jnp.int32)

    def bisect(i, thr):
        cand = thr | jnp.left_shift(jnp.int32(1), F32_VALUE_BITS - 1 - i)
        cnt = jnp.sum((bits >= cand).astype(F32), axis=1, keepdims=True)
        return jnp.where(cnt >= cap, cand, thr)

    thr = lax.fori_loop(0, F32_VALUE_BITS, bisect, jnp.zeros((ne, 1), jnp.int32))
    above = bits > thr
    tied = bits == thr
    need = cap - jnp.sum(above.astype(F32), axis=1, keepdims=True)

    row = lax.broadcasted_iota(jnp.int32, (LANES, LANES), 0)
    colm = lax.broadcasted_iota(jnp.int32, (LANES, LANES), 1)
    before = (row < colm).astype(BF16)
    ones = jnp.ones((LANES, LANES), BF16)

    def prefix(x):
        xb = x.astype(BF16)
        outs, carry = [], jnp.zeros((ne, LANES), F32)
        for blk in range(n // LANES):
            xs = xb[:, blk * LANES:(blk + 1) * LANES]
            outs.append(jnp.dot(xs, before, preferred_element_type=F32) + carry)
            carry = carry + jnp.dot(xs, ones, preferred_element_type=F32)
        return jnp.concatenate(outs, axis=1)

    chosen = above | (tied & (prefix(tied.astype(F32)) < need))
    slot = prefix(chosen.astype(F32)).astype(jnp.int32)
    slot_ref[...] = jnp.where(chosen, slot, n)


def _route(aff, row0, n, bsz, cap):
    ne = aff.shape[1]
    at = jnp.swapaxes(aff[row0:row0 + bsz * n].reshape(bsz, n, ne), 1, 2)
    spec = pl.BlockSpec((bsz * ne, n), lambda i: (0, 0))
    slot = pl.pallas_call(
        functools.partial(_route_kernel, n=n, ne=bsz * ne, cap=cap),
        grid=(1,), in_specs=[spec], out_specs=spec,
        out_shape=jax.ShapeDtypeStruct((bsz * ne, n), jnp.int32),
        compiler_params=_params("arbitrary"), name="expert_route",
    )(at.reshape(bsz * ne, n))
    return at, slot.reshape(bsz, ne, n)


def _gather_kernel(slot_ref, afft_ref, h_ref, xg_ref, g_ref, *, cap, eg):
    e0 = pl.program_id(1) * eg
    want = lax.broadcasted_iota(jnp.int32, (cap, 1), 0)
    sels = [slot_ref[pl.ds(e0 + k, 1), :] == want for k in range(eg)]
    sel = sels[0].astype(BF16) if eg == 1 else jnp.concatenate([s.astype(BF16) for s in sels], axis=0)
    xg = jnp.dot(sel, h_ref[...], preferred_element_type=F32).astype(xg_ref.dtype)
    xg_ref[...] = xg.reshape(xg_ref.shape)
    for k in range(eg):
        g_ref[k] = jnp.sum(jnp.where(sels[k], afft_ref[pl.ds(e0 + k, 1), :], 0.0), axis=1, keepdims=True)


def _gather(slot, afft, h2, row_block0, n, bsz, cap):
    ne = slot.shape[1]
    d = h2.shape[1]
    eg = max(1, min(ne, GATHER_ROWS // cap))
    while ne % eg:
        eg -= 1
    row_spec = pl.BlockSpec((None, ne, n), lambda b, e: (b, 0, 0))
    return pl.pallas_call(
        functools.partial(_gather_kernel, cap=cap, eg=eg),
        grid=(bsz, ne // eg),
        in_specs=[row_spec, row_spec, pl.BlockSpec((n, d), lambda b, e: (row_block0 + b, 0))],
        out_specs=[pl.BlockSpec((eg, cap, d), lambda b, e: (e, b, 0)),
                   pl.BlockSpec((eg, cap, 1), lambda b, e: (e, b, 0))],
        out_shape=[jax.ShapeDtypeStruct((ne, bsz * cap, d), BF16),
                   jax.ShapeDtypeStruct((ne, bsz * cap, 1), F32)],
        compiler_params=_params("arbitrary", "arbitrary"), name="expert_gather",
    )(slot, afft, h2)


def _ffn_kernel(*refs, ns, nf):
    x_refs = refs[0:2 * ns:2]
    g_refs = refs[1:2 * ns:2]
    wg_ref, wu_ref, wd_ref = refs[2 * ns:2 * ns + 3]
    o_refs = refs[2 * ns + 3:3 * ns + 3]
    wgb, wub, wdb = refs[3 * ns + 3:3 * ns + 6]
    acc_refs = refs[3 * ns + 6:]
    f = pl.program_id(1)
    wgb[...] = wg_ref[...].astype(BF16)
    wub[...] = wu_ref[...].astype(BF16)
    wdb[...] = wd_ref[...].astype(BF16)

    def ff_tile(first, last):
        for k in range(ns):
            rows = x_refs[k].shape[0]
            step = min(FFN_ROWS, rows)
            for r0 in range(0, rows, step):
                rs = slice(r0, r0 + step)
                x = x_refs[k][rs, :]
                hg = jnp.dot(x, wgb[...], preferred_element_type=F32)
                hu = jnp.dot(x, wub[...], preferred_element_type=F32)
                hid = (hg * _sigmoid(hg) * hu).astype(BF16)
                part = jnp.dot(hid, wdb[...], preferred_element_type=F32)
                if not first:
                    part = acc_refs[k][rs, :] + part
                if last:
                    o_refs[k][rs, :] = (part * g_refs[k][rs, :]).astype(o_refs[k].dtype)
                else:
                    acc_refs[k][rs, :] = part

    if nf == 1:
        ff_tile(True, True)
    else:
        pl.when(f == 0)(functools.partial(ff_tile, True, False))
        pl.when(f == nf - 1)(functools.partial(ff_tile, False, True))
        if nf > 2:
            pl.when((f > 0) & (f < nf - 1))(functools.partial(ff_tile, False, False))


def _expert_ffn(xgs, gates, w_gate, w_up, w_down, layer):
    ns = len(xgs)
    _, ne, d, ff = w_gate.shape
    tf = _lane_tile(ff, FF_TILE)
    nf = ff // tf
    in_specs, out_specs, out_shape, acc_scr, args = [], [], [], [], []
    for k in range(ns):
        rows = xgs[k].shape[1]
        in_specs.append(pl.BlockSpec((None, rows, d), lambda e, f: (e, 0, 0)))
        in_specs.append(pl.BlockSpec((None, rows, 1), lambda e, f: (e, 0, 0)))
        out_specs.append(pl.BlockSpec((None, rows, d), lambda e, f: (e, 0, 0)))
        out_shape.append(jax.ShapeDtypeStruct((ne, rows, d), BF16))
        acc_scr.append(pltpu.VMEM((rows, d), F32))
        args += [xgs[k], gates[k]]
    in_specs += [pl.BlockSpec((None, None, d, tf), lambda e, f: (layer, e, 0, f)),
                 pl.BlockSpec((None, None, d, tf), lambda e, f: (layer, e, 0, f)),
                 pl.BlockSpec((None, None, tf, d), lambda e, f: (layer, e, f, 0))]
    return pl.pallas_call(
        functools.partial(_ffn_kernel, ns=ns, nf=nf),
        grid=(ne, nf),
        in_specs=in_specs, out_specs=out_specs, out_shape=out_shape,
        scratch_shapes=[pltpu.VMEM((d, tf), BF16), pltpu.VMEM((d, tf), BF16), pltpu.VMEM((tf, d), BF16)] + acc_scr,
        compiler_params=_params("arbitrary", "arbitrary"), name="expert_ffn",
    )(*args, w_gate, w_up, w_down)


def _scatter_kernel(slot_ref, y_ref, x_ref, mod_ref, o_ref, *, ne, cap):
    sl = slot_ref[...]
    want = lax.broadcasted_iota(jnp.int32, (1, cap), 1)
    acc = None
    for e in range(ne):
        sel = (sl[:, e:e + 1] == want).astype(BF16)
        p = jnp.dot(sel, y_ref[e], preferred_element_type=F32)
        acc = p if acc is None else acc + p
    o_ref[...] = x_ref[...] + mod_ref[5:6, :] * acc


def _scatter(slot, y, x, mod, layer, mod_row0, row0, n, bsz, cap, tn):
    ne = slot.shape[1]
    d = x.shape[1]
    tm = min(SCATTER_ROWS, n)
    slot_t = jnp.swapaxes(slot, 1, 2)
    xb0 = row0 // tm
    per_seq = n // tm
    return pl.pallas_call(
        functools.partial(_scatter_kernel, ne=ne, cap=cap),
        grid=(bsz, d // tn, per_seq),
        in_specs=[pl.BlockSpec((None, tm, ne), lambda b, j, i: (b, i, 0)),
                  pl.BlockSpec((ne, cap, tn), lambda b, j, i: (0, b, j)),
                  pl.BlockSpec((tm, tn), lambda b, j, i: (xb0 + b * per_seq + i, j)),
                  pl.BlockSpec((None, None, 6, tn), lambda b, j, i: (layer, mod_row0(b), 0, j))],
        out_specs=pl.BlockSpec((tm, tn), lambda b, j, i: (b * per_seq + i, j)),
        out_shape=jax.ShapeDtypeStruct((bsz * n, d), F32),
        compiler_params=_params("arbitrary", "arbitrary", "arbitrary"), name="expert_scatter",
    )(slot_t, y, x, mod)


def kernel(x, c, ctx, c_ctx, w_mod, b_mod, g_norm1, w_in, lb_param, g_hgrn, w_pool, pool_scale, rpb,
           w_branch, w_out, g_norm2, w_router, w_gate_e, w_up_e, w_down_e, g_final):
    bsz, n, d = x.shape
    lc = ctx.shape[1]
    depth = w_mod.shape[0]
    width = lb_param.shape[2]
    heads = width // HEAD_DIM
    ne = w_router.shape[2]
    n_lat = bsz * n
    n_ctx = bsz * lc
    total = n_lat + n_ctx
    assert bsz + 1 <= 8 and n % ROW_TILE == 0 and lc % ROW_TILE == 0 and n_lat % lc == 0
    assert w_pool.shape[-1] * 4 == width and w_branch.shape[2] == width
    tm = _row_tile(n, n_ctx)
    tn = _lane_tile(d, MERGE_COL_TILE)
    tn_wide = _lane_tile(d, WIDE_COL_TILE)

    lb_all = jnp.cumsum(jax.nn.softmax(lb_param.astype(F32), axis=1), axis=1)
    lb_all = lb_all - lb_all[:, :1]
    lbp_all = jnp.stack([jnp.log(lb_all), jnp.log1p(-lb_all)], axis=2) * LOG2E
    lbp_all = lbp_all.reshape(2, depth, 2, heads, LANES).transpose(0, 1, 3, 2, 4)

    c8 = jnp.concatenate([c, c_ctx[None], jnp.zeros((8 - bsz - 1, d), F32)], axis=0)
    mod = _modulation(c8, w_mod, b_mod).reshape(depth, 8, 6, d)

    cm3, pair_masks, upper = _hgrn_constants()
    hgrn_consts = (jnp.asarray(cm3, BF16), jnp.asarray(pair_masks, F32), jnp.asarray(upper, F32))
    na_layout = _na_block_layout(n // GRID_W)
    na_bias = _na_bias_table(rpb)
    lanes_e = -(-ne // LANES) * LANES
    w_router_p = jnp.pad(w_router, ((0, 0), (0, 0), (0, lanes_e - ne)))

    x_lat = x.reshape(n_lat, d)
    x_ctx = ctx.reshape(n_ctx, d)
    zero_state = jnp.zeros((bsz, heads, LANES, LANES), F32)
    cap = EC_CAPACITY * n // ne
    cap_c = EC_CAPACITY * lc // ne
    a_cols = 5 * width
    b_cols = w_in.shape[2] - a_cols
    tn_in = _lane_tile(math.gcd(a_cols, b_cols), WIDE_COL_TILE)
    gate_col0 = 4 * width
    rt_all = _Rows(tm, n, bsz, n_lat, total)
    rt_lat = _Rows(tm, n, bsz, n_lat, n_lat)
    tm_norm = min(NORM_ROWS, tm)
    rn_all = _Rows(tm_norm, n, bsz, n_lat, total)
    rn_lat = _Rows(tm_norm, n, bsz, n_lat, n_lat)
    tm_out = max(tm // 2, ROW_TILE)
    ro_all = _Rows(tm_out, n, bsz, n_lat, total)
    ro_lat = _Rows(tm_out, n, bsz, n_lat, n_lat)

    for l in range(depth):
        last = l == depth - 1
        rt, rn, ro = (rt_lat, rn_lat, ro_lat) if last else (rt_all, rn_all, ro_all)
        gain_h = g_hgrn[l].reshape(heads, 1, LANES)

        h = _norm1(x_lat, x_ctx, g_norm1[l], mod, l, rn_all)
        ua = _matmul(h, w_in, l, 0, a_cols, F32, tm, tn_in)
        ub = _matmul(h, w_in, l, a_cols, b_cols, BF16, tm, tn_in)

        cy_a, s_f, s_b = _hgrn(ua, n_lat // lc, lc, bsz, heads, lbp_all[:, l], gain_h, zero_state, zero_state,
                               hgrn_consts)
        y_a, _, _ = _hgrn(ua, 0, n, bsz, heads, lbp_all[:, l], gain_h, s_f, s_b, hgrn_consts)
        y_b = _pool(ub, 0, n, bsz, w_pool, pool_scale, l)
        y_c, cy_c = _attention(ub, width, n, lc, bsz, heads, na_bias, l, na_layout)
        ys_lat = (y_a, y_b, y_c)
        ys_ctx = ys_lat if last else (cy_a, _pool(ub, n_lat // lc, lc, bsz, w_pool, pool_scale, l), cy_c)

        merged = _merge(ys_lat, ys_ctx, ub, gate_col0, w_branch, l, rt, tn)
        x_mid = _out_proj(merged, w_out, l, x_lat, x_ctx, mod, ro, tn_wide)

        h2, aff = _norm2(x_mid, g_norm2[l], mod, l, rn, w_router_p, ne)
        afft, slot = _route(aff, 0, n, bsz, cap)
        xg, gate = _gather(slot, afft, h2, 0, n, bsz, cap)
        xgs, gates = [xg], [gate]
        if not last:
            afft_c, slot_c = _route(aff, n_lat, lc, bsz, cap_c)
            xg_c, gate_c = _gather(slot_c, afft_c, h2, n_lat // lc, lc, bsz, cap_c)
            xgs, gates = xgs + [xg_c], gates + [gate_c]
        ys = _expert_ffn(xgs, gates, w_gate_e, w_up_e, w_down_e, l)
        x_lat = _scatter(slot, ys[0], x_mid, mod, l, lambda b: b, 0, n, bsz, cap, tn_wide)
        if not last:
            x_ctx = _scatter(slot_c, ys[1], x_mid, mod, l, lambda b: bsz, n_lat, lc, bsz, cap_c, tn_wide)

    return _final_norm(x_lat, g_final, tm_norm).reshape(bsz, n, d)
```

```python
import functools
import math

import numpy as np
import jax
import jax.numpy as jnp
from jax import lax
from jax.experimental import pallas as pl
from jax.experimental.pallas import tpu as pltpu

F32 = jnp.float32
BF16 = jnp.bfloat16
EPS = 1e-6
LANES = 128
HEAD_DIM = 128
CHUNK = 64
HGRN_GROUP = 4
GRID_W = 64
POOL_WINDOWS = (2, 4, 8, 16)
POOL_PAD = 16
NA_ROWS_MAX = 8
NA_COLS = 16
NA_QROWS = 4
NA_KROWS = 12
ROPE_THETA = 10000.0
EC_CAPACITY = 2
GATHER_ROWS = 512
FFN_ROWS = 512
NEG = -1e30
ROW_TILE = 256
NORM_ROWS = 512
MAX_ROW_TILE = 1024
WIDE_COL_TILE = 1024
MERGE_COL_TILE = 512
FF_TILE = 256
SCATTER_ROWS = 512
F32_SIGN_BIT = 0x80000000
F32_VALUE_BITS = 31
LOG2E = 1.4426950408889634
VMEM_LIMIT = 56 * 2 ** 20


def _params(*sem):
    return pltpu.CompilerParams(dimension_semantics=sem, vmem_limit_bytes=VMEM_LIMIT)


def _sigmoid(x):
    return 1.0 / (1.0 + jnp.exp(-x))


def _neg_abs(x):
    bits = lax.bitcast_convert_type(x, jnp.uint32) | jnp.uint32(F32_SIGN_BIT)
    return lax.bitcast_convert_type(bits, F32)


def _lane_tile(n, limit):
    t = min(limit, n) // LANES * LANES
    while n % t:
        t -= LANES
    return t


def _row_tile(n, ctx_rows):
    tm = MAX_ROW_TILE
    while n % tm or ctx_rows % tm:
        tm //= 2
    return tm


class _Rows:
    def __init__(self, tm, n, bsz, n_lat, rows):
        self.tm, self.n, self.bsz = tm, n, bsz
        self.lat_tiles = n_lat // tm
        self.tiles = rows // tm
        self.ctx_tiles = max(self.tiles - self.lat_tiles, 1)

    def lat(self, i):
        return jnp.minimum(i, self.lat_tiles - 1)

    def ctx(self, i):
        return jnp.clip(i - self.lat_tiles, 0, self.ctx_tiles - 1)

    def mod_row(self, i):
        return jnp.minimum(i * self.tm // self.n, self.bsz)


def _pick(is_lat, lat_ref, ctx_ref):
    return jnp.where(is_lat, lat_ref[...], ctx_ref[...])


def _mod_kernel(c_ref, w_ref, b_ref, o_ref):
    c = c_ref[...]
    sc = (c * _sigmoid(c)).astype(BF16)
    o_ref[...] = jnp.dot(sc, w_ref[...].astype(BF16), preferred_element_type=F32) + b_ref[...]


def _modulation(c8, w_mod, b_mod):
    depth, d, n6 = w_mod.shape
    tn = _lane_tile(n6, WIDE_COL_TILE)
    return pl.pallas_call(
        _mod_kernel,
        grid=(depth, n6 // tn),
        in_specs=[pl.BlockSpec((8, d), lambda l, j: (0, 0)),
                  pl.BlockSpec((None, d, tn), lambda l, j: (l, 0, j)),
                  pl.BlockSpec((None, 1, tn), lambda l, j: (l, 0, j))],
        out_specs=pl.BlockSpec((None, 8, tn), lambda l, j: (l, 0, j)),
        out_shape=jax.ShapeDtypeStruct((depth, 8, n6), F32),
        compiler_params=_params("arbitrary", "arbitrary"),
        name="modulation",
    )(c8, w_mod, b_mod.reshape(depth, 1, n6))


def _rmsnorm(x, gain):
    return x * lax.rsqrt(jnp.mean(x * x, axis=-1, keepdims=True) + EPS) * gain


def _modulate(y, mod_ref, shift_idx):
    return y * (1.0 + mod_ref[shift_idx + 1:shift_idx + 2, :]) + mod_ref[shift_idx:shift_idx + 1, :]


def _norm1_kernel(xl_ref, xc_ref, g_ref, mod_ref, o_ref, *, lat_tiles):
    x = _pick(pl.program_id(0) < lat_tiles, xl_ref, xc_ref)
    o_ref[...] = _modulate(_rmsnorm(x, g_ref[...]), mod_ref, 0).astype(o_ref.dtype)


def _norm2_kernel(x_ref, g_ref, mod_ref, wr_ref, o_ref, aff_ref):
    h = _modulate(_rmsnorm(x_ref[...], g_ref[...]), mod_ref, 3)
    hb = h.astype(BF16)
    o_ref[...] = hb
    h_lo = (h - hb.astype(F32)).astype(BF16)
    w = wr_ref[...]
    wb = w.astype(BF16)
    w_lo = (w - wb.astype(F32)).astype(BF16)
    logits = (jnp.dot(hb, wb, preferred_element_type=F32) + jnp.dot(h_lo, wb, preferred_element_type=F32)
              + jnp.dot(hb, w_lo, preferred_element_type=F32))
    logits = logits[:, :aff_ref.shape[1]]
    e = jnp.exp(logits - jnp.max(logits, axis=-1, keepdims=True))
    aff_ref[...] = e / jnp.sum(e, axis=-1, keepdims=True)


def _final_norm_kernel(x_ref, g_ref, o_ref):
    o_ref[...] = _rmsnorm(x_ref[...], g_ref[...])


def _norm1(x_lat, x_ctx, gain, mod, layer, rt):
    d = x_lat.shape[1]
    return pl.pallas_call(
        functools.partial(_norm1_kernel, lat_tiles=rt.lat_tiles),
        grid=(rt.tiles,),
        in_specs=[pl.BlockSpec((rt.tm, d), lambda i: (rt.lat(i), 0)),
                  pl.BlockSpec((rt.tm, d), lambda i: (rt.ctx(i), 0)),
                  pl.BlockSpec((1, d), lambda i: (0, 0)),
                  pl.BlockSpec((None, None, 6, d), lambda i: (layer, rt.mod_row(i), 0, 0))],
        out_specs=pl.BlockSpec((rt.tm, d), lambda i: (i, 0)),
        out_shape=jax.ShapeDtypeStruct((rt.tiles * rt.tm, d), BF16),
        compiler_params=_params("arbitrary"), name="norm1",
    )(x_lat, x_ctx, gain.reshape(1, d), mod)


def _norm2(x, gain, mod, layer, rt, w_router_padded, ne):
    d = x.shape[1]
    rows = rt.tiles * rt.tm
    return pl.pallas_call(
        _norm2_kernel,
        grid=(rt.tiles,),
        in_specs=[pl.BlockSpec((rt.tm, d), lambda i: (i, 0)),
                  pl.BlockSpec((1, d), lambda i: (0, 0)),
                  pl.BlockSpec((None, None, 6, d), lambda i: (layer, rt.mod_row(i), 0, 0)),
                  pl.BlockSpec((None, d, w_router_padded.shape[2]), lambda i: (layer, 0, 0))],
        out_specs=[pl.BlockSpec((rt.tm, d), lambda i: (i, 0)), pl.BlockSpec((rt.tm, ne), lambda i: (i, 0))],
        out_shape=[jax.ShapeDtypeStruct((rows, d), BF16), jax.ShapeDtypeStruct((rows, ne), F32)],
        compiler_params=_params("arbitrary"), name="norm2_router",
    )(x, gain.reshape(1, d), mod, w_router_padded)


def _final_norm(x, gain, tm):
    rows, d = x.shape
    return pl.pallas_call(
        _final_norm_kernel,
        grid=(rows // tm,),
        in_specs=[pl.BlockSpec((tm, d), lambda i: (i, 0)), pl.BlockSpec((1, d), lambda i: (0, 0))],
        out_specs=pl.BlockSpec((tm, d), lambda i: (i, 0)),
        out_shape=jax.ShapeDtypeStruct((rows, d), F32),
        compiler_params=_params("arbitrary"), name="final_norm",
    )(x, gain.reshape(1, d))


def _mm_kernel(a_ref, w_ref, o_ref, wbf_ref):
    @pl.when(pl.program_id(1) == 0)
    def _():
        wbf_ref[...] = w_ref[...].astype(BF16)
    o_ref[...] = jnp.dot(a_ref[...], wbf_ref[...], preferred_element_type=F32).astype(o_ref.dtype)


def _matmul(a, w_all, layer, col0, ncols, out_dtype, tm, tn):
    m, k = a.shape
    off = col0 // tn
    return pl.pallas_call(
        _mm_kernel,
        grid=(ncols // tn, m // tm),
        in_specs=[pl.BlockSpec((tm, k), lambda j, i: (i, 0)),
                  pl.BlockSpec((None, k, tn), lambda j, i: (layer, 0, j + off))],
        out_specs=pl.BlockSpec((tm, tn), lambda j, i: (i, j)),
        out_shape=jax.ShapeDtypeStruct((m, ncols), out_dtype),
        scratch_shapes=[pltpu.VMEM((k, tn), BF16)],
        compiler_params=_params("arbitrary", "arbitrary"), name="matmul",
    )(a, w_all)


def _out_proj_kernel(a_ref, w_ref, xl_ref, xc_ref, mod_ref, o_ref, wbf_ref, *, lat_tiles):
    @pl.when(pl.program_id(1) == 0)
    def _():
        wbf_ref[...] = w_ref[...].astype(BF16)
    acc = jnp.dot(a_ref[...], wbf_ref[...], preferred_element_type=F32)
    x = _pick(pl.program_id(1) < lat_tiles, xl_ref, xc_ref)
    o_ref[...] = x + mod_ref[2:3, :] * acc


def _out_proj(a, w_all, layer, x_lat, x_ctx, mod, rt, tn):
    k = a.shape[1]
    d = w_all.shape[2]
    tm = rt.tm
    return pl.pallas_call(
        functools.partial(_out_proj_kernel, lat_tiles=rt.lat_tiles),
        grid=(d // tn, rt.tiles),
        in_specs=[pl.BlockSpec((tm, k), lambda j, i: (i, 0)),
                  pl.BlockSpec((None, k, tn), lambda j, i: (layer, 0, j)),
                  pl.BlockSpec((tm, tn), lambda j, i: (rt.lat(i), j)),
                  pl.BlockSpec((tm, tn), lambda j, i: (rt.ctx(i), j)),
                  pl.BlockSpec((None, None, 6, tn), lambda j, i: (layer, rt.mod_row(i), 0, j))],
        out_specs=pl.BlockSpec((tm, tn), lambda j, i: (i, j)),
        out_shape=jax.ShapeDtypeStruct((rt.tiles * tm, d), F32),
        scratch_shapes=[pltpu.VMEM((k, tn), BF16)],
        compiler_params=_params("arbitrary", "arbitrary"), name="out_proj",
    )(a, w_all, x_lat, x_ctx, mod)


def _merge_kernel(ya_ref, yb_ref, yc_ref, cya_ref, cyb_ref, cyc_ref, g0_ref, g1_ref, g2_ref, w_ref, o_ref, wbf_ref,
                  *, lat_tiles):
    @pl.when(pl.program_id(1) == 0)
    def _():
        wbf_ref[...] = w_ref[...].astype(BF16)
    is_lat = pl.program_id(1) < lat_tiles
    acc = None
    for j, (y_ref, cy_ref, g_ref) in enumerate(((ya_ref, cya_ref, g0_ref), (yb_ref, cyb_ref, g1_ref),
                                                (yc_ref, cyc_ref, g2_ref))):
        p = jnp.dot(_pick(is_lat, y_ref, cy_ref), wbf_ref[j], preferred_element_type=F32)
        t = _sigmoid(g_ref[...].astype(F32)) * p
        acc = t if acc is None else acc + t
    o_ref[...] = acc.astype(o_ref.dtype)


def _merge(ys_lat, ys_ctx, ub, gate_col0, w_branch, layer, rt, tn):
    bw = ys_lat[0].shape[1]
    d = w_branch.shape[3]
    tm = rt.tm
    lat_spec = pl.BlockSpec((tm, bw), lambda j, i: (rt.lat(i), 0))
    ctx_spec = pl.BlockSpec((tm, bw), lambda j, i: (rt.ctx(i), 0))

    def gate_spec(k):
        off = (gate_col0 + k * d) // tn
        return pl.BlockSpec((tm, tn), lambda j, i: (i, off + j))

    return pl.pallas_call(
        functools.partial(_merge_kernel, lat_tiles=rt.lat_tiles),
        grid=(d // tn, rt.tiles),
        in_specs=[lat_spec] * 3 + [ctx_spec] * 3 + [gate_spec(0), gate_spec(1), gate_spec(2),
                  pl.BlockSpec((None, 3, bw, tn), lambda j, i: (layer, 0, 0, j))],
        out_specs=pl.BlockSpec((tm, tn), lambda j, i: (i, j)),
        out_shape=jax.ShapeDtypeStruct((rt.tiles * tm, d), BF16),
        scratch_shapes=[pltpu.VMEM((3, bw, tn), BF16)],
        compiler_params=_params("arbitrary", "arbitrary"), name="merge",
    )(*ys_lat, *ys_ctx, ub, ub, ub, w_branch)


_HGRN_BLOCKS = (32, 16, 8, 4, 2, 1)


def _hgrn_constants():
    c = CHUNK
    t = np.arange(c)
    tri_f = (t[:, None] >= t[None, :]).astype(np.float32)
    tri_b = np.ascontiguousarray(tri_f[::-1, ::-1])
    sel_f, sel_b, masks, upper = [tri_f], [tri_b], [], []
    for m in _HGRN_BLOCKS:
        blk = (t // (2 * m)) * (2 * m)
        up = (t % (2 * m)) >= m
        sel_f.append(tri_f - tri_f[blk + m - 1])
        sel_b.append(tri_b - tri_b[blk + m])
        same = (t[:, None] // (2 * m)) == (t[None, :] // (2 * m))
        masks.append(same & (up[:, None] != up[None, :]))
        upper.append(np.broadcast_to(up[:, None], (c, LANES)))
    masks.append(np.eye(c, dtype=bool))
    sel = np.stack([np.concatenate(sel_f, axis=0), np.concatenate(sel_b, axis=0)])
    return (np.concatenate([sel, sel, sel], axis=2), np.stack(masks).astype(np.float32),
            np.stack(upper).astype(np.float32))


def _hgrn_log2_gate(z, lbp):
    zl = z * LOG2E
    log_sig = jnp.minimum(zl, 0.0) - jnp.log2(1.0 + jnp.exp2(_neg_abs(zl)))
    a = lbp[0:1, :]
    cc = lbp[1:2, :] + log_sig
    return jnp.maximum(a, cc) + jnp.log2(1.0 + jnp.exp2(_neg_abs(a - cc)))


def _split3(x):
    g1 = x.astype(BF16)
    r1 = x - g1.astype(F32)
    g2 = r1.astype(BF16)
    g3 = (r1 - g2.astype(F32)).astype(BF16)
    return jnp.concatenate([g1, g2, g3], axis=0)


_NT = (((1,), (1,)), ((), ()))
_TN = (((0,), (0,)), ((), ()))


def _hgrn_kernel(q_ref, ff_ref, fb_ref, i_ref, g_ref, lbf_ref, lbb_ref, gain_ref, cm_ref, mask_ref, up_ref,
                 sf0_ref, sb0_ref, y_ref, sf_ref, sb_ref, o_scr, qt_scr, upd_scr, dec_scr, st_scr, sums_scr, k_scr,
                 *, nc):
    z_refs = (ff_ref, fb_ref)
    lb_refs = (lbf_ref, lbb_ref)
    nl = len(_HGRN_BLOCKS)
    last_row = (CHUNK - 1, 0)

    groups = nc // HGRN_GROUP
    srows = (1 + nl) * CHUNK

    def gate_pass(it, carry):
        rows = [pl.ds(pl.multiple_of((it * HGRN_GROUP + s) * CHUNK, CHUNK), CHUNK) for s in range(HGRN_GROUP)]
        for d in range(2):
            logf = [_hgrn_log2_gate(z_refs[d][r, :], lb_refs[d][...]) for r in rows]
            sums_scr[d, pl.ds(pl.multiple_of(it * srows, srows), srows), :] = jnp.dot(
                cm_ref[d], jnp.concatenate([_split3(lf) for lf in logf], axis=1), preferred_element_type=F32)
            for r, lf in zip(rows, logf):
                k_scr[d, r, :] = 1.0 - jnp.exp2(lf)
        return carry

    lax.fori_loop(0, groups, gate_pass, 0, unroll=min(4, groups))

    def local_pass(it, carry):
        sums = [sums_scr[d, pl.ds(pl.multiple_of(it * srows, srows), srows), :] for d in range(2)]
        rows = [pl.ds(pl.multiple_of((it * HGRN_GROUP + s) * CHUNK, CHUNK), CHUNK) for s in range(HGRN_GROUP)]
        up = up_ref[...] > 0.5
        for s, r in enumerate(rows):
            lanes = slice(s * LANES, (s + 1) * LANES)
            qr = q_ref[r, :]
            q = qr * _sigmoid(qr)
            vb = i_ref[r, :].astype(BF16)
            b = [sums[d][0:CHUNK, lanes] for d in range(2)]
            lvl = [jnp.exp2(_neg_abs(sums[d][CHUNK:, lanes].reshape(nl, CHUNK, LANES))) for d in range(2)]
            k = [k_scr[d, r, :] for d in range(2)]
            q_dec = jnp.where(up, lvl[0], lvl[1])
            k_dec = jnp.where(up, k[1][None] * lvl[1], k[0][None] * lvl[0])
            qs = jnp.concatenate([q[None] * q_dec, q[None]], axis=0).astype(BF16)
            ks = jnp.concatenate([k_dec, (k[0] + k[1])[None]], axis=0).astype(BF16)
            sc = jnp.einsum("ltk,lsk->lts", qs, ks, preferred_element_type=F32)
            amat = jnp.sum(sc * mask_ref[...], axis=0)
            o_scr[r, :] = jnp.dot(amat.astype(BF16), vb, preferred_element_type=F32)
            tot = [b[d][last_row[d]:last_row[d] + 1, :] for d in range(2)]
            kd = jnp.concatenate([(k[d] * jnp.exp2(tot[d] - b[d])).astype(BF16) for d in range(2)], axis=1)
            upd = lax.dot_general(vb, kd, _TN, preferred_element_type=F32)
            j = it * HGRN_GROUP + s
            for d in range(2):
                qt_scr[d, r, :] = (q * jnp.exp2(b[d])).astype(BF16)
                upd_scr[d, j] = upd[:, d * LANES:(d + 1) * LANES]
                dec_scr[d, pl.ds(j, 1), :] = jnp.exp2(tot[d])
        return carry

    lax.fori_loop(0, groups, local_pass, 0, unroll=min(4, groups))

    st_scr[0] = sf0_ref[...]
    st_scr[1] = sb0_ref[...]

    def state_pass(it, carry):
        for d in range(2):
            j = it if d == 0 else nc - 1 - it
            rows = pl.ds(pl.multiple_of(j * CHUNK, CHUNK), CHUNK)
            st = st_scr[d]
            o_scr[rows, :] += lax.dot_general(qt_scr[d, rows, :], st.astype(BF16), _NT, preferred_element_type=F32)
            st_scr[d] = st * dec_scr[d, pl.ds(j, 1), :] + upd_scr[d, j]
        return carry

    lax.fori_loop(0, nc, state_pass, 0, unroll=min(8, nc))
    sf_ref[...] = st_scr[0]
    sb_ref[...] = st_scr[1]
    o = o_scr[...]
    o = o * lax.rsqrt(jnp.mean(o * o, axis=-1, keepdims=True) + EPS) * gain_ref[...]
    g = g_ref[...]
    y_ref[...] = (o * (g * _sigmoid(g))).astype(y_ref.dtype)


def _hgrn(ua, row_block0, n, bsz, heads, lbp, gain, sf0, sb0, consts):
    nc = n // CHUNK
    assert nc % HGRN_GROUP == 0 and nc % 2 == 0

    def col(k):
        return pl.BlockSpec((n, LANES), lambda b, h: (row_block0 + b, k * heads + h))

    lb_spec = lambda d: pl.BlockSpec((None, None, 2, LANES), lambda b, h: (d, h, 0, 0))
    st_spec = pl.BlockSpec((None, None, LANES, LANES), lambda b, h: (b, h, 0, 0))
    st_shape = jax.ShapeDtypeStruct((bsz, heads, LANES, LANES), F32)
    const_specs = [pl.BlockSpec(a.shape, lambda b, h, nd=a.ndim: (0,) * nd) for a in consts]
    return pl.pallas_call(
        functools.partial(_hgrn_kernel, nc=nc),
        grid=(bsz, heads),
        in_specs=[col(0), col(1), col(2), col(3), col(4), lb_spec(0), lb_spec(1),
                  pl.BlockSpec((None, 1, LANES), lambda b, h: (h, 0, 0))] + const_specs + [st_spec, st_spec],
        out_specs=[pl.BlockSpec((n, LANES), lambda b, h: (b, h)), st_spec, st_spec],
        out_shape=[jax.ShapeDtypeStruct((bsz * n, heads * LANES), BF16), st_shape, st_shape],
        scratch_shapes=[pltpu.VMEM((n, LANES), F32), pltpu.VMEM((2, n, LANES), BF16),
                        pltpu.VMEM((2, nc, LANES, LANES), F32), pltpu.VMEM((2, nc, LANES), F32),
                        pltpu.VMEM((2, LANES, LANES), F32),
                        pltpu.VMEM((2, nc * (1 + len(_HGRN_BLOCKS)) * CHUNK // HGRN_GROUP, HGRN_GROUP * LANES), F32),
                        pltpu.VMEM((2, n, LANES), F32)],
        compiler_params=_params("arbitrary", "arbitrary"), name="hgrn2",
    )(ua, ua, ua, ua, ua, lbp, lbp, gain, *consts, sf0, sb0)


def _pool_kernel(u_ref, w_ref, s_ref, y_ref, pad_ref, buf_a, buf_b, *, n, group):
    pos = lax.broadcasted_iota(jnp.int32, (n, 1), 0)
    zeros = jnp.zeros((POOL_PAD, pad_ref.shape[1]), F32)
    pad_ref[0:POOL_PAD, :] = zeros
    pad_ref[POOL_PAD + n:2 * POOL_PAD + n, :] = zeros
    pad_ref[POOL_PAD:POOL_PAD + n, :] = u_ref[...].astype(F32)
    live = POOL_PAD + n
    for buf in (buf_a, buf_b):
        buf[live:live + POOL_PAD, :] = jnp.zeros((POOL_PAD, group), F32)
    for gi, w in enumerate(POOL_WINDOWS):
        cols = slice(gi * group, (gi + 1) * group)
        k, dst = 1, buf_a
        dst[0:live, :] = pad_ref[0:live, cols] + pad_ref[1:live + 1, cols]
        while 2 * k < w:
            k *= 2
            src, dst = dst, (buf_b if dst is buf_a else buf_a)
            dst[0:live, :] = src[0:live, :] + src[k:live + k, :]
        acc = dst[POOL_PAD - w // 2:POOL_PAD - w // 2 + n, :]
        lo = jnp.maximum(pos - w // 2, 0)
        hi = jnp.minimum(pos + w // 2 - 1, n - 1)
        cnt = (hi - lo + 1).astype(F32)
        dd = acc / cnt - pad_ref[POOL_PAD:POOL_PAD + n, cols]
        y = jnp.dot(dd.astype(BF16), w_ref[gi].astype(BF16), preferred_element_type=F32)
        y_ref[:, cols] = (y * s_ref[:, cols]).astype(y_ref.dtype)


def _pool(ub, row_block0, n, bsz, w_pool, scale, layer):
    group = w_pool.shape[-1]
    width = 4 * group
    assert all(w & (w - 1) == 0 and 2 <= w <= 2 * POOL_PAD for w in POOL_WINDOWS)
    return pl.pallas_call(
        functools.partial(_pool_kernel, n=n, group=group),
        grid=(bsz,),
        in_specs=[pl.BlockSpec((n, width), lambda b: (row_block0 + b, 0)),
                  pl.BlockSpec((None, 4, group, group), lambda b: (layer, 0, 0, 0)),
                  pl.BlockSpec((None, 1, width), lambda b: (layer, 0, 0))],
        out_specs=pl.BlockSpec((n, width), lambda b: (b, 0)),
        out_shape=jax.ShapeDtypeStruct((bsz * n, width), BF16),
        scratch_shapes=[pltpu.VMEM((n + 2 * POOL_PAD, width), F32),
                        pltpu.VMEM((n + 2 * POOL_PAD, group), F32), pltpu.VMEM((n + 2 * POOL_PAD, group), F32)],
        compiler_params=_params("arbitrary"), name="pool",
    )(ub, w_pool, scale.reshape(scale.shape[0], 1, width))


def _na_tables(n):
    pos = np.arange(n)
    half = HEAD_DIM // 2
    inv_freq = ROPE_THETA ** (-np.arange(0, half, 2, dtype=np.float64) / half)
    lane = np.arange(HEAD_DIM)
    p = np.where(lane[None, :] < half, (pos // GRID_W)[:, None], (pos % GRID_W)[:, None]).astype(np.float64)
    ang = p * inv_freq[lane % (half // 2)][None, :]
    sign = np.where((lane % half) < half // 2, -1.0, 1.0)[None, :]
    return np.cos(ang).astype(np.float32), (np.sin(ang) * sign).astype(np.float32)


def _na_block_layout(rows):
    kr = NA_ROWS_MAX
    nblk = rows // NA_QROWS
    starts, patterns, types = [], [], []
    for j in range(nblk):
        u = int(np.clip(NA_QROWS * j - kr // 2, 0, rows - NA_KROWS))
        r = NA_QROWS * j + np.arange(NA_QROWS)
        start_r = np.clip(r - kr // 2, 0, rows - kr)
        kabs = u + np.arange(NA_KROWS)
        valid = (kabs[None, :] >= start_r[:, None]) & (kabs[None, :] < start_r[:, None] + kr)
        assert valid.sum(axis=1).min() == kr
        dr = np.clip(kabs[None, :] - r[:, None] + NA_ROWS_MAX - 1, 0, 2 * NA_ROWS_MAX - 2)
        key = (valid.tobytes(), dr.tobytes())
        keys = [p[0] for p in patterns]
        if key not in keys:
            patterns.append((key, valid, dr))
        types.append([p[0] for p in patterns].index(key))
        starts.append(u)
    return starts, types, [(p[1], p[2]) for p in patterns]


NA_DR = 2 * NA_ROWS_MAX - 1


def _na_bias_table(rpb):
    qcol = np.arange(GRID_W)
    col_start = np.clip(qcol - NA_COLS // 2, 0, GRID_W - NA_COLS)
    kcol = np.arange(GRID_W)
    col_mask = (kcol[None, :] >= col_start[:, None]) & (kcol[None, :] < col_start[:, None] + NA_COLS)
    dc = np.clip(kcol[None, :] - qcol[:, None] + NA_COLS - 1, 0, 2 * NA_COLS - 2)
    col_hot = (dc[None] == np.arange(2 * NA_COLS - 1)[:, None, None]).astype(np.float32)
    by_col = jnp.einsum("lhab,bqk->lhaqk", rpb.astype(F32), col_hot, precision=lax.Precision.HIGHEST)
    by_col = jnp.where(col_mask, by_col * LOG2E, NEG)
    masked = jnp.full(by_col.shape[:2] + (1, GRID_W, GRID_W), NEG, F32)
    table = jnp.concatenate([by_col, masked], axis=2)
    return jnp.concatenate([table, table], axis=-1)


def _na_pieces(patterns):
    out = []
    for valid, dr in patterns:
        idx = np.where(valid, dr, NA_DR)
        out.append(tuple(tuple((int(idx[r, 2 * p]), int(idx[r, 2 * p + 1])) for p in range(NA_KROWS // 2))
                         for r in range(NA_QROWS)))
    return tuple(out)


def _softmax_pv(s_list, v_list):
    m = None
    for s in s_list:
        mm = jnp.max(s, axis=-1, keepdims=True)
        m = mm if m is None else jnp.maximum(m, mm)
    num, den = None, None
    for s, v in zip(s_list, v_list):
        p = jnp.exp2(s - m)
        ssum = jnp.sum(p, axis=-1, keepdims=True)
        o = jnp.dot(p.astype(BF16), v, preferred_element_type=F32)
        num = o if num is None else num + o
        den = ssum if den is None else den + ssum
    return num / den


def _na_kernel(q_ref, k_ref, v_ref, cq_ref, ck_ref, cv_ref, tab_ref, cos_ref, sin_ref, y_ref, cy_ref,
               qs_ref, ks_ref, bias_ref, *, starts, types, pieces):
    lane = lax.broadcasted_iota(jnp.int32, (1, HEAD_DIM), 1)
    first = (lane % (HEAD_DIM // 2)) < HEAD_DIM // 4
    scale = HEAD_DIM ** -0.5 * LOG2E

    @pl.when(pl.program_id(1) == 0)
    def _():
        even = lane < GRID_W
        for tp, by_row in enumerate(pieces):
            for r, by_pair in enumerate(by_row):
                for p, (ie, io) in enumerate(by_pair):
                    bias_ref[tp, r * GRID_W:(r + 1) * GRID_W, 2 * p * GRID_W:2 * (p + 1) * GRID_W] = (
                        jnp.where(even, tab_ref[ie], tab_ref[io]))

    def rope(t):
        partner = jnp.where(first, pltpu.roll(t, HEAD_DIM - HEAD_DIM // 4, axis=1),
                            pltpu.roll(t, HEAD_DIM // 4, axis=1))
        return t * cos_ref[...] + partner * sin_ref[...]

    qs_ref[...] = (rope(q_ref[...].astype(F32)) * scale).astype(BF16)
    ks_ref[...] = rope(k_ref[...].astype(F32)).astype(BF16)
    ck = ck_ref[...]
    cv = cv_ref[...]
    qrows = NA_QROWS * GRID_W
    krows = NA_KROWS * GRID_W
    for j, (u, tp) in enumerate(zip(starts, types)):
        qb = qs_ref[j * qrows:(j + 1) * qrows, :]
        kb = ks_ref[u * GRID_W:u * GRID_W + krows, :]
        vb = v_ref[u * GRID_W:u * GRID_W + krows, :]
        s_loc = lax.dot_general(qb, kb, _NT, preferred_element_type=F32) + bias_ref[tp]
        s_ctx = lax.dot_general(qb, ck, _NT, preferred_element_type=F32)
        y_ref[j * qrows:(j + 1) * qrows, :] = _softmax_pv([s_loc, s_ctx], [vb, cv]).astype(y_ref.dtype)
    s = lax.dot_general(cq_ref[...], ck, _NT, preferred_element_type=F32) * scale
    cy_ref[...] = _softmax_pv([s], [cv]).astype(cy_ref.dtype)


def _attention(ub, col0, n, lc, bsz, heads, table, layer, layout):
    starts, types, patterns = layout
    cos, sin = _na_tables(n)
    cb = col0 // HEAD_DIM
    ctx0 = bsz * n // lc

    def lat(k):
        return pl.BlockSpec((n, HEAD_DIM), lambda h, b: (b, cb + k * heads + h))

    def ctx(k):
        return pl.BlockSpec((lc, HEAD_DIM), lambda h, b: (ctx0 + b, cb + k * heads + h))

    tab = pl.BlockSpec((n, HEAD_DIM), lambda h, b: (0, 0))
    return pl.pallas_call(
        functools.partial(_na_kernel, starts=tuple(starts), types=tuple(types), pieces=_na_pieces(patterns)),
        grid=(heads, bsz),
        in_specs=[lat(0), lat(1), lat(2), ctx(0), ctx(1), ctx(2),
                  pl.BlockSpec((None, None) + table.shape[2:], lambda h, b: (layer, h, 0, 0, 0)), tab, tab],
        out_specs=[pl.BlockSpec((n, HEAD_DIM), lambda h, b: (b, h)),
                   pl.BlockSpec((lc, HEAD_DIM), lambda h, b: (b, h))],
        out_shape=[jax.ShapeDtypeStruct((bsz * n, heads * HEAD_DIM), BF16),
                   jax.ShapeDtypeStruct((bsz * lc, heads * HEAD_DIM), BF16)],
        scratch_shapes=[pltpu.VMEM((n, HEAD_DIM), BF16), pltpu.VMEM((n, HEAD_DIM), BF16),
                        pltpu.VMEM((len(patterns), NA_QROWS * GRID_W, NA_KROWS * GRID_W), F32)],
        compiler_params=_params("arbitrary", "arbitrary"), name="attention",
    )(ub, ub, ub, ub, ub, ub, table, jnp.asarray(cos), jnp.asarray(sin))


def _route_kernel(afft_ref, slot_ref, *, n, ne, cap):
    bits = lax.bitcast_convert_type(afft_ref[...], jnp.int32)

    def bisect(i, thr):
        cand = thr | jnp.left_shift(jnp.int32(1), F32_VALUE_BITS - 1 - i)
        cnt = jnp.sum((bits >= cand).astype(F32), axis=1, keepdims=True)
        return jnp.where(cnt >= cap, cand, thr)

    thr = lax.fori_loop(0, F32_VALUE_BITS, bisect, jnp.zeros((ne, 1), jnp.int32))
    above = bits > thr
    tied = bits == thr
    need = cap - jnp.sum(above.astype(F32), axis=1, keepdims=True)

    row = lax.broadcasted_iota(jnp.int32, (LANES, LANES), 0)
    colm = lax.broadcasted_iota(jnp.int32, (LANES, LANES), 1)
    before = (row < colm).astype(BF16)
    ones = jnp.ones((LANES, LANES), BF16)

    def prefix(x):
        xb = x.astype(BF16)
        outs, carry = [], jnp.zeros((ne, LANES), F32)
        for blk in range(n // LANES):
            xs = xb[:, blk * LANES:(blk + 1) * LANES]
            outs.append(jnp.dot(xs, before, preferred_element_type=F32) + carry)
            carry = carry + jnp.dot(xs, ones, preferred_element_type=F32)
        return jnp.concatenate(outs, axis=1)

    chosen = above | (tied & (prefix(tied.astype(F32)) < need))
    slot = prefix(chosen.astype(F32)).astype(jnp.int32)
    slot_ref[...] = jnp.where(chosen, slot, n)


def _route(aff, row0, n, bsz, cap):
    ne = aff.shape[1]
    at = jnp.swapaxes(aff[row0:row0 + bsz * n].reshape(bsz, n, ne), 1, 2)
    spec = pl.BlockSpec((bsz * ne, n), lambda i: (0, 0))
    slot = pl.pallas_call(
        functools.partial(_route_kernel, n=n, ne=bsz * ne, cap=cap),
        grid=(1,), in_specs=[spec], out_specs=spec,
        out_shape=jax.ShapeDtypeStruct((bsz * ne, n), jnp.int32),
        compiler_params=_params("arbitrary"), name="expert_route",
    )(at.reshape(bsz * ne, n))
    return at, slot.reshape(bsz, ne, n)


def _gather_kernel(slot_ref, afft_ref, h_ref, xg_ref, g_ref, *, cap, eg):
    e0 = pl.program_id(1) * eg
    want = lax.broadcasted_iota(jnp.int32, (cap, 1), 0)
    sels = [slot_ref[pl.ds(e0 + k, 1), :] == want for k in range(eg)]
    sel = sels[0].astype(BF16) if eg == 1 else jnp.concatenate([s.astype(BF16) for s in sels], axis=0)
    xg = jnp.dot(sel, h_ref[...], preferred_element_type=F32).astype(xg_ref.dtype)
    xg_ref[...] = xg.reshape(xg_ref.shape)
    for k in range(eg):
        g_ref[k] = jnp.sum(jnp.where(sels[k], afft_ref[pl.ds(e0 + k, 1), :], 0.0), axis=1, keepdims=True)


def _gather(slot, afft, h2, row_block0, n, bsz, cap):
    ne = slot.shape[1]
    d = h2.shape[1]
    eg = max(1, min(ne, GATHER_ROWS // cap))
    while ne % eg:
        eg -= 1
    row_spec = pl.BlockSpec((None, ne, n), lambda b, e: (b, 0, 0))
    return pl.pallas_call(
        functools.partial(_gather_kernel, cap=cap, eg=eg),
        grid=(bsz, ne // eg),
        in_specs=[row_spec, row_spec, pl.BlockSpec((n, d), lambda b, e: (row_block0 + b, 0))],
        out_specs=[pl.BlockSpec((eg, cap, d), lambda b, e: (e, b, 0)),
                   pl.BlockSpec((eg, cap, 1), lambda b, e: (e, b, 0))],
        out_shape=[jax.ShapeDtypeStruct((ne, bsz * cap, d), BF16),
                   jax.ShapeDtypeStruct((ne, bsz * cap, 1), F32)],
        compiler_params=_params("arbitrary", "arbitrary"), name="expert_gather",
    )(slot, afft, h2)


def _ffn_kernel(*refs, ns, nf):
    x_refs = refs[0:2 * ns:2]
    g_refs = refs[1:2 * ns:2]
    wg_ref, wu_ref, wd_ref = refs[2 * ns:2 * ns + 3]
    o_refs = refs[2 * ns + 3:3 * ns + 3]
    wgb, wub, wdb = refs[3 * ns + 3:3 * ns + 6]
    acc_refs = refs[3 * ns + 6:]
    f = pl.program_id(1)
    wgb[...] = wg_ref[...].astype(BF16)
    wub[...] = wu_ref[...].astype(BF16)
    wdb[...] = wd_ref[...].astype(BF16)

    def ff_tile(first, last):
        for k in range(ns):
            rows = x_refs[k].shape[0]
            step = min(FFN_ROWS, rows)
            for r0 in range(0, rows, step):
                rs = slice(r0, r0 + step)
                x = x_refs[k][rs, :]
                hg = jnp.dot(x, wgb[...], preferred_element_type=F32)
                hu = jnp.dot(x, wub[...], preferred_element_type=F32)
                hid = (hg * _sigmoid(hg) * hu).astype(BF16)
                part = jnp.dot(hid, wdb[...], preferred_element_type=F32)
                if not first:
                    part = acc_refs[k][rs, :] + part
                if last:
                    o_refs[k][rs, :] = (part * g_refs[k][rs, :]).astype(o_refs[k].dtype)
                else:
                    acc_refs[k][rs, :] = part

    if nf == 1:
        ff_tile(True, True)
    else:
        pl.when(f == 0)(functools.partial(ff_tile, True, False))
        pl.when(f == nf - 1)(functools.partial(ff_tile, False, True))
        if nf > 2:
            pl.when((f > 0) & (f < nf - 1))(functools.partial(ff_tile, False, False))


def _expert_ffn(xgs, gates, w_gate, w_up, w_down, layer):
    ns = len(xgs)
    _, ne, d, ff = w_gate.shape
    tf = _lane_tile(ff, FF_TILE)
    nf = ff // tf
    in_specs, out_specs, out_shape, acc_scr, args = [], [], [], [], []
    for k in range(ns):
        rows = xgs[k].shape[1]
        in_specs.append(pl.BlockSpec((None, rows, d), lambda e, f: (e, 0, 0)))
        in_specs.append(pl.BlockSpec((None, rows, 1), lambda e, f: (e, 0, 0)))
        out_specs.append(pl.BlockSpec((None, rows, d), lambda e, f: (e, 0, 0)))
        out_shape.append(jax.ShapeDtypeStruct((ne, rows, d), BF16))
        acc_scr.append(pltpu.VMEM((rows, d), F32))
        args += [xgs[k], gates[k]]
    in_specs += [pl.BlockSpec((None, None, d, tf), lambda e, f: (layer, e, 0, f)),
                 pl.BlockSpec((None, None, d, tf), lambda e, f: (layer, e, 0, f)),
                 pl.BlockSpec((None, None, tf, d), lambda e, f: (layer, e, f, 0))]
    return pl.pallas_call(
        functools.partial(_ffn_kernel, ns=ns, nf=nf),
        grid=(ne, nf),
        in_specs=in_specs, out_specs=out_specs, out_shape=out_shape,
        scratch_shapes=[pltpu.VMEM((d, tf), BF16), pltpu.VMEM((d, tf), BF16), pltpu.VMEM((tf, d), BF16)] + acc_scr,
        compiler_params=_params("arbitrary", "arbitrary"), name="expert_ffn",
    )(*args, w_gate, w_up, w_down)


def _scatter_kernel(slot_ref, y_ref, x_ref, mod_ref, o_ref, *, ne, cap):
    sl = slot_ref[...]
    want = lax.broadcasted_iota(jnp.int32, (1, cap), 1)
    acc = None
    for e in range(ne):
        sel = (sl[:, e:e + 1] == want).astype(BF16)
        p = jnp.dot(sel, y_ref[e], preferred_element_type=F32)
        acc = p if acc is None else acc + p
    o_ref[...] = x_ref[...] + mod_ref[5:6, :] * acc


def _scatter(slot, y, x, mod, layer, mod_row0, row0, n, bsz, cap, tn):
    ne = slot.shape[1]
    d = x.shape[1]
    tm = min(SCATTER_ROWS, n)
    slot_t = jnp.swapaxes(slot, 1, 2)
    xb0 = row0 // tm
    per_seq = n // tm
    return pl.pallas_call(
        functools.partial(_scatter_kernel, ne=ne, cap=cap),
        grid=(bsz, d // tn, per_seq),
        in_specs=[pl.BlockSpec((None, tm, ne), lambda b, j, i: (b, i, 0)),
                  pl.BlockSpec((ne, cap, tn), lambda b, j, i: (0, b, j)),
                  pl.BlockSpec((tm, tn), lambda b, j, i: (xb0 + b * per_seq + i, j)),
                  pl.BlockSpec((None, None, 6, tn), lambda b, j, i: (layer, mod_row0(b), 0, j))],
        out_specs=pl.BlockSpec((tm, tn), lambda b, j, i: (b * per_seq + i, j)),
        out_shape=jax.ShapeDtypeStruct((bsz * n, d), F32),
        compiler_params=_params("arbitrary", "arbitrary", "arbitrary"), name="expert_scatter",
    )(slot_t, y, x, mod)


def kernel(x, c, ctx, c_ctx, w_mod, b_mod, g_norm1, w_in, lb_param, g_hgrn, w_pool, pool_scale, rpb,
           w_branch, w_out, g_norm2, w_router, w_gate_e, w_up_e, w_down_e, g_final):
    bsz, n, d = x.shape
    lc = ctx.shape[1]
    depth = w_mod.shape[0]
    width = lb_param.shape[2]
    heads = width // HEAD_DIM
    ne = w_router.shape[2]
    n_lat = bsz * n
    n_ctx = bsz * lc
    total = n_lat + n_ctx
    assert bsz + 1 <= 8 and n % ROW_TILE == 0 and lc % ROW_TILE == 0 and n_lat % lc == 0
    assert w_pool.shape[-1] * 4 == width and w_branch.shape[2] == width
    tm = _row_tile(n, n_ctx)
    tn = _lane_tile(d, MERGE_COL_TILE)
    tn_wide = _lane_tile(d, WIDE_COL_TILE)

    lb_all = jnp.cumsum(jax.nn.softmax(lb_param.astype(F32), axis=1), axis=1)
    lb_all = lb_all - lb_all[:, :1]
    lbp_all = jnp.stack([jnp.log(lb_all), jnp.log1p(-lb_all)], axis=2) * LOG2E
    lbp_all = lbp_all.reshape(2, depth, 2, heads, LANES).transpose(0, 1, 3, 2, 4)

    c8 = jnp.concatenate([c, c_ctx[None], jnp.zeros((8 - bsz - 1, d), F32)], axis=0)
    mod = _modulation(c8, w_mod, b_mod).reshape(depth, 8, 6, d)

    cm3, pair_masks, upper = _hgrn_constants()
    hgrn_consts = (jnp.asarray(cm3, BF16), jnp.asarray(pair_masks, F32), jnp.asarray(upper, F32))
    na_layout = _na_block_layout(n // GRID_W)
    na_bias = _na_bias_table(rpb)
    lanes_e = -(-ne // LANES) * LANES
    w_router_p = jnp.pad(w_router, ((0, 0), (0, 0), (0, lanes_e - ne)))

    x_lat = x.reshape(n_lat, d)
    x_ctx = ctx.reshape(n_ctx, d)
    zero_state = jnp.zeros((bsz, heads, LANES, LANES), F32)
    cap = EC_CAPACITY * n // ne
    cap_c = EC_CAPACITY * lc // ne
    a_cols = 5 * width
    b_cols = w_in.shape[2] - a_cols
    tn_in = _lane_tile(math.gcd(a_cols, b_cols), WIDE_COL_TILE)
    gate_col0 = 4 * width
    rt_all = _Rows(tm, n, bsz, n_lat, total)
    rt_lat = _Rows(tm, n, bsz, n_lat, n_lat)
    tm_norm = min(NORM_ROWS, tm)
    rn_all = _Rows(tm_norm, n, bsz, n_lat, total)
    rn_lat = _Rows(tm_norm, n, bsz, n_lat, n_lat)
    tm_out = max(tm // 2, ROW_TILE)
    ro_all = _Rows(tm_out, n, bsz, n_lat, total)
    ro_lat = _Rows(tm_out, n, bsz, n_lat, n_lat)

    for l in range(depth):
        last = l == depth - 1
        rt, rn, ro = (rt_lat, rn_lat, ro_lat) if last else (rt_all, rn_all, ro_all)
        gain_h = g_hgrn[l].reshape(heads, 1, LANES)

        h = _norm1(x_lat, x_ctx, g_norm1[l], mod, l, rn_all)
        ua = _matmul(h, w_in, l, 0, a_cols, F32, tm, tn_in)
        ub = _matmul(h, w_in, l, a_cols, b_cols, BF16, tm, tn_in)

        cy_a, s_f, s_b = _hgrn(ua, n_lat // lc, lc, bsz, heads, lbp_all[:, l], gain_h, zero_state, zero_state,
                               hgrn_consts)
        y_a, _, _ = _hgrn(ua, 0, n, bsz, heads, lbp_all[:, l], gain_h, s_f, s_b, hgrn_consts)
        y_b = _pool(ub, 0, n, bsz, w_pool, pool_scale, l)
        y_c, cy_c = _attention(ub, width, n, lc, bsz, heads, na_bias, l, na_layout)
        ys_lat = (y_a, y_b, y_c)
        ys_ctx = ys_lat if last else (cy_a, _pool(ub, n_lat // lc, lc, bsz, w_pool, pool_scale, l), cy_c)

        merged = _merge(ys_lat, ys_ctx, ub, gate_col0, w_branch, l, rt, tn)
        x_mid = _out_proj(merged, w_out, l, x_lat, x_ctx, mod, ro, tn_wide)

        h2, aff = _norm2(x_mid, g_norm2[l], mod, l, rn, w_router_p, ne)
        afft, slot = _route(aff, 0, n, bsz, cap)
        xg, gate = _gather(slot, afft, h2, 0, n, bsz, cap)
        xgs, gates = [xg], [gate]
        if not last:
            afft_c, slot_c = _route(aff, n_lat, lc, bsz, cap_c)
            xg_c, gate_c = _gather(slot_c, afft_c, h2, n_lat // lc, lc, bsz, cap_c)
            xgs, gates = xgs + [xg_c], gates + [gate_c]
        ys = _expert_ffn(xgs, gates, w_gate_e, w_up_e, w_down_e, l)
        x_lat = _scatter(slot, ys[0], x_mid, mod, l, lambda b: b, 0, n, bsz, cap, tn_wide)
        if not last:
            x_ctx = _scatter(slot_c, ys[1], x_mid, mod, l, lambda b: bsz, n_lat, lc, bsz, cap_c, tn_wide)

    return _final_norm(x_lat, g_final, tm_norm).reshape(bsz, n, d)
```

```python
import functools
import math

import numpy as np
import jax
import jax.numpy as jnp
from jax import lax
from jax.experimental import pallas as pl
from jax.experimental.pallas import tpu as pltpu

F32 = jnp.float32
BF16 = jnp.bfloat16
EPS = 1e-6
LANES = 128
HEAD_DIM = 128
CHUNK = 64
HGRN_GROUP = 4
GRID_W = 64
POOL_WINDOWS = (2, 4, 8, 16)
POOL_PAD = 16
NA_ROWS_MAX = 8
NA_COLS = 16
NA_QROWS = 4
NA_KROWS = 12
ROPE_THETA = 10000.0
EC_CAPACITY = 2
GATHER_ROWS = 512
FFN_ROWS = 512
NEG = -1e30
ROW_TILE = 256
NORM_ROWS = 512
MAX_ROW_TILE = 1024
WIDE_COL_TILE = 1024
MERGE_COL_TILE = 512
FF_TILE = 256
SCATTER_ROWS = 512
F32_SIGN_BIT = 0x80000000
F32_VALUE_BITS = 31
LOG2E = 1.4426950408889634
VMEM_LIMIT = 56 * 2 ** 20


def _params(*sem):
    return pltpu.CompilerParams(dimension_semantics=sem, vmem_limit_bytes=VMEM_LIMIT)


def _sigmoid(x):
    return 1.0 / (1.0 + jnp.exp(-x))


def _neg_abs(x):
    bits = lax.bitcast_convert_type(x, jnp.uint32) | jnp.uint32(F32_SIGN_BIT)
    return lax.bitcast_convert_type(bits, F32)


def _lane_tile(n, limit):
    t = min(limit, n) // LANES * LANES
    while n % t:
        t -= LANES
    return t


def _row_tile(n, ctx_rows):
    tm = MAX_ROW_TILE
    while n % tm or ctx_rows % tm:
        tm //= 2
    return tm


class _Rows:
    def __init__(self, tm, n, bsz, n_lat, rows):
        self.tm, self.n, self.bsz = tm, n, bsz
        self.lat_tiles = n_lat // tm
        self.tiles = rows // tm
        self.ctx_tiles = max(self.tiles - self.lat_tiles, 1)

    def lat(self, i):
        return jnp.minimum(i, self.lat_tiles - 1)

    def ctx(self, i):
        return jnp.clip(i - self.lat_tiles, 0, self.ctx_tiles - 1)

    def mod_row(self, i):
        return jnp.minimum(i * self.tm // self.n, self.bsz)


def _pick(is_lat, lat_ref, ctx_ref):
    return jnp.where(is_lat, lat_ref[...], ctx_ref[...])


def _mod_kernel(c_ref, w_ref, b_ref, o_ref):
    c = c_ref[...]
    sc = (c * _sigmoid(c)).astype(BF16)
    o_ref[...] = jnp.dot(sc, w_ref[...].astype(BF16), preferred_element_type=F32) + b_ref[...]


def _modulation(c8, w_mod, b_mod):
    depth, d, n6 = w_mod.shape
    tn = _lane_tile(n6, WIDE_COL_TILE)
    return pl.pallas_call(
        _mod_kernel,
        grid=(depth, n6 // tn),
        in_specs=[pl.BlockSpec((8, d), lambda l, j: (0, 0)),
                  pl.BlockSpec((None, d, tn), lambda l, j: (l, 0, j)),
                  pl.BlockSpec((None, 1, tn), lambda l, j: (l, 0, j))],
        out_specs=pl.BlockSpec((None, 8, tn), lambda l, j: (l, 0, j)),
        out_shape=jax.ShapeDtypeStruct((depth, 8, n6), F32),
        compiler_params=_params("arbitrary", "arbitrary"),
        name="modulation",
    )(c8, w_mod, b_mod.reshape(depth, 1, n6))


def _rmsnorm(x, gain):
    return x * lax.rsqrt(jnp.mean(x * x, axis=-1, keepdims=True) + EPS) * gain


def _modulate(y, mod_ref, shift_idx):
    return y * (1.0 + mod_ref[shift_idx + 1:shift_idx + 2, :]) + mod_ref[shift_idx:shift_idx + 1, :]


def _norm1_kernel(xl_ref, xc_ref, g_ref, mod_ref, o_ref, *, lat_tiles):
    x = _pick(pl.program_id(0) < lat_tiles, xl_ref, xc_ref)
    o_ref[...] = _modulate(_rmsnorm(x, g_ref[...]), mod_ref, 0).astype(o_ref.dtype)


def _norm2_kernel(x_ref, g_ref, mod_ref, wr_ref, o_ref, aff_ref):
    h = _modulate(_rmsnorm(x_ref[...], g_ref[...]), mod_ref, 3)
    hb = h.astype(BF16)
    o_ref[...] = hb
    h_lo = (h - hb.astype(F32)).astype(BF16)
    w = wr_ref[...]
    wb = w.astype(BF16)
    w_lo = (w - wb.astype(F32)).astype(BF16)
    logits = (lax.dot_general(wb, hb, _NT, preferred_element_type=F32)
              + lax.dot_general(wb, h_lo, _NT, preferred_element_type=F32)
              + lax.dot_general(w_lo, hb, _NT, preferred_element_type=F32))
    logits = logits[:aff_ref.shape[0], :]
    e = jnp.exp(logits - jnp.max(logits, axis=0, keepdims=True))
    aff_ref[...] = e / jnp.sum(e, axis=0, keepdims=True)


def _final_norm_kernel(x_ref, g_ref, o_ref):
    o_ref[...] = _rmsnorm(x_ref[...], g_ref[...])


def _norm1(x_lat, x_ctx, gain, mod, layer, rt):
    d = x_lat.shape[1]
    return pl.pallas_call(
        functools.partial(_norm1_kernel, lat_tiles=rt.lat_tiles),
        grid=(rt.tiles,),
        in_specs=[pl.BlockSpec((rt.tm, d), lambda i: (rt.lat(i), 0)),
                  pl.BlockSpec((rt.tm, d), lambda i: (rt.ctx(i), 0)),
                  pl.BlockSpec((1, d), lambda i: (0, 0)),
                  pl.BlockSpec((None, None, 6, d), lambda i: (layer, rt.mod_row(i), 0, 0))],
        out_specs=pl.BlockSpec((rt.tm, d), lambda i: (i, 0)),
        out_shape=jax.ShapeDtypeStruct((rt.tiles * rt.tm, d), BF16),
        compiler_params=_params("arbitrary"), name="norm1",
    )(x_lat, x_ctx, gain.reshape(1, d), mod)


def _norm2(x, gain, mod, layer, rt, w_router_padded, ne):
    d = x.shape[1]
    rows = rt.tiles * rt.tm
    return pl.pallas_call(
        _norm2_kernel,
        grid=(rt.tiles,),
        in_specs=[pl.BlockSpec((rt.tm, d), lambda i: (i, 0)),
                  pl.BlockSpec((1, d), lambda i: (0, 0)),
                  pl.BlockSpec((None, None, 6, d), lambda i: (layer, rt.mod_row(i), 0, 0)),
                  pl.BlockSpec((None, w_router_padded.shape[1], d), lambda i: (layer, 0, 0))],
        out_specs=[pl.BlockSpec((rt.tm, d), lambda i: (i, 0)), pl.BlockSpec((ne, rt.tm), lambda i: (0, i))],
        out_shape=[jax.ShapeDtypeStruct((rows, d), BF16), jax.ShapeDtypeStruct((ne, rows), F32)],
        compiler_params=_params("arbitrary"), name="norm2_router",
    )(x, gain.reshape(1, d), mod, w_router_padded)


def _final_norm(x, gain, tm):
    rows, d = x.shape
    return pl.pallas_call(
        _final_norm_kernel,
        grid=(rows // tm,),
        in_specs=[pl.BlockSpec((tm, d), lambda i: (i, 0)), pl.BlockSpec((1, d), lambda i: (0, 0))],
        out_specs=pl.BlockSpec((tm, d), lambda i: (i, 0)),
        out_shape=jax.ShapeDtypeStruct((rows, d), F32),
        compiler_params=_params("arbitrary"), name="final_norm",
    )(x, gain.reshape(1, d))


def _mm_kernel(a_ref, w_ref, o_ref, wbf_ref):
    @pl.when(pl.program_id(1) == 0)
    def _():
        wbf_ref[...] = w_ref[...].astype(BF16)
    o_ref[...] = jnp.dot(a_ref[...], wbf_ref[...], preferred_element_type=F32).astype(o_ref.dtype)


def _matmul(a, w_all, layer, col0, ncols, out_dtype, tm, tn):
    m, k = a.shape
    off = col0 // tn
    return pl.pallas_call(
        _mm_kernel,
        grid=(ncols // tn, m // tm),
        in_specs=[pl.BlockSpec((tm, k), lambda j, i: (i, 0)),
                  pl.BlockSpec((None, k, tn), lambda j, i: (layer, 0, j + off))],
        out_specs=pl.BlockSpec((tm, tn), lambda j, i: (i, j)),
        out_shape=jax.ShapeDtypeStruct((m, ncols), out_dtype),
        scratch_shapes=[pltpu.VMEM((k, tn), BF16)],
        compiler_params=_params("arbitrary", "arbitrary"), name="matmul",
    )(a, w_all)


def _out_proj_kernel(a_ref, w_ref, xl_ref, xc_ref, mod_ref, o_ref, wbf_ref, *, lat_tiles):
    @pl.when(pl.program_id(1) == 0)
    def _():
        wbf_ref[...] = w_ref[...].astype(BF16)
    acc = jnp.dot(a_ref[...], wbf_ref[...], preferred_element_type=F32)
    x = _pick(pl.program_id(1) < lat_tiles, xl_ref, xc_ref)
    o_ref[...] = x + mod_ref[2:3, :] * acc


def _out_proj(a, w_all, layer, x_lat, x_ctx, mod, rt, tn):
    k = a.shape[1]
    d = w_all.shape[2]
    tm = rt.tm
    return pl.pallas_call(
        functools.partial(_out_proj_kernel, lat_tiles=rt.lat_tiles),
        grid=(d // tn, rt.tiles),
        in_specs=[pl.BlockSpec((tm, k), lambda j, i: (i, 0)),
                  pl.BlockSpec((None, k, tn), lambda j, i: (layer, 0, j)),
                  pl.BlockSpec((tm, tn), lambda j, i: (rt.lat(i), j)),
                  pl.BlockSpec((tm, tn), lambda j, i: (rt.ctx(i), j)),
                  pl.BlockSpec((None, None, 6, tn), lambda j, i: (layer, rt.mod_row(i), 0, j))],
        out_specs=pl.BlockSpec((tm, tn), lambda j, i: (i, j)),
        out_shape=jax.ShapeDtypeStruct((rt.tiles * tm, d), F32),
        scratch_shapes=[pltpu.VMEM((k, tn), BF16)],
        compiler_params=_params("arbitrary", "arbitrary"), name="out_proj",
    )(a, w_all, x_lat, x_ctx, mod)


def _merge_kernel(ya_ref, yb_ref, yc_ref, cya_ref, cyb_ref, cyc_ref, g0_ref, g1_ref, g2_ref, w_ref, o_ref, wbf_ref,
                  *, lat_tiles):
    @pl.when(pl.program_id(1) == 0)
    def _():
        wbf_ref[...] = w_ref[...].astype(BF16)
    is_lat = pl.program_id(1) < lat_tiles
    acc = None
    for j, (y_ref, cy_ref, g_ref) in enumerate(((ya_ref, cya_ref, g0_ref), (yb_ref, cyb_ref, g1_ref),
                                                (yc_ref, cyc_ref, g2_ref))):
        p = jnp.dot(_pick(is_lat, y_ref, cy_ref), wbf_ref[j], preferred_element_type=F32)
        t = _sigmoid(g_ref[...].astype(F32)) * p
        acc = t if acc is None else acc + t
    o_ref[...] = acc.astype(o_ref.dtype)


def _merge(ys_lat, ys_ctx, ub, gate_col0, w_branch, layer, rt, tn):
    bw = ys_lat[0].shape[1]
    d = w_branch.shape[3]
    tm = rt.tm
    lat_spec = pl.BlockSpec((tm, bw), lambda j, i: (rt.lat(i), 0))
    ctx_spec = pl.BlockSpec((tm, bw), lambda j, i: (rt.ctx(i), 0))

    def gate_spec(k):
        off = (gate_col0 + k * d) // tn
        return pl.BlockSpec((tm, tn), lambda j, i: (i, off + j))

    return pl.pallas_call(
        functools.partial(_merge_kernel, lat_tiles=rt.lat_tiles),
        grid=(d // tn, rt.tiles),
        in_specs=[lat_spec] * 3 + [ctx_spec] * 3 + [gate_spec(0), gate_spec(1), gate_spec(2),
                  pl.BlockSpec((None, 3, bw, tn), lambda j, i: (layer, 0, 0, j))],
        out_specs=pl.BlockSpec((tm, tn), lambda j, i: (i, j)),
        out_shape=jax.ShapeDtypeStruct((rt.tiles * tm, d), BF16),
        scratch_shapes=[pltpu.VMEM((3, bw, tn), BF16)],
        compiler_params=_params("arbitrary", "arbitrary"), name="merge",
    )(*ys_lat, *ys_ctx, ub, ub, ub, w_branch)


_HGRN_BLOCKS = (32, 16, 8, 4, 2, 1)


def _hgrn_constants():
    c = CHUNK
    t = np.arange(c)
    tri_f = (t[:, None] >= t[None, :]).astype(np.float32)
    tri_b = np.ascontiguousarray(tri_f[::-1, ::-1])
    sel_f, sel_b, masks, upper = [tri_f], [tri_b], [], []
    for m in _HGRN_BLOCKS:
        blk = (t // (2 * m)) * (2 * m)
        up = (t % (2 * m)) >= m
        sel_f.append(tri_f - tri_f[blk + m - 1])
        sel_b.append(tri_b - tri_b[blk + m])
        same = (t[:, None] // (2 * m)) == (t[None, :] // (2 * m))
        masks.append(same & (up[:, None] != up[None, :]))
        upper.append(np.broadcast_to(up[:, None], (c, LANES)))
    masks.append(np.eye(c, dtype=bool))
    sel = np.stack([np.concatenate(sel_f, axis=0), np.concatenate(sel_b, axis=0)])
    return (np.concatenate([sel, sel, sel], axis=2), np.stack(masks).astype(np.float32),
            np.stack(upper).astype(np.float32))


def _hgrn_log2_gate(z, lbp):
    zl = z * LOG2E
    log_sig = jnp.minimum(zl, 0.0) - jnp.log2(1.0 + jnp.exp2(_neg_abs(zl)))
    a = lbp[0:1, :]
    cc = lbp[1:2, :] + log_sig
    return jnp.maximum(a, cc) + jnp.log2(1.0 + jnp.exp2(_neg_abs(a - cc)))


def _split3(x):
    g1 = x.astype(BF16)
    r1 = x - g1.astype(F32)
    g2 = r1.astype(BF16)
    g3 = (r1 - g2.astype(F32)).astype(BF16)
    return jnp.concatenate([g1, g2, g3], axis=0)


_NT = (((1,), (1,)), ((), ()))
_TN = (((0,), (0,)), ((), ()))


def _hgrn_kernel(q_ref, ff_ref, fb_ref, i_ref, g_ref, lbf_ref, lbb_ref, gain_ref, cm_ref, mask_ref, up_ref,
                 sf0_ref, sb0_ref, y_ref, sf_ref, sb_ref, o_scr, qt_scr, upd_scr, dec_scr, st_scr, sums_scr, k_scr,
                 *, nc):
    z_refs = (ff_ref, fb_ref)
    lb_refs = (lbf_ref, lbb_ref)
    nl = len(_HGRN_BLOCKS)
    last_row = (CHUNK - 1, 0)

    groups = nc // HGRN_GROUP
    srows = (1 + nl) * CHUNK

    def gate_pass(it, carry):
        rows = [pl.ds(pl.multiple_of((it * HGRN_GROUP + s) * CHUNK, CHUNK), CHUNK) for s in range(HGRN_GROUP)]
        for d in range(2):
            logf = [_hgrn_log2_gate(z_refs[d][r, :], lb_refs[d][...]) for r in rows]
            sums_scr[d, pl.ds(pl.multiple_of(it * srows, srows), srows), :] = jnp.dot(
                cm_ref[d], jnp.concatenate([_split3(lf) for lf in logf], axis=1), preferred_element_type=F32)
            for r, lf in zip(rows, logf):
                k_scr[d, r, :] = 1.0 - jnp.exp2(lf)
        return carry

    lax.fori_loop(0, groups, gate_pass, 0, unroll=min(4, groups))

    def local_pass(it, carry):
        sums = [sums_scr[d, pl.ds(pl.multiple_of(it * srows, srows), srows), :] for d in range(2)]
        rows = [pl.ds(pl.multiple_of((it * HGRN_GROUP + s) * CHUNK, CHUNK), CHUNK) for s in range(HGRN_GROUP)]
        up = up_ref[...] > 0.5
        for s, r in enumerate(rows):
            lanes = slice(s * LANES, (s + 1) * LANES)
            qr = q_ref[r, :]
            q = qr * _sigmoid(qr)
            vb = i_ref[r, :].astype(BF16)
            b = [sums[d][0:CHUNK, lanes] for d in range(2)]
            lvl = [jnp.exp2(_neg_abs(sums[d][CHUNK:, lanes].reshape(nl, CHUNK, LANES))) for d in range(2)]
            k = [k_scr[d, r, :] for d in range(2)]
            q_dec = jnp.where(up, lvl[0], lvl[1])
            k_dec = jnp.where(up, k[1][None] * lvl[1], k[0][None] * lvl[0])
            qs = jnp.concatenate([q[None] * q_dec, q[None]], axis=0).astype(BF16)
            ks = jnp.concatenate([k_dec, (k[0] + k[1])[None]], axis=0).astype(BF16)
            sc = jnp.einsum("ltk,lsk->lts", qs, ks, preferred_element_type=F32)
            amat = jnp.sum(sc * mask_ref[...], axis=0)
            o_scr[r, :] = jnp.dot(amat.astype(BF16), vb, preferred_element_type=F32)
            tot = [b[d][last_row[d]:last_row[d] + 1, :] for d in range(2)]
            kd = jnp.concatenate([(k[d] * jnp.exp2(tot[d] - b[d])).astype(BF16) for d in range(2)], axis=1)
            upd = lax.dot_general(vb, kd, _TN, preferred_element_type=F32)
            j = it * HGRN_GROUP + s
            for d in range(2):
                qt_scr[d, r, :] = (q * jnp.exp2(b[d])).astype(BF16)
                upd_scr[d, j] = upd[:, d * LANES:(d + 1) * LANES]
                dec_scr[d, pl.ds(j, 1), :] = jnp.exp2(tot[d])
        return carry

    lax.fori_loop(0, groups, local_pass, 0, unroll=min(4, groups))

    st_scr[0] = sf0_ref[...]
    st_scr[1] = sb0_ref[...]

    def state_pass(it, carry):
        for d in range(2):
            j = it if d == 0 else nc - 1 - it
            rows = pl.ds(pl.multiple_of(j * CHUNK, CHUNK), CHUNK)
            st = st_scr[d]
            o_scr[rows, :] += lax.dot_general(qt_scr[d, rows, :], st.astype(BF16), _NT, preferred_element_type=F32)
            st_scr[d] = st * dec_scr[d, pl.ds(j, 1), :] + upd_scr[d, j]
        return carry

    lax.fori_loop(0, nc, state_pass, 0, unroll=min(8, nc))
    sf_ref[...] = st_scr[0]
    sb_ref[...] = st_scr[1]
    o = o_scr[...]
    o = o * lax.rsqrt(jnp.mean(o * o, axis=-1, keepdims=True) + EPS) * gain_ref[...]
    g = g_ref[...]
    y_ref[...] = (o * (g * _sigmoid(g))).astype(y_ref.dtype)


def _hgrn(ua, row_block0, n, bsz, heads, lbp, gain, sf0, sb0, consts):
    nc = n // CHUNK
    assert nc % HGRN_GROUP == 0 and nc % 2 == 0

    def col(k):
        return pl.BlockSpec((n, LANES), lambda b, h: (row_block0 + b, k * heads + h))

    lb_spec = lambda d: pl.BlockSpec((None, None, 2, LANES), lambda b, h: (d, h, 0, 0))
    st_spec = pl.BlockSpec((None, None, LANES, LANES), lambda b, h: (b, h, 0, 0))
    st_shape = jax.ShapeDtypeStruct((bsz, heads, LANES, LANES), F32)
    const_specs = [pl.BlockSpec(a.shape, lambda b, h, nd=a.ndim: (0,) * nd) for a in consts]
    return pl.pallas_call(
        functools.partial(_hgrn_kernel, nc=nc),
        grid=(bsz, heads),
        in_specs=[col(0), col(1), col(2), col(3), col(4), lb_spec(0), lb_spec(1),
                  pl.BlockSpec((None, 1, LANES), lambda b, h: (h, 0, 0))] + const_specs + [st_spec, st_spec],
        out_specs=[pl.BlockSpec((n, LANES), lambda b, h: (b, h)), st_spec, st_spec],
        out_shape=[jax.ShapeDtypeStruct((bsz * n, heads * LANES), BF16), st_shape, st_shape],
        scratch_shapes=[pltpu.VMEM((n, LANES), F32), pltpu.VMEM((2, n, LANES), BF16),
                        pltpu.VMEM((2, nc, LANES, LANES), F32), pltpu.VMEM((2, nc, LANES), F32),
                        pltpu.VMEM((2, LANES, LANES), F32),
                        pltpu.VMEM((2, nc * (1 + len(_HGRN_BLOCKS)) * CHUNK // HGRN_GROUP, HGRN_GROUP * LANES), F32),
                        pltpu.VMEM((2, n, LANES), F32)],
        compiler_params=_params("arbitrary", "arbitrary"), name="hgrn2",
    )(ua, ua, ua, ua, ua, lbp, lbp, gain, *consts, sf0, sb0)


def _pool_kernel(u_ref, w_ref, s_ref, y_ref, pad_ref, buf_a, buf_b, *, n, group):
    pos = lax.broadcasted_iota(jnp.int32, (n, 1), 0)
    zeros = jnp.zeros((POOL_PAD, pad_ref.shape[1]), F32)
    pad_ref[0:POOL_PAD, :] = zeros
    pad_ref[POOL_PAD + n:2 * POOL_PAD + n, :] = zeros
    pad_ref[POOL_PAD:POOL_PAD + n, :] = u_ref[...].astype(F32)
    live = POOL_PAD + n
    for buf in (buf_a, buf_b):
        buf[live:live + POOL_PAD, :] = jnp.zeros((POOL_PAD, group), F32)
    for gi, w in enumerate(POOL_WINDOWS):
        cols = slice(gi * group, (gi + 1) * group)
        k, dst = 1, buf_a
        dst[0:live, :] = pad_ref[0:live, cols] + pad_ref[1:live + 1, cols]
        while 2 * k < w:
            k *= 2
            src, dst = dst, (buf_b if dst is buf_a else buf_a)
            dst[0:live, :] = src[0:live, :] + src[k:live + k, :]
        acc = dst[POOL_PAD - w // 2:POOL_PAD - w // 2 + n, :]
        lo = jnp.maximum(pos - w // 2, 0)
        hi = jnp.minimum(pos + w // 2 - 1, n - 1)
        cnt = (hi - lo + 1).astype(F32)
        dd = acc / cnt - pad_ref[POOL_PAD:POOL_PAD + n, cols]
        y = jnp.dot(dd.astype(BF16), w_ref[gi].astype(BF16), preferred_element_type=F32)
        y_ref[:, cols] = (y * s_ref[:, cols]).astype(y_ref.dtype)


def _pool(ub, row_block0, n, bsz, w_pool, scale, layer):
    group = w_pool.shape[-1]
    width = 4 * group
    assert all(w & (w - 1) == 0 and 2 <= w <= 2 * POOL_PAD for w in POOL_WINDOWS)
    return pl.pallas_call(
        functools.partial(_pool_kernel, n=n, group=group),
        grid=(bsz,),
        in_specs=[pl.BlockSpec((n, width), lambda b: (row_block0 + b, 0)),
                  pl.BlockSpec((None, 4, group, group), lambda b: (layer, 0, 0, 0)),
                  pl.BlockSpec((None, 1, width), lambda b: (layer, 0, 0))],
        out_specs=pl.BlockSpec((n, width), lambda b: (b, 0)),
        out_shape=jax.ShapeDtypeStruct((bsz * n, width), BF16),
        scratch_shapes=[pltpu.VMEM((n + 2 * POOL_PAD, width), F32),
                        pltpu.VMEM((n + 2 * POOL_PAD, group), F32), pltpu.VMEM((n + 2 * POOL_PAD, group), F32)],
        compiler_params=_params("arbitrary"), name="pool",
    )(ub, w_pool, scale.reshape(scale.shape[0], 1, width))


def _na_tables(n):
    pos = np.arange(n)
    half = HEAD_DIM // 2
    inv_freq = ROPE_THETA ** (-np.arange(0, half, 2, dtype=np.float64) / half)
    lane = np.arange(HEAD_DIM)
    p = np.where(lane[None, :] < half, (pos // GRID_W)[:, None], (pos % GRID_W)[:, None]).astype(np.float64)
    ang = p * inv_freq[lane % (half // 2)][None, :]
    sign = np.where((lane % half) < half // 2, -1.0, 1.0)[None, :]
    return np.cos(ang).astype(np.float32), (np.sin(ang) * sign).astype(np.float32)


def _na_block_layout(rows):
    kr = NA_ROWS_MAX
    nblk = rows // NA_QROWS
    starts, patterns, types = [], [], []
    for j in range(nblk):
        u = int(np.clip(NA_QROWS * j - kr // 2, 0, rows - NA_KROWS))
        r = NA_QROWS * j + np.arange(NA_QROWS)
        start_r = np.clip(r - kr // 2, 0, rows - kr)
        kabs = u + np.arange(NA_KROWS)
        valid = (kabs[None, :] >= start_r[:, None]) & (kabs[None, :] < start_r[:, None] + kr)
        assert valid.sum(axis=1).min() == kr
        dr = np.clip(kabs[None, :] - r[:, None] + NA_ROWS_MAX - 1, 0, 2 * NA_ROWS_MAX - 2)
        key = (valid.tobytes(), dr.tobytes())
        keys = [p[0] for p in patterns]
        if key not in keys:
            patterns.append((key, valid, dr))
        types.append([p[0] for p in patterns].index(key))
        starts.append(u)
    return starts, types, [(p[1], p[2]) for p in patterns]


NA_DR = 2 * NA_ROWS_MAX - 1


def _na_bias_table(rpb):
    qcol = np.arange(GRID_W)
    col_start = np.clip(qcol - NA_COLS // 2, 0, GRID_W - NA_COLS)
    kcol = np.arange(GRID_W)
    col_mask = (kcol[None, :] >= col_start[:, None]) & (kcol[None, :] < col_start[:, None] + NA_COLS)
    dc = np.clip(kcol[None, :] - qcol[:, None] + NA_COLS - 1, 0, 2 * NA_COLS - 2)
    col_hot = (dc[None] == np.arange(2 * NA_COLS - 1)[:, None, None]).astype(np.float32)
    by_col = jnp.einsum("lhab,bqk->lhaqk", rpb.astype(F32), col_hot, precision=lax.Precision.HIGHEST)
    by_col = jnp.where(col_mask, by_col * LOG2E, NEG)
    masked = jnp.full(by_col.shape[:2] + (1, GRID_W, GRID_W), NEG, F32)
    table = jnp.concatenate([by_col, masked], axis=2)
    return jnp.concatenate([table, table], axis=-1)


def _na_pieces(patterns):
    out = []
    for valid, dr in patterns:
        idx = np.where(valid, dr, NA_DR)
        out.append(tuple(tuple((int(idx[r, 2 * p]), int(idx[r, 2 * p + 1])) for p in range(NA_KROWS // 2))
                         for r in range(NA_QROWS)))
    return tuple(out)


def _softmax_pv(s_list, v_list):
    m = None
    for s in s_list:
        mm = jnp.max(s, axis=-1, keepdims=True)
        m = mm if m is None else jnp.maximum(m, mm)
    num, den = None, None
    for s, v in zip(s_list, v_list):
        p = jnp.exp2(s - m)
        ssum = jnp.sum(p, axis=-1, keepdims=True)
        o = jnp.dot(p.astype(BF16), v, preferred_element_type=F32)
        num = o if num is None else num + o
        den = ssum if den is None else den + ssum
    return num / den


def _na_kernel(q_ref, k_ref, v_ref, cq_ref, ck_ref, cv_ref, tab_ref, cos_ref, sin_ref, y_ref, cy_ref,
               qs_ref, ks_ref, bias_ref, *, starts, types, pieces):
    lane = lax.broadcasted_iota(jnp.int32, (1, HEAD_DIM), 1)
    first = (lane % (HEAD_DIM // 2)) < HEAD_DIM // 4
    scale = HEAD_DIM ** -0.5 * LOG2E

    @pl.when(pl.program_id(1) == 0)
    def _():
        even = lane < GRID_W
        for tp, by_row in enumerate(pieces):
            for r, by_pair in enumerate(by_row):
                for p, (ie, io) in enumerate(by_pair):
                    bias_ref[tp, r * GRID_W:(r + 1) * GRID_W, 2 * p * GRID_W:2 * (p + 1) * GRID_W] = (
                        jnp.where(even, tab_ref[ie], tab_ref[io]))

    def rope(t):
        partner = jnp.where(first, pltpu.roll(t, HEAD_DIM - HEAD_DIM // 4, axis=1),
                            pltpu.roll(t, HEAD_DIM // 4, axis=1))
        return t * cos_ref[...] + partner * sin_ref[...]

    qs_ref[...] = (rope(q_ref[...].astype(F32)) * scale).astype(BF16)
    ks_ref[...] = rope(k_ref[...].astype(F32)).astype(BF16)
    ck = ck_ref[...]
    cv = cv_ref[...]
    qrows = NA_QROWS * GRID_W
    krows = NA_KROWS * GRID_W
    for j, (u, tp) in enumerate(zip(starts, types)):
        qb = qs_ref[j * qrows:(j + 1) * qrows, :]
        kb = ks_ref[u * GRID_W:u * GRID_W + krows, :]
        vb = v_ref[u * GRID_W:u * GRID_W + krows, :]
        s_loc = lax.dot_general(qb, kb, _NT, preferred_element_type=F32) + bias_ref[tp]
        s_ctx = lax.dot_general(qb, ck, _NT, preferred_element_type=F32)
        y_ref[j * qrows:(j + 1) * qrows, :] = _softmax_pv([s_loc, s_ctx], [vb, cv]).astype(y_ref.dtype)
    s = lax.dot_general(cq_ref[...], ck, _NT, preferred_element_type=F32) * scale
    cy_ref[...] = _softmax_pv([s], [cv]).astype(cy_ref.dtype)


def _attention(ub, col0, n, lc, bsz, heads, table, layer, layout):
    starts, types, patterns = layout
    cos, sin = _na_tables(n)
    cb = col0 // HEAD_DIM
    ctx0 = bsz * n // lc

    def lat(k):
        return pl.BlockSpec((n, HEAD_DIM), lambda h, b: (b, cb + k * heads + h))

    def ctx(k):
        return pl.BlockSpec((lc, HEAD_DIM), lambda h, b: (ctx0 + b, cb + k * heads + h))

    tab = pl.BlockSpec((n, HEAD_DIM), lambda h, b: (0, 0))
    return pl.pallas_call(
        functools.partial(_na_kernel, starts=tuple(starts), types=tuple(types), pieces=_na_pieces(patterns)),
        grid=(heads, bsz),
        in_specs=[lat(0), lat(1), lat(2), ctx(0), ctx(1), ctx(2),
                  pl.BlockSpec((None, None) + table.shape[2:], lambda h, b: (layer, h, 0, 0, 0)), tab, tab],
        out_specs=[pl.BlockSpec((n, HEAD_DIM), lambda h, b: (b, h)),
                   pl.BlockSpec((lc, HEAD_DIM), lambda h, b: (b, h))],
        out_shape=[jax.ShapeDtypeStruct((bsz * n, heads * HEAD_DIM), BF16),
                   jax.ShapeDtypeStruct((bsz * lc, heads * HEAD_DIM), BF16)],
        scratch_shapes=[pltpu.VMEM((n, HEAD_DIM), BF16), pltpu.VMEM((n, HEAD_DIM), BF16),
                        pltpu.VMEM((len(patterns), NA_QROWS * GRID_W, NA_KROWS * GRID_W), F32)],
        compiler_params=_params("arbitrary", "arbitrary"), name="attention",
    )(ub, ub, ub, ub, ub, ub, table, jnp.asarray(cos), jnp.asarray(sin))


def _route_kernel(afft_ref, slot_ref, *, n, ne, cap):
    bits = lax.bitcast_convert_type(afft_ref[...], jnp.int32)

    def bisect(i, thr):
        cand = thr | jnp.left_shift(jnp.int32(1), F32_VALUE_BITS - 1 - i)
        cnt = jnp.sum((bits >= cand).astype(F32), axis=1, keepdims=True)
        return jnp.where(cnt >= cap, cand, thr)

    thr = lax.fori_loop(0, F32_VALUE_BITS, bisect, jnp.zeros((ne, 1), jnp.int32))
    above = bits > thr
    tied = bits == thr
    need = cap - jnp.sum(above.astype(F32), axis=1, keepdims=True)

    row = lax.broadcasted_iota(jnp.int32, (LANES, LANES), 0)
    colm = lax.broadcasted_iota(jnp.int32, (LANES, LANES), 1)
    before = (row < colm).astype(BF16)
    ones = jnp.ones((LANES, LANES), BF16)

    def prefix(x):
        xb = x.astype(BF16)
        outs, carry = [], jnp.zeros((ne, LANES), F32)
        for blk in range(n // LANES):
            xs = xb[:, blk * LANES:(blk + 1) * LANES]
            outs.append(jnp.dot(xs, before, preferred_element_type=F32) + carry)
            carry = carry + jnp.dot(xs, ones, preferred_element_type=F32)
        return jnp.concatenate(outs, axis=1)

    chosen = above | (tied & (prefix(tied.astype(F32)) < need))
    slot = prefix(chosen.astype(F32)).astype(jnp.int32)
    slot_ref[...] = jnp.where(chosen, slot, n)


def _route(aff, row0, n, bsz, cap):
    ne = aff.shape[0]
    at = jnp.swapaxes(aff[:, row0:row0 + bsz * n].reshape(ne, bsz, n), 0, 1)
    spec = pl.BlockSpec((bsz * ne, n), lambda i: (0, 0))
    slot = pl.pallas_call(
        functools.partial(_route_kernel, n=n, ne=bsz * ne, cap=cap),
        grid=(1,), in_specs=[spec], out_specs=spec,
        out_shape=jax.ShapeDtypeStruct((bsz * ne, n), jnp.int32),
        compiler_params=_params("arbitrary"), name="expert_route",
    )(at.reshape(bsz * ne, n))
    return at, slot.reshape(bsz, ne, n)


def _gather_kernel(slot_ref, afft_ref, h_ref, xg_ref, g_ref, *, cap, eg):
    e0 = pl.program_id(1) * eg
    want = lax.broadcasted_iota(jnp.int32, (cap, 1), 0)
    sels = [slot_ref[pl.ds(e0 + k, 1), :] == want for k in range(eg)]
    sel = sels[0].astype(BF16) if eg == 1 else jnp.concatenate([s.astype(BF16) for s in sels], axis=0)
    xg = jnp.dot(sel, h_ref[...], preferred_element_type=F32).astype(xg_ref.dtype)
    xg_ref[...] = xg.reshape(xg_ref.shape)
    for k in range(eg):
        g_ref[k] = jnp.sum(jnp.where(sels[k], afft_ref[pl.ds(e0 + k, 1), :], 0.0), axis=1, keepdims=True)


def _gather(slot, afft, h2, row_block0, n, bsz, cap):
    ne = slot.shape[1]
    d = h2.shape[1]
    eg = max(1, min(ne, GATHER_ROWS // cap))
    while ne % eg:
        eg -= 1
    row_spec = pl.BlockSpec((None, ne, n), lambda b, e: (b, 0, 0))
    return pl.pallas_call(
        functools.partial(_gather_kernel, cap=cap, eg=eg),
        grid=(bsz, ne // eg),
        in_specs=[row_spec, row_spec, pl.BlockSpec((n, d), lambda b, e: (row_block0 + b, 0))],
        out_specs=[pl.BlockSpec((eg, cap, d), lambda b, e: (e, b, 0)),
                   pl.BlockSpec((eg, cap, 1), lambda b, e: (e, b, 0))],
        out_shape=[jax.ShapeDtypeStruct((ne, bsz * cap, d), BF16),
                   jax.ShapeDtypeStruct((ne, bsz * cap, 1), F32)],
        compiler_params=_params("arbitrary", "arbitrary"), name="expert_gather",
    )(slot, afft, h2)


def _ffn_kernel(*refs, ns, nf):
    x_refs = refs[0:2 * ns:2]
    g_refs = refs[1:2 * ns:2]
    wg_ref, wu_ref, wd_ref = refs[2 * ns:2 * ns + 3]
    o_refs = refs[2 * ns + 3:3 * ns + 3]
    wgb, wub, wdb = refs[3 * ns + 3:3 * ns + 6]
    acc_refs = refs[3 * ns + 6:]
    f = pl.program_id(1)
    wgb[...] = wg_ref[...].astype(BF16)
    wub[...] = wu_ref[...].astype(BF16)
    wdb[...] = wd_ref[...].astype(BF16)

    def ff_tile(first, last):
        for k in range(ns):
            rows = x_refs[k].shape[0]
            step = min(FFN_ROWS, rows)
            for r0 in range(0, rows, step):
                rs = slice(r0, r0 + step)
                x = x_refs[k][rs, :]
                hg = jnp.dot(x, wgb[...], preferred_element_type=F32)
                hu = jnp.dot(x, wub[...], preferred_element_type=F32)
                hid = (hg * _sigmoid(hg) * hu).astype(BF16)
                part = jnp.dot(hid, wdb[...], preferred_element_type=F32)
                if not first:
                    part = acc_refs[k][rs, :] + part
                if last:
                    o_refs[k][rs, :] = (part * g_refs[k][rs, :]).astype(o_refs[k].dtype)
                else:
                    acc_refs[k][rs, :] = part

    if nf == 1:
        ff_tile(True, True)
    else:
        pl.when(f == 0)(functools.partial(ff_tile, True, False))
        pl.when(f == nf - 1)(functools.partial(ff_tile, False, True))
        if nf > 2:
            pl.when((f > 0) & (f < nf - 1))(functools.partial(ff_tile, False, False))


def _expert_ffn(xgs, gates, w_gate, w_up, w_down, layer):
    ns = len(xgs)
    _, ne, d, ff = w_gate.shape
    tf = _lane_tile(ff, FF_TILE)
    nf = ff // tf
    in_specs, out_specs, out_shape, acc_scr, args = [], [], [], [], []
    for k in range(ns):
        rows = xgs[k].shape[1]
        in_specs.append(pl.BlockSpec((None, rows, d), lambda e, f: (e, 0, 0)))
        in_specs.append(pl.BlockSpec((None, rows, 1), lambda e, f: (e, 0, 0)))
        out_specs.append(pl.BlockSpec((None, rows, d), lambda e, f: (e, 0, 0)))
        out_shape.append(jax.ShapeDtypeStruct((ne, rows, d), BF16))
        acc_scr.append(pltpu.VMEM((rows, d), F32))
        args += [xgs[k], gates[k]]
    in_specs += [pl.BlockSpec((None, None, d, tf), lambda e, f: (layer, e, 0, f)),
                 pl.BlockSpec((None, None, d, tf), lambda e, f: (layer, e, 0, f)),
                 pl.BlockSpec((None, None, tf, d), lambda e, f: (layer, e, f, 0))]
    return pl.pallas_call(
        functools.partial(_ffn_kernel, ns=ns, nf=nf),
        grid=(ne, nf),
        in_specs=in_specs, out_specs=out_specs, out_shape=out_shape,
        scratch_shapes=[pltpu.VMEM((d, tf), BF16), pltpu.VMEM((d, tf), BF16), pltpu.VMEM((tf, d), BF16)] + acc_scr,
        compiler_params=_params("arbitrary", "arbitrary"), name="expert_ffn",
    )(*args, w_gate, w_up, w_down)


def _scatter_kernel(slot_ref, y_ref, x_ref, mod_ref, o_ref, *, ne, cap):
    sl = slot_ref[...]
    want = lax.broadcasted_iota(jnp.int32, (1, cap), 1)
    acc = None
    for e in range(ne):
        sel = (sl[:, e:e + 1] == want).astype(BF16)
        p = jnp.dot(sel, y_ref[e], preferred_element_type=F32)
        acc = p if acc is None else acc + p
    o_ref[...] = x_ref[...] + mod_ref[5:6, :] * acc


def _scatter(slot, y, x, mod, layer, mod_row0, row0, n, bsz, cap, tn):
    ne = slot.shape[1]
    d = x.shape[1]
    tm = min(SCATTER_ROWS, n)
    slot_t = jnp.swapaxes(slot, 1, 2)
    xb0 = row0 // tm
    per_seq = n // tm
    return pl.pallas_call(
        functools.partial(_scatter_kernel, ne=ne, cap=cap),
        grid=(bsz, d // tn, per_seq),
        in_specs=[pl.BlockSpec((None, tm, ne), lambda b, j, i: (b, i, 0)),
                  pl.BlockSpec((ne, cap, tn), lambda b, j, i: (0, b, j)),
                  pl.BlockSpec((tm, tn), lambda b, j, i: (xb0 + b * per_seq + i, j)),
                  pl.BlockSpec((None, None, 6, tn), lambda b, j, i: (layer, mod_row0(b), 0, j))],
        out_specs=pl.BlockSpec((tm, tn), lambda b, j, i: (b * per_seq + i, j)),
        out_shape=jax.ShapeDtypeStruct((bsz * n, d), F32),
        compiler_params=_params("arbitrary", "arbitrary", "arbitrary"), name="expert_scatter",
    )(slot_t, y, x, mod)


def kernel(x, c, ctx, c_ctx, w_mod, b_mod, g_norm1, w_in, lb_param, g_hgrn, w_pool, pool_scale, rpb,
           w_branch, w_out, g_norm2, w_router, w_gate_e, w_up_e, w_down_e, g_final):
    bsz, n, d = x.shape
    lc = ctx.shape[1]
    depth = w_mod.shape[0]
    width = lb_param.shape[2]
    heads = width // HEAD_DIM
    ne = w_router.shape[2]
    n_lat = bsz * n
    n_ctx = bsz * lc
    total = n_lat + n_ctx
    assert bsz + 1 <= 8 and n % ROW_TILE == 0 and lc % ROW_TILE == 0 and n_lat % lc == 0
    assert w_pool.shape[-1] * 4 == width and w_branch.shape[2] == width
    tm = _row_tile(n, n_ctx)
    tn = _lane_tile(d, MERGE_COL_TILE)
    tn_wide = _lane_tile(d, WIDE_COL_TILE)

    lb_all = jnp.cumsum(jax.nn.softmax(lb_param.astype(F32), axis=1), axis=1)
    lb_all = lb_all - lb_all[:, :1]
    lbp_all = jnp.stack([jnp.log(lb_all), jnp.log1p(-lb_all)], axis=2) * LOG2E
    lbp_all = lbp_all.reshape(2, depth, 2, heads, LANES).transpose(0, 1, 3, 2, 4)

    c8 = jnp.concatenate([c, c_ctx[None], jnp.zeros((8 - bsz - 1, d), F32)], axis=0)
    mod = _modulation(c8, w_mod, b_mod).reshape(depth, 8, 6, d)

    cm3, pair_masks, upper = _hgrn_constants()
    hgrn_consts = (jnp.asarray(cm3, BF16), jnp.asarray(pair_masks, F32), jnp.asarray(upper, F32))
    na_layout = _na_block_layout(n // GRID_W)
    na_bias = _na_bias_table(rpb)
    lanes_e = -(-ne // LANES) * LANES
    w_router_p = jnp.swapaxes(jnp.pad(w_router, ((0, 0), (0, 0), (0, lanes_e - ne))), 1, 2)

    x_lat = x.reshape(n_lat, d)
    x_ctx = ctx.reshape(n_ctx, d)
    zero_state = jnp.zeros((bsz, heads, LANES, LANES), F32)
    cap = EC_CAPACITY * n // ne
    cap_c = EC_CAPACITY * lc // ne
    a_cols = 5 * width
    b_cols = w_in.shape[2] - a_cols
    tn_in = _lane_tile(math.gcd(a_cols, b_cols), WIDE_COL_TILE)
    gate_col0 = 4 * width
    rt_all = _Rows(tm, n, bsz, n_lat, total)
    rt_lat = _Rows(tm, n, bsz, n_lat, n_lat)
    tm_norm = min(NORM_ROWS, tm)
    rn_all = _Rows(tm_norm, n, bsz, n_lat, total)
    rn_lat = _Rows(tm_norm, n, bsz, n_lat, n_lat)
    tm_out = max(tm // 2, ROW_TILE)
    ro_all = _Rows(tm_out, n, bsz, n_lat, total)
    ro_lat = _Rows(tm_out, n, bsz, n_lat, n_lat)

    for l in range(depth):
        last = l == depth - 1
        rt, rn, ro = (rt_lat, rn_lat, ro_lat) if last else (rt_all, rn_all, ro_all)
        gain_h = g_hgrn[l].reshape(heads, 1, LANES)

        h = _norm1(x_lat, x_ctx, g_norm1[l], mod, l, rn_all)
        ua = _matmul(h, w_in, l, 0, a_cols, F32, tm, tn_in)
        ub = _matmul(h, w_in, l, a_cols, b_cols, BF16, tm, tn_in)

        cy_a, s_f, s_b = _hgrn(ua, n_lat // lc, lc, bsz, heads, lbp_all[:, l], gain_h, zero_state, zero_state,
                               hgrn_consts)
        y_a, _, _ = _hgrn(ua, 0, n, bsz, heads, lbp_all[:, l], gain_h, s_f, s_b, hgrn_consts)
        y_b = _pool(ub, 0, n, bsz, w_pool, pool_scale, l)
        y_c, cy_c = _attention(ub, width, n, lc, bsz, heads, na_bias, l, na_layout)
        ys_lat = (y_a, y_b, y_c)
        ys_ctx = ys_lat if last else (cy_a, _pool(ub, n_lat // lc, lc, bsz, w_pool, pool_scale, l), cy_c)

        merged = _merge(ys_lat, ys_ctx, ub, gate_col0, w_branch, l, rt, tn)
        x_mid = _out_proj(merged, w_out, l, x_lat, x_ctx, mod, ro, tn_wide)

        h2, aff = _norm2(x_mid, g_norm2[l], mod, l, rn, w_router_p, ne)
        afft, slot = _route(aff, 0, n, bsz, cap)
        xg, gate = _gather(slot, afft, h2, 0, n, bsz, cap)
        xgs, gates = [xg], [gate]
        if not last:
            afft_c, slot_c = _route(aff, n_lat, lc, bsz, cap_c)
            xg_c, gate_c = _gather(slot_c, afft_c, h2, n_lat // lc, lc, bsz, cap_c)
            xgs, gates = xgs + [xg_c], gates + [gate_c]
        ys = _expert_ffn(xgs, gates, w_gate_e, w_up_e, w_down_e, l)
        x_lat = _scatter(slot, ys[0], x_mid, mod, l, lambda b: b, 0, n, bsz, cap, tn_wide)
        if not last:
            x_ctx = _scatter(slot_c, ys[1], x_mid, mod, l, lambda b: bsz, n_lat, lc, bsz, cap_c, tn_wide)

    return _final_norm(x_lat, g_final, tm_norm).reshape(bsz, n, d)
```
